```python
import math
import jax, jax.numpy as jnp
from jax import lax
import numpy as np

D_MODEL = 1024
BATCH = 2
SEQ = 16384
DEPTH = 2

GRID_W = 64
CTX_LEN = 256
EPS = 1e-6
N_HEADS = 8
N_KV_HEADS = 2
GQA_GROUP = N_HEADS // N_KV_HEADS
HEAD_DIM = 64
AXIS_DIM = HEAD_DIM // 2
ATT_W = N_HEADS * HEAD_DIM
KV_W = N_KV_HEADS * HEAD_DIM
ATT_SCALE = HEAD_DIM ** -0.5
ROPE_THETA = 10000.0
Q_BLOCK = 128
SC_W = 512
CONV_K = 3
HY_W = 512
HY_ORDER = 2
HY_BANDS = 16
HY_EMB = 1 + 2 * HY_BANDS
HY_FH = 64
HY_FAST_DECAY = 0.3
HY_SLOW_DECAY = 1.5
HY_DECAY_TARGET = 1e-2
PEER_HEADS = 8
PEER_NKEYS = 128
PEER_N = PEER_NKEYS * PEER_NKEYS
PEER_DK = 128
PEER_TOPK = 16
PEER_CHUNK = 128
Q0 = 0
K0 = Q0 + ATT_W
V0 = K0 + KV_W
SC0 = V0 + KV_W
HY0 = SC0 + 3 * SC_W
GT0 = HY0 + (HY_ORDER + 1) * HY_W
IN_W = GT0 + 3 * D_MODEL

kernel_name = 'hybrid_attn_conv_hyena_peer_dit'


def _rmsnorm(x, g):
    xf = x.astype(jnp.float32)
    y = xf * lax.rsqrt(jnp.mean(xf * xf, axis=-1, keepdims=True) + EPS)
    return (y * g.astype(jnp.float32)).astype(x.dtype)


def _dwconv3(u, w):
    up = jnp.pad(u, ((0, 0), (1, 1), (0, 0)))
    return up[:, :-2] * w[0] + up[:, 1:-1] * w[1] + up[:, 2:] * w[2]


def _axial_rope(x):
    L = x.shape[1]
    pos = jnp.arange(L, dtype=jnp.int32)
    row = (pos // GRID_W).astype(jnp.float32)
    col = (pos % GRID_W).astype(jnp.float32)
    inv = jnp.power(ROPE_THETA, -jnp.arange(0, AXIS_DIM, 2, dtype=jnp.float32) / AXIS_DIM)

    def rot(xa, p):
        ang = p[:, None] * inv[None, :]
        cos = jnp.cos(ang)[None, :, None, :]
        sin = jnp.sin(ang)[None, :, None, :]
        x1, x2 = jnp.split(xa.astype(jnp.float32), 2, axis=-1)
        return jnp.concatenate([x1 * cos - x2 * sin, x2 * cos + x1 * sin], axis=-1)

    xr, xc = jnp.split(x, 2, axis=-1)
    return jnp.concatenate([rot(xr, row), rot(xc, col)], axis=-1).astype(x.dtype)


def _attend(q, k, v):
    s = jnp.einsum('bqhgd,bkhd->bhgqk', q, k, preferred_element_type=jnp.float32) * ATT_SCALE
    p = jax.nn.softmax(s, axis=-1).astype(v.dtype)
    return jnp.einsum('bhgqk,bkhd->bqhgd', p, v)


def _latent_attention(q, k_all, v_all):
    B, S = q.shape[:2]
    nb = S // Q_BLOCK
    qb = q.reshape(B, nb, Q_BLOCK, N_KV_HEADS, GQA_GROUP, HEAD_DIM).transpose(1, 0, 2, 3, 4, 5)
    ob = lax.map(lambda qi: _attend(qi, k_all, v_all), qb)
    return ob.transpose(1, 0, 2, 3, 4, 5).reshape(B, S, ATT_W)


def _short_conv(p, w):
    bg, cg, xs = jnp.split(p, 3, axis=-1)
    return bg * _dwconv3(cg * xs, w)


def _hyena_filter_fft(L, w1, b1, f1, w2, b2, f2, w3, decay):
    t = jnp.linspace(0.0, 1.0, L, dtype=jnp.float32)[:, None]
    w = 2.0 * math.pi * jnp.arange(L, dtype=jnp.float32)[:, None] / L
    bands = jnp.linspace(1e-4, HY_BANDS - 1, HY_BANDS, dtype=jnp.float32)[None, :]
    z = jnp.concatenate([t, jnp.cos(bands * w), -jnp.sin(bands * w)], axis=-1)
    h = jnp.sin(f1.astype(jnp.float32) * (z @ w1.astype(jnp.float32) + b1.astype(jnp.float32)))
    h = jnp.sin(f2.astype(jnp.float32) * (h @ w2.astype(jnp.float32) + b2.astype(jnp.float32)))
    h = (h @ w3.astype(jnp.float32)).reshape(L, HY_ORDER, 2, HY_W)
    window = jnp.exp(-t[:, :, None] * jnp.abs(decay.astype(jnp.float32)))
    h = h * window[:, :, None, :]
    h = h / jnp.sum(jnp.abs(h), axis=(0, 2), keepdims=True)
    hf, hb = h[:, :, 0], h[:, :, 1]
    k = jnp.concatenate([hf, jnp.zeros_like(hf[:1]), hb[:0:-1]], axis=0)
    return jnp.fft.rfft(k, axis=0)


def _long_conv(u, kf, bias):
    L = u.shape[1]
    uf = u.astype(jnp.float32)
    y = jnp.fft.irfft(jnp.fft.rfft(uf, n=2 * L, axis=1) * kf[None], n=2 * L, axis=1)[:, :L]
    return (y + uf * bias.astype(jnp.float32)).astype(u.dtype)


def _hyena(p, conv_w, conv_b, kf, bias):
    u = _dwconv3(p, conv_w) + conv_b
    v, x1, x2 = jnp.split(u, HY_ORDER + 1, axis=-1)
    z = v
    for o, gate in enumerate((x1, x2)):
        z = gate * _long_conv(z, kf[:, o], bias[o])
    return z


def _merge(gate_pre, y_att, y_sc, y_hy, w_att, w_sc, w_hy, w_o):
    g_att, g_sc, g_hy = jnp.split(jax.nn.sigmoid(gate_pre), 3, axis=-1)
    m = g_att * (y_att @ w_att) + g_sc * (y_sc @ w_sc) + g_hy * (y_hy @ w_hy)
    return m @ w_o


def _peer(tok, wq, subkeys, u_tab, v_tab):
    T, D = tok.shape

    def chunk(hc):
        q = (hc @ wq).reshape(PEER_CHUNK, PEER_HEADS, 2, PEER_DK // 2)
        s = jnp.einsum('chpd,hpnd->chpn', q, subkeys, preferred_element_type=jnp.float32)
        s1, i1 = lax.top_k(s[:, :, 0], PEER_TOPK)
        s2, i2 = lax.top_k(s[:, :, 1], PEER_TOPK)
        cand = (s1[..., :, None] + s2[..., None, :]).reshape(PEER_CHUNK, PEER_HEADS, PEER_TOPK * PEER_TOPK)
        sc, ic = lax.top_k(cand, PEER_TOPK)
        e = (jnp.take_along_axis(i1, ic // PEER_TOPK, axis=-1) * PEER_NKEYS
             + jnp.take_along_axis(i2, ic % PEER_TOPK, axis=-1))
        g = jax.nn.softmax(sc, axis=-1)
        a = jax.nn.gelu(jnp.einsum('chkd,cd->chk', u_tab[e], hc, preferred_element_type=jnp.float32),
                        approximate=False)
        return jnp.einsum('chk,chkd->cd', (g * a).astype(v_tab.dtype), v_tab[e])

    out = lax.map(chunk, tok.reshape(T // PEER_CHUNK, PEER_CHUNK, D))
    return out.reshape(T, D)


def setup_inputs(seed: int = 0) -> dict:
    key = jax.random.key(seed)
    ks = list(jax.random.split(key, 40))
    cnt = [0]

    def nrm(shape, scale):
        k = ks[cnt[0]]
        cnt[0] += 1
        return jax.random.normal(k, shape, jnp.float32) * scale

    L = DEPTH
    D = D_MODEL
    dec = jnp.linspace(math.log(HY_DECAY_TARGET) / HY_SLOW_DECAY, math.log(HY_DECAY_TARGET) / HY_FAST_DECAY,
                       HY_W, dtype=jnp.float32)
    return {
        'x': nrm((BATCH, SEQ, D), 1.0),
        'c': nrm((BATCH, D), 1.0),
        'ctx': nrm((BATCH, CTX_LEN, D), 1.0),
        'c_ctx': nrm((D,), 1.0),
        'w_mod': nrm((L, D, 6 * D), D ** -0.5),
        'b_mod': nrm((L, 6 * D), 0.02),
        'g_norm1': 1.0 + nrm((L, D), 0.02),
        'g_norm2': 1.0 + nrm((L, D), 0.02),
        'w_in': nrm((L, D, IN_W), D ** -0.5),
        'q_gain': 1.0 + nrm((L, HEAD_DIM), 0.02),
        'k_gain': 1.0 + nrm((L, HEAD_DIM), 0.02),
        'sc_conv_w': nrm((L, CONV_K, SC_W), CONV_K ** -0.5),
        'hy_conv_w': nrm((L, CONV_K, (HY_ORDER + 1) * HY_W), CONV_K ** -0.5),
        'hy_conv_b': nrm((L, (HY_ORDER + 1) * HY_W), 0.02),
        'hy_w1': nrm((L, HY_EMB, HY_FH), HY_EMB ** -0.5),
        'hy_b1': nrm((L, HY_FH), 0.02),
        'hy_f1': 1.0 + nrm((L, HY_FH), 0.02),
        'hy_w2': nrm((L, HY_FH, HY_FH), HY_FH ** -0.5),
        'hy_b2': nrm((L, HY_FH), 0.02),
        'hy_f2': 1.0 + nrm((L, HY_FH), 0.02),
        'hy_w3': nrm((L, HY_FH, HY_ORDER * 2 * HY_W), HY_FH ** -0.5),
        'hy_decay': dec[None, None, :] + nrm((L, HY_ORDER, HY_W), 0.1),
        'hy_bias': nrm((L, HY_ORDER, HY_W), 1.0),
        'w_br_att': nrm((L, ATT_W, D), ATT_W ** -0.5),
        'w_br_sc': nrm((L, SC_W, D), SC_W ** -0.5),
        'w_br_hy': nrm((L, HY_W, D), HY_W ** -0.5),
        'w_out': nrm((L, D, D), D ** -0.5),
        'peer_wq': nrm((L, D, PEER_HEADS * PEER_DK), D ** -0.5),
        'peer_subkeys': nrm((L, PEER_HEADS, 2, PEER_NKEYS, PEER_DK // 2), (PEER_DK // 2) ** -0.5),
        'peer_u': nrm((L, PEER_N, D), D ** -0.5),
        'peer_v': nrm((L, PEER_N, D), PEER_HEADS ** -0.5),
        'g_final': 1.0 + nrm((D,), 0.02),
    }


def reference(x, c, ctx, c_ctx, w_mod, b_mod, g_norm1, g_norm2, w_in, q_gain, k_gain, sc_conv_w,
              hy_conv_w, hy_conv_b, hy_w1, hy_b1, hy_f1, hy_w2, hy_b2, hy_f2, hy_w3, hy_decay, hy_bias,
              w_br_att, w_br_sc, w_br_hy, w_out, peer_wq, peer_subkeys, peer_u, peer_v, g_final):
    B, S, D = x.shape
    Lc = ctx.shape[1]
    cond_lat = jax.nn.silu(c)
    cond_ctx = jax.nn.silu(c_ctx)
    for l in range(DEPTH):
        need_ctx = l < DEPTH - 1
        mod_lat = (cond_lat @ w_mod[l] + b_mod[l])[:, None, :]
        mod_ctx = cond_ctx @ w_mod[l] + b_mod[l]
        sh1, sc1, gt1, sh2, sc2, gt2 = jnp.split(mod_lat, 6, axis=-1)
        csh1, csc1, cgt1, csh2, csc2, cgt2 = jnp.split(mod_ctx, 6, axis=-1)
        hy_params = (hy_w1[l], hy_b1[l], hy_f1[l], hy_w2[l], hy_b2[l], hy_f2[l], hy_w3[l], hy_decay[l])

        h = _rmsnorm(x, g_norm1[l]) * (1.0 + sc1) + sh1
        hc = _rmsnorm(ctx, g_norm1[l]) * (1.0 + csc1) + csh1
        p = h @ w_in[l]
        q = _axial_rope(_rmsnorm(p[..., Q0:K0].reshape(B, S, N_HEADS, HEAD_DIM), q_gain[l]))
        k = _axial_rope(_rmsnorm(p[..., K0:V0].reshape(B, S, N_KV_HEADS, HEAD_DIM), k_gain[l]))
        v = p[..., V0:SC0].reshape(B, S, N_KV_HEADS, HEAD_DIM)
        if need_ctx:
            pc = hc @ w_in[l]
            kvc = pc[..., K0:SC0]
        else:
            kvc = hc @ w_in[l][:, K0:SC0]
        kc = _rmsnorm(kvc[..., :KV_W].reshape(B, Lc, N_KV_HEADS, HEAD_DIM), k_gain[l])
        vc = kvc[..., KV_W:].reshape(B, Lc, N_KV_HEADS, HEAD_DIM)
        k_all = jnp.concatenate([kc, k], axis=1)
        v_all = jnp.concatenate([vc, v], axis=1)

        y_att = _latent_attention(q, k_all, v_all)
        y_sc = _short_conv(p[..., SC0:HY0], sc_conv_w[l])
        y_hy = _hyena(p[..., HY0:GT0], hy_conv_w[l], hy_conv_b[l], _hyena_filter_fft(S, *hy_params), hy_bias[l])
        x = x + gt1 * _merge(p[..., GT0:], y_att, y_sc, y_hy, w_br_att[l], w_br_sc[l], w_br_hy[l], w_out[l])

        if need_ctx:
            qc = _rmsnorm(pc[..., Q0:K0].reshape(B, Lc, N_HEADS, HEAD_DIM), q_gain[l])
            yc_att = _attend(qc.reshape(B, Lc, N_KV_HEADS, GQA_GROUP, HEAD_DIM), kc, vc).reshape(B, Lc, ATT_W)
            yc_sc = _short_conv(pc[..., SC0:HY0], sc_conv_w[l])
            yc_hy = _hyena(pc[..., HY0:GT0], hy_conv_w[l], hy_conv_b[l], _hyena_filter_fft(Lc, *hy_params),
                           hy_bias[l])
            ctx = ctx + cgt1 * _merge(pc[..., GT0:], yc_att, yc_sc, yc_hy, w_br_att[l], w_br_sc[l], w_br_hy[l],
                                      w_out[l])

        h2 = _rmsnorm(x, g_norm2[l]) * (1.0 + sc2) + sh2
        if need_ctx:
            h2c = _rmsnorm(ctx, g_norm2[l]) * (1.0 + csc2) + csh2
            tok = jnp.concatenate([h2.reshape(B * S, D), h2c.reshape(B * Lc, D)], axis=0)
            f = _peer(tok, peer_wq[l], peer_subkeys[l], peer_u[l], peer_v[l])
            x = x + gt2 * f[:B * S].reshape(B, S, D)
            ctx = ctx + cgt2 * f[B * S:].reshape(B, Lc, D)
        else:
            f = _peer(h2.reshape(B * S, D), peer_wq[l], peer_subkeys[l], peer_u[l], peer_v[l])
            x = x + gt2 * f.reshape(B, S, D)
    return _rmsnorm(x, g_final)
```

```python
import functools
import math

import jax
import jax.numpy as jnp
from jax import lax
from jax.experimental import pallas as pl
from jax.experimental.pallas import tpu as pltpu

F32 = jnp.float32
BF16 = jnp.bfloat16

DEPTH = 2
GRID_W = 64
EPS = 1e-6
N_HEADS = 8
N_KV_HEADS = 2
GQA_GROUP = N_HEADS // N_KV_HEADS
HEAD_DIM = 64
AXIS_DIM = HEAD_DIM // 2
ATT_W = N_HEADS * HEAD_DIM
KV_W = N_KV_HEADS * HEAD_DIM
ATT_SCALE = HEAD_DIM ** -0.5
ROPE_THETA = 10000.0
SC_W = 512
HY_W = 512
HY_ORDER = 2
HY_BANDS = 16
PEER_HEADS = 8
PEER_NKEYS = 128
PEER_DK = 128
PEER_TOPK = 16
REF_GT0 = ATT_W + 2 * KV_W + 3 * SC_W + (HY_ORDER + 1) * HY_W
GT0 = 0
Q0 = GT0 + 3 * 1024
K0 = Q0 + ATT_W
V0 = K0 + KV_W
SC0 = V0 + KV_W
HY0 = SC0 + 3 * SC_W
IN_W = HY0 + (HY_ORDER + 1) * HY_W

VMEM_LIMIT = 56 * 1024 * 1024


def _cparams(sem):
    return pltpu.CompilerParams(dimension_semantics=sem, vmem_limit_bytes=VMEM_LIMIT)


def _in_proj_kernel(x_ref, g_ref, sc_ref, sh_ref, w_ref, o_ref, h_scr):
    @pl.when(pl.program_id(1) == 0)
    def _():
        x = x_ref[...]
        ms = jnp.mean(x * x, axis=-1, keepdims=True)
        y = x * lax.rsqrt(ms + EPS) * g_ref[...]
        h_scr[...] = (y * (1.0 + sc_ref[0]) + sh_ref[0]).astype(BF16)

    o_ref[...] = jnp.dot(h_scr[...], w_ref[...], preferred_element_type=F32).astype(o_ref.dtype)


def _in_proj(x2d, g, sc, sh, w, rows_per_batch, tm, tn):
    T, D = x2d.shape
    N = w.shape[1]
    tpb = rows_per_batch // tm
    return pl.pallas_call(
        _in_proj_kernel,
        grid=(T // tm, N // tn),
        in_specs=[
            pl.BlockSpec((tm, D), lambda i, j: (i, 0)),
            pl.BlockSpec((1, D), lambda i, j: (0, 0)),
            pl.BlockSpec((1, 1, D), lambda i, j: (i // tpb, 0, 0)),
            pl.BlockSpec((1, 1, D), lambda i, j: (i // tpb, 0, 0)),
            pl.BlockSpec((D, tn), lambda i, j: (0, j)),
        ],
        out_specs=pl.BlockSpec((tm, tn), lambda i, j: (i, j)),
        out_shape=jax.ShapeDtypeStruct((T, N), BF16),
        scratch_shapes=[pltpu.VMEM((tm, D), BF16)],
        compiler_params=_cparams(("parallel", "arbitrary")),
        name="in_proj",
    )(x2d, g, sc, sh, w)


def _attn_kernel(q_ref, kT_ref, v_ref, o_ref, m_scr, acc_scr, *, nkb):
    q = q_ref[0, 0, 0]
    m_scr[...] = jnp.full(m_scr.shape, -jnp.inf, F32)
    acc_scr[...] = jnp.zeros(acc_scr.shape, F32)

    def body(j, carry):
        s = jnp.dot(q, kT_ref[0, 0, j], preferred_element_type=F32)
        m_prev = m_scr[...]
        m_new = jnp.maximum(m_prev, jnp.max(s, axis=-1, keepdims=True))
        p = jnp.exp(s - m_new).astype(BF16)
        alpha = jnp.exp(m_prev - m_new)
        acc_scr[...] = acc_scr[...] * alpha + jnp.dot(p, v_ref[0, 0, j], preferred_element_type=F32)
        m_scr[...] = m_new
        return carry

    lax.fori_loop(0, nkb, body, 0)
    acc = acc_scr[...]
    o_ref[0, 0, 0] = (acc[:, :HEAD_DIM] / acc[:, HEAD_DIM:HEAD_DIM + 1]).astype(o_ref.dtype)


def _attention(q, k, v, tq, tk):
    B, Lq, _ = q.shape
    Lk = k.shape[1]
    nqb, nkb = Lq // tq, Lk // tk
    R = GQA_GROUP * tq
    qb = q.reshape(B, nqb, tq, N_KV_HEADS, GQA_GROUP, HEAD_DIM).transpose(0, 3, 1, 4, 2, 5)
    qb = qb.reshape(B, N_KV_HEADS, nqb, R, HEAD_DIM)
    kT = k.reshape(B, nkb, tk, N_KV_HEADS, HEAD_DIM).transpose(0, 3, 1, 4, 2)
    vb = v.reshape(B, nkb, tk, N_KV_HEADS, HEAD_DIM).transpose(0, 3, 1, 2, 4)
    ones = jnp.ones(vb.shape[:-1] + (1,), BF16)
    zeros = jnp.zeros(vb.shape[:-1] + (HEAD_DIM - 1,), BF16)
    vb = jnp.concatenate([vb, ones, zeros], axis=-1)
    ob = pl.pallas_call(
        functools.partial(_attn_kernel, nkb=nkb),
        grid=(B, N_KV_HEADS, nqb),
        in_specs=[
            pl.BlockSpec((1, 1, 1, R, HEAD_DIM), lambda b, h, i: (b, h, i, 0, 0)),
            pl.BlockSpec((1, 1, nkb, HEAD_DIM, tk), lambda b, h, i: (b, h, 0, 0, 0)),
            pl.BlockSpec((1, 1, nkb, tk, 2 * HEAD_DIM), lambda b, h, i: (b, h, 0, 0, 0)),
        ],
        out_specs=pl.BlockSpec((1, 1, 1, R, HEAD_DIM), lambda b, h, i: (b, h, i, 0, 0)),
        out_shape=jax.ShapeDtypeStruct((B, N_KV_HEADS, nqb, R, HEAD_DIM), BF16),
        scratch_shapes=[pltpu.VMEM((R, 1), F32), pltpu.VMEM((R, 2 * HEAD_DIM), F32)],
        compiler_params=_cparams(("parallel", "parallel", "arbitrary")),
        name="attention",
    )(qb, kT, vb)
    ob = ob.reshape(B, N_KV_HEADS, nqb, GQA_GROUP, tq, HEAD_DIM).transpose(0, 2, 4, 1, 3, 5)
    return ob.reshape(B, Lq, ATT_W)


def _merge_kernel(x_ref, ya_ref, ys_ref, yh_ref, ga_ref, gs_ref, gh_ref, gt_ref, g2_ref, sc_ref, sh_ref,
                  wa_ref, ws_ref, wh_ref, wo_ref, xo_ref, h2_ref):
    def br(y_ref, g_ref, w_ref):
        gate = jax.nn.sigmoid(g_ref[...].astype(F32))
        return gate * jnp.dot(y_ref[...], w_ref[...], preferred_element_type=F32)

    m = br(ya_ref, ga_ref, wa_ref) + br(ys_ref, gs_ref, ws_ref) + br(yh_ref, gh_ref, wh_ref)
    o = jnp.dot(m.astype(BF16), wo_ref[...], preferred_element_type=F32)
    xn = x_ref[...] + gt_ref[0] * o
    xo_ref[...] = xn
    ms = jnp.mean(xn * xn, axis=-1, keepdims=True)
    y = xn * lax.rsqrt(ms + EPS) * g2_ref[...]
    h2_ref[...] = (y * (1.0 + sc_ref[0]) + sh_ref[0]).astype(BF16)


def _merge(x2d, ya, ys, yh, p, gt1, g2, sc2, sh2, wa, ws, wh, wo, rows_per_batch, tm):
    T, D = x2d.shape
    tpb = rows_per_batch // tm
    gblk = GT0 // D
    row = lambda i: (i, 0)
    mod = lambda i: (i // tpb, 0, 0)
    full = lambda i: (0, 0)
    return pl.pallas_call(
        _merge_kernel,
        grid=(T // tm,),
        in_specs=[
            pl.BlockSpec((tm, D), row),
            pl.BlockSpec((tm, ATT_W), row),
            pl.BlockSpec((tm, SC_W), row),
            pl.BlockSpec((tm, HY_W), row),
            pl.BlockSpec((tm, D), lambda i: (i, gblk)),
            pl.BlockSpec((tm, D), lambda i: (i, gblk + 1)),
            pl.BlockSpec((tm, D), lambda i: (i, gblk + 2)),
            pl.BlockSpec((1, 1, D), mod),
            pl.BlockSpec((1, D), full),
            pl.BlockSpec((1, 1, D), mod),
            pl.BlockSpec((1, 1, D), mod),
            pl.BlockSpec((ATT_W, D), full),
            pl.BlockSpec((SC_W, D), full),
            pl.BlockSpec((HY_W, D), full),
            pl.BlockSpec((D, D), full),
        ],
        out_specs=[pl.BlockSpec((tm, D), row), pl.BlockSpec((tm, D), row)],
        out_shape=[jax.ShapeDtypeStruct((T, D), F32), jax.ShapeDtypeStruct((T, D), BF16)],
        compiler_params=_cparams(("parallel",)),
        name="merge",
    )(x2d, ya, ys, yh, p, p, p, gt1, g2, sc2, sh2, wa, ws, wh, wo)


def _peer_kernel(hT_ref, s2_ref, e2_ref, c_ref, e1_ref, u_ref, vT_ref, o_ref, gw_scr, *, nb):
    @pl.when(pl.program_id(1) == 0)
    def _():
        o_ref[...] = jnp.zeros(o_ref.shape, F32)

    hT = hT_ref[...]
    for ii in range(nb):
        a = jnp.dot(u_ref[ii * PEER_NKEYS:(ii + 1) * PEER_NKEYS, :], hT, preferred_element_type=F32)
        w = jnp.zeros(a.shape, F32)
        for h in range(PEER_HEADS):
            keep = s2_ref[h] >= c_ref[h, 0, ii:ii + 1, :]
            w = w + jnp.where(keep, e2_ref[h], 0.0) * e1_ref[h, 0, ii:ii + 1, :]
        act = 0.5 * a * (1.0 + lax.erf(a * (2.0 ** -0.5)))
        gw_scr[ii * PEER_NKEYS:(ii + 1) * PEER_NKEYS, :] = (act * w).astype(BF16)
    o_ref[...] += jnp.dot(vT_ref[...], gw_scr[...], preferred_element_type=F32)


def _peer_dense(hT, s2, e2, c, e1, u, vT, tc, nb):
    D, T = hT.shape
    N = u.shape[0]
    eb = nb * PEER_NKEYS
    return pl.pallas_call(
        functools.partial(_peer_kernel, nb=nb),
        grid=(T // tc, N // eb),
        in_specs=[
            pl.BlockSpec((D, tc), lambda t, e: (0, t)),
            pl.BlockSpec((PEER_HEADS, PEER_NKEYS, tc), lambda t, e: (0, 0, t)),
            pl.BlockSpec((PEER_HEADS, PEER_NKEYS, tc), lambda t, e: (0, 0, t)),
            pl.BlockSpec((PEER_HEADS, 1, nb, tc), lambda t, e: (0, e, 0, t)),
            pl.BlockSpec((PEER_HEADS, 1, nb, tc), lambda t, e: (0, e, 0, t)),
            pl.BlockSpec((eb, D), lambda t, e: (e, 0)),
            pl.BlockSpec((D, eb), lambda t, e: (0, e)),
        ],
        out_specs=pl.BlockSpec((D, tc), lambda t, e: (0, t)),
        out_shape=jax.ShapeDtypeStruct((D, T), F32),
        scratch_shapes=[pltpu.VMEM((eb, tc), BF16)],
        compiler_params=_cparams(("parallel", "arbitrary")),
        name="peer_dense",
    )(hT, s2, e2, c, e1, u, vT)


def _rms_heads(x, gain):
    xf = x.astype(F32)
    return xf * lax.rsqrt(jnp.mean(xf * xf, axis=-1, keepdims=True) + EPS) * gain


def _rope_tables(L):
    pos = jnp.arange(L, dtype=jnp.int32)
    row = (pos // GRID_W).astype(F32)
    col = (pos % GRID_W).astype(F32)
    inv = jnp.power(ROPE_THETA, -jnp.arange(0, AXIS_DIM, 2, dtype=F32) / AXIS_DIM)
    ar = row[:, None] * inv[None, :]
    ac = col[:, None] * inv[None, :]
    cos = jnp.concatenate([jnp.cos(ar), jnp.cos(ar), jnp.cos(ac), jnp.cos(ac)], axis=-1)
    sin = jnp.concatenate([-jnp.sin(ar), jnp.sin(ar), -jnp.sin(ac), jnp.sin(ac)], axis=-1)
    return cos, sin


def _rope(x, cos, sin):
    q = AXIS_DIM // 2
    xr = x.reshape(x.shape[:-1] + (2, 2, q))
    partner = jnp.flip(xr, axis=-2).reshape(x.shape)
    return x * cos[None, :, None, :] + partner * sin[None, :, None, :]


def _dwconv3(u, w):
    up = jnp.pad(u, ((0, 0), (1, 1), (0, 0)))
    return up[:, :-2] * w[0] + up[:, 1:-1] * w[1] + up[:, 2:] * w[2]


def _hyena_filter_fft(L, w1, b1, f1, w2, b2, f2, w3, decay):
    t = jnp.linspace(0.0, 1.0, L, dtype=F32)[:, None]
    w = 2.0 * math.pi * jnp.arange(L, dtype=F32)[:, None] / L
    bands = jnp.linspace(1e-4, HY_BANDS - 1, HY_BANDS, dtype=F32)[None, :]
    z = jnp.concatenate([t, jnp.cos(bands * w), -jnp.sin(bands * w)], axis=-1)
    hp = lax.Precision.HIGHEST
    h = jnp.sin(f1 * (jnp.dot(z, w1, precision=hp) + b1))
    h = jnp.sin(f2 * (jnp.dot(h, w2, precision=hp) + b2))
    h = jnp.dot(h, w3, precision=hp).reshape(L, HY_ORDER, 2, HY_W)
    window = jnp.exp(-t[:, :, None] * jnp.abs(decay))
    h = h * window[:, :, None, :]
    h = h / jnp.sum(jnp.abs(h), axis=(0, 2), keepdims=True)
    hf, hb = h[:, :, 0], h[:, :, 1]
    k = jnp.concatenate([hf, jnp.zeros_like(hf[:1]), hb[:0:-1]], axis=0)
    return jnp.fft.rfft(k, axis=0)


def _long_conv(u, kf, bias):
    L = u.shape[1]
    y = jnp.fft.irfft(jnp.fft.rfft(u, n=2 * L, axis=1) * kf[None], n=2 * L, axis=1)[:, :L]
    return y + u * bias


def _hyena(p, conv_w, conv_b, kf, bias):
    u = _dwconv3(p.astype(F32), conv_w) + conv_b
    v, x1, x2 = jnp.split(u, HY_ORDER + 1, axis=-1)
    z = v
    for o, gate in enumerate((x1, x2)):
        z = gate * _long_conv(z, kf[:, o], bias[o])
    return z


def _short_conv(p, w):
    bg, cg, xs = jnp.split(p.astype(F32), 3, axis=-1)
    return bg * _dwconv3(cg * xs, w)


def _peer_route(h2, wq, subkeys):
    T = h2.shape[0]
    hp = lax.Precision.HIGHEST
    q = jnp.dot(h2.astype(F32), wq, precision=hp).reshape(T, PEER_HEADS, 2, PEER_DK // 2)
    s = jnp.einsum('thpd,hpnd->thpn', q, subkeys, precision=hp)
    s1, s2 = s[:, :, 0], s[:, :, 1]
    v1, _ = lax.top_k(s1, PEER_TOPK)
    v2, _ = lax.top_k(s2, PEER_TOPK)
    cand = (v1[..., :, None] + v2[..., None, :]).reshape(T, PEER_HEADS, PEER_TOPK * PEER_TOPK)
    top, _ = lax.top_k(cand, PEER_TOPK + 1)
    theta = 0.5 * (top[..., PEER_TOPK - 1] + top[..., PEER_TOPK])
    z = jnp.sum(jnp.exp(top[..., :PEER_TOPK] - top[..., :1]), axis=-1)
    return s1, s2, theta, z


def _peer(h2, wq, subkeys, u_bf, vT_bf, tc, nb):
    T, D = h2.shape
    s1, s2, theta, z = _peer_route(h2, wq, subkeys)
    m1 = jnp.max(s1, axis=-1, keepdims=True)
    m2 = jnp.max(s2, axis=-1, keepdims=True)
    c = theta[..., None] - s1
    e1 = jnp.exp(s1 - m1) / z[..., None]
    e2 = jnp.exp(s2 - m2)
    tr = lambda a: a.transpose(1, 2, 0)
    c4 = tr(c).reshape(PEER_HEADS, PEER_NKEYS // nb, nb, T)
    e14 = tr(e1).reshape(PEER_HEADS, PEER_NKEYS // nb, nb, T)
    oT = _peer_dense(h2.T, tr(s2), tr(e2), c4, e14, u_bf, vT_bf, tc, nb)
    return oT.T


def kernel(x, c, ctx, c_ctx, w_mod, b_mod, g_norm1, g_norm2, w_in, q_gain, k_gain, sc_conv_w, hy_conv_w,
           hy_conv_b, hy_w1, hy_b1, hy_f1, hy_w2, hy_b2, hy_f2, hy_w3, hy_decay, hy_bias, w_br_att, w_br_sc,
           w_br_hy, w_out, peer_wq, peer_subkeys, peer_u, peer_v, g_final):
    B, S, D = x.shape
    Lc = ctx.shape[1]
    hp = lax.Precision.HIGHEST
    cond = jnp.concatenate([jax.nn.silu(c), jnp.broadcast_to(jax.nn.silu(c_ctx), (B, D))], axis=0)
    cos, sin = _rope_tables(S)
    tm = min(1024, S)
    tq = min(128, S)
    x2 = x.reshape(B * S, D)
    ctx2 = ctx.reshape(B * Lc, D)

    for l in range(DEPTH):
        need_ctx = l < DEPTH - 1
        mod = (jnp.dot(cond, w_mod[l], precision=hp) + b_mod[l]).reshape(2, B, 1, 6, D)
        sh1, sc1, gt1, sh2, sc2, gt2 = (mod[0, :, :, i] for i in range(6))
        csh1, csc1, cgt1, csh2, csc2, cgt2 = (mod[1, :, :, i] for i in range(6))
        w_in_bf = jnp.concatenate([w_in[l][:, REF_GT0:], w_in[l][:, :REF_GT0]], axis=1).astype(BF16)
        wa, ws, wh, wo = (w.astype(BF16) for w in (w_br_att[l], w_br_sc[l], w_br_hy[l], w_out[l]))
        g1 = g_norm1[l][None, :]
        g2 = g_norm2[l][None, :]
        hy_params = (hy_w1[l], hy_b1[l], hy_f1[l], hy_w2[l], hy_b2[l], hy_f2[l], hy_w3[l], hy_decay[l])

        p = _in_proj(x2, g1, sc1, sh1, w_in_bf, S, tm, 768)
        p3 = p.reshape(B, S, -1)
        pc = _in_proj(ctx2, g1, csc1, csh1, w_in_bf, Lc, Lc, 768)
        pc3 = pc.reshape(B, Lc, -1)

        q = _rope(_rms_heads(p3[..., Q0:K0].reshape(B, S, N_HEADS, HEAD_DIM), q_gain[l]), cos, sin)
        k = _rope(_rms_heads(p3[..., K0:V0].reshape(B, S, N_KV_HEADS, HEAD_DIM), k_gain[l]), cos, sin)
        v = p3[..., V0:SC0]
        kc = _rms_heads(pc3[..., K0:V0].reshape(B, Lc, N_KV_HEADS, HEAD_DIM), k_gain[l])
        vc = pc3[..., V0:SC0]
        qs = (q * ATT_SCALE).astype(BF16).reshape(B, S, ATT_W)
        k_all = jnp.concatenate([kc.astype(BF16).reshape(B, Lc, KV_W), k.astype(BF16).reshape(B, S, KV_W)], axis=1)
        v_all = jnp.concatenate([vc, v], axis=1)
        tk = (S + Lc) // 13 if (S + Lc) % (13 * 128) == 0 else 128
        y_att = _attention(qs, k_all, v_all, tq, tk).reshape(B * S, ATT_W)

        y_sc = _short_conv(p3[..., SC0:HY0], sc_conv_w[l]).astype(BF16).reshape(B * S, SC_W)
        y_hy = _hyena(p3[..., HY0:IN_W], hy_conv_w[l], hy_conv_b[l], _hyena_filter_fft(S, *hy_params), hy_bias[l])
        y_hy = y_hy.astype(BF16).reshape(B * S, HY_W)
        x2, h2 = _merge(x2, y_att, y_sc, y_hy, p, gt1, g2, sc2, sh2, wa, ws, wh, wo, S, min(512, S))

        if need_ctx:
            qc = _rms_heads(pc3[..., Q0:K0].reshape(B, Lc, N_HEADS, HEAD_DIM), q_gain[l])
            qcs = (qc * ATT_SCALE).astype(BF16).reshape(B, Lc, ATT_W)
            yc_att = _attention(qcs, kc.astype(BF16).reshape(B, Lc, KV_W), vc, Lc, Lc).reshape(B * Lc, ATT_W)
            yc_sc = _short_conv(pc3[..., SC0:HY0], sc_conv_w[l]).astype(BF16).reshape(B * Lc, SC_W)
            yc_hy = _hyena(pc3[..., HY0:IN_W], hy_conv_w[l], hy_conv_b[l], _hyena_filter_fft(Lc, *hy_params),
                           hy_bias[l]).astype(BF16).reshape(B * Lc, HY_W)
            ctx2, h2c = _merge(ctx2, yc_att, yc_sc, yc_hy, pc, cgt1, g2, csc2, csh2, wa, ws, wh, wo, Lc, Lc)
            tok = jnp.concatenate([h2, h2c], axis=0)
        else:
            tok = h2

        u_bf = peer_u[l].astype(BF16)
        vT_bf = peer_v[l].astype(BF16).T
        f = _peer(tok, peer_wq[l], peer_subkeys[l], u_bf, vT_bf, min(512, tok.shape[0]), 4)
        x2 = x2 + (gt2 * f[:B * S].reshape(B, S, D)).reshape(B * S, D)
        if need_ctx:
            ctx2 = ctx2 + (cgt2 * f[B * S:].reshape(B, Lc, D)).reshape(B * Lc, D)

    xf = x2.reshape(B, S, D)
    y = xf * lax.rsqrt(jnp.mean(xf * xf, axis=-1, keepdims=True) + EPS)
    return y * g_final
```

```python
import functools
import math

import jax
import jax.numpy as jnp
from jax import lax
from jax.experimental import pallas as pl
from jax.experimental.pallas import tpu as pltpu

F32 = jnp.float32
BF16 = jnp.bfloat16

DEPTH = 2
GRID_W = 64
EPS = 1e-6
N_HEADS = 8
N_KV_HEADS = 2
GQA_GROUP = N_HEADS // N_KV_HEADS
HEAD_DIM = 64
AXIS_DIM = HEAD_DIM // 2
ATT_W = N_HEADS * HEAD_DIM
KV_W = N_KV_HEADS * HEAD_DIM
ATT_SCALE = HEAD_DIM ** -0.5
ROPE_THETA = 10000.0
SC_W = 512
HY_W = 512
HY_ORDER = 2
HY_BANDS = 16
PEER_HEADS = 8
PEER_NKEYS = 128
PEER_DK = 128
PEER_TOPK = 16
REF_GT0 = ATT_W + 2 * KV_W + 3 * SC_W + (HY_ORDER + 1) * HY_W
GT0 = 0
Q0 = GT0 + 3 * 1024
K0 = Q0 + ATT_W
V0 = K0 + KV_W
SC0 = V0 + KV_W
HY0 = SC0 + 3 * SC_W
IN_W = HY0 + (HY_ORDER + 1) * HY_W

VMEM_LIMIT = 56 * 1024 * 1024


def _cparams(sem):
    return pltpu.CompilerParams(dimension_semantics=sem, vmem_limit_bytes=VMEM_LIMIT)


def _in_proj_kernel(x_ref, g_ref, sc_ref, sh_ref, w_ref, o_ref, h_scr):
    @pl.when(pl.program_id(1) == 0)
    def _():
        x = x_ref[...]
        ms = jnp.mean(x * x, axis=-1, keepdims=True)
        y = x * lax.rsqrt(ms + EPS) * g_ref[...]
        h_scr[...] = (y * (1.0 + sc_ref[0]) + sh_ref[0]).astype(BF16)

    o_ref[...] = jnp.dot(h_scr[...], w_ref[...], preferred_element_type=F32).astype(o_ref.dtype)


def _in_proj(x2d, g, sc, sh, w, rows_per_batch, tm, tn):
    T, D = x2d.shape
    N = w.shape[1]
    tpb = rows_per_batch // tm
    return pl.pallas_call(
        _in_proj_kernel,
        grid=(T // tm, N // tn),
        in_specs=[
            pl.BlockSpec((tm, D), lambda i, j: (i, 0)),
            pl.BlockSpec((1, D), lambda i, j: (0, 0)),
            pl.BlockSpec((1, 1, D), lambda i, j: (i // tpb, 0, 0)),
            pl.BlockSpec((1, 1, D), lambda i, j: (i // tpb, 0, 0)),
            pl.BlockSpec((D, tn), lambda i, j: (0, j)),
        ],
        out_specs=pl.BlockSpec((tm, tn), lambda i, j: (i, j)),
        out_shape=jax.ShapeDtypeStruct((T, N), BF16),
        scratch_shapes=[pltpu.VMEM((tm, D), BF16)],
        compiler_params=_cparams(("parallel", "arbitrary")),
        name="in_proj",
    )(x2d, g, sc, sh, w)


def _attn_kernel(q_ref, kT_ref, v_ref, o_ref, m_scr, acc_scr, *, nkb):
    q = q_ref[0, 0, 0]
    m_scr[...] = jnp.full(m_scr.shape, -jnp.inf, F32)
    acc_scr[...] = jnp.zeros(acc_scr.shape, F32)

    def body(j, carry):
        s = jnp.dot(q, kT_ref[0, 0, j], preferred_element_type=F32)
        m_prev = m_scr[...]
        m_new = jnp.maximum(m_prev, jnp.max(s, axis=-1, keepdims=True))
        p = jnp.exp(s - m_new).astype(BF16)
        alpha = jnp.exp(m_prev - m_new)
        acc_scr[...] = acc_scr[...] * alpha + jnp.dot(p, v_ref[0, 0, j], preferred_element_type=F32)
        m_scr[...] = m_new
        return carry

    lax.fori_loop(0, nkb, body, 0)
    acc = acc_scr[...]
    o_ref[0, 0, 0] = (acc[:, :HEAD_DIM] / acc[:, HEAD_DIM:HEAD_DIM + 1]).astype(o_ref.dtype)


def _attention(q, k, v, tq, tk):
    B, Lq, _ = q.shape
    Lk = k.shape[1]
    nqb, nkb = Lq // tq, Lk // tk
    R = GQA_GROUP * tq
    qb = q.reshape(B, nqb, tq, N_KV_HEADS, GQA_GROUP, HEAD_DIM).transpose(0, 3, 1, 4, 2, 5)
    qb = qb.reshape(B, N_KV_HEADS, nqb, R, HEAD_DIM)
    kT = k.reshape(B, nkb, tk, N_KV_HEADS, HEAD_DIM).transpose(0, 3, 1, 4, 2)
    vb = v.reshape(B, nkb, tk, N_KV_HEADS, HEAD_DIM).transpose(0, 3, 1, 2, 4)
    ones = jnp.ones(vb.shape[:-1] + (1,), BF16)
    zeros = jnp.zeros(vb.shape[:-1] + (HEAD_DIM - 1,), BF16)
    vb = jnp.concatenate([vb, ones, zeros], axis=-1)
    ob = pl.pallas_call(
        functools.partial(_attn_kernel, nkb=nkb),
        grid=(B, N_KV_HEADS, nqb),
        in_specs=[
            pl.BlockSpec((1, 1, 1, R, HEAD_DIM), lambda b, h, i: (b, h, i, 0, 0)),
            pl.BlockSpec((1, 1, nkb, HEAD_DIM, tk), lambda b, h, i: (b, h, 0, 0, 0)),
            pl.BlockSpec((1, 1, nkb, tk, 2 * HEAD_DIM), lambda b, h, i: (b, h, 0, 0, 0)),
        ],
        out_specs=pl.BlockSpec((1, 1, 1, R, HEAD_DIM), lambda b, h, i: (b, h, i, 0, 0)),
        out_shape=jax.ShapeDtypeStruct((B, N_KV_HEADS, nqb, R, HEAD_DIM), BF16),
        scratch_shapes=[pltpu.VMEM((R, 1), F32), pltpu.VMEM((R, 2 * HEAD_DIM), F32)],
        compiler_params=_cparams(("parallel", "parallel", "arbitrary")),
        name="attention",
    )(qb, kT, vb)
    ob = ob.reshape(B, N_KV_HEADS, nqb, GQA_GROUP, tq, HEAD_DIM).transpose(0, 2, 4, 1, 3, 5)
    return ob.reshape(B, Lq, ATT_W)


def _merge_kernel(x_ref, ya_ref, ys_ref, yh_ref, ga_ref, gs_ref, gh_ref, gt_ref, g2_ref, sc_ref, sh_ref,
                  wa_ref, ws_ref, wh_ref, wo_ref, xo_ref, h2_ref):
    def br(y_ref, g_ref, w_ref):
        gate = jax.nn.sigmoid(g_ref[...].astype(F32))
        return gate * jnp.dot(y_ref[...], w_ref[...], preferred_element_type=F32)

    m = br(ya_ref, ga_ref, wa_ref) + br(ys_ref, gs_ref, ws_ref) + br(yh_ref, gh_ref, wh_ref)
    o = jnp.dot(m.astype(BF16), wo_ref[...], preferred_element_type=F32)
    xn = x_ref[...] + gt_ref[0] * o
    xo_ref[...] = xn
    ms = jnp.mean(xn * xn, axis=-1, keepdims=True)
    y = xn * lax.rsqrt(ms + EPS) * g2_ref[...]
    h2_ref[...] = (y * (1.0 + sc_ref[0]) + sh_ref[0]).astype(BF16)


def _merge(x2d, ya, ys, yh, p, gt1, g2, sc2, sh2, wa, ws, wh, wo, rows_per_batch, tm):
    T, D = x2d.shape
    tpb = rows_per_batch // tm
    gblk = GT0 // D
    row = lambda i: (i, 0)
    mod = lambda i: (i // tpb, 0, 0)
    full = lambda i: (0, 0)
    return pl.pallas_call(
        _merge_kernel,
        grid=(T // tm,),
        in_specs=[
            pl.BlockSpec((tm, D), row),
            pl.BlockSpec((tm, ATT_W), row),
            pl.BlockSpec((tm, SC_W), row),
            pl.BlockSpec((tm, HY_W), row),
            pl.BlockSpec((tm, D), lambda i: (i, gblk)),
            pl.BlockSpec((tm, D), lambda i: (i, gblk + 1)),
            pl.BlockSpec((tm, D), lambda i: (i, gblk + 2)),
            pl.BlockSpec((1, 1, D), mod),
            pl.BlockSpec((1, D), full),
            pl.BlockSpec((1, 1, D), mod),
            pl.BlockSpec((1, 1, D), mod),
            pl.BlockSpec((ATT_W, D), full),
            pl.BlockSpec((SC_W, D), full),
            pl.BlockSpec((HY_W, D), full),
            pl.BlockSpec((D, D), full),
        ],
        out_specs=[pl.BlockSpec((tm, D), row), pl.BlockSpec((tm, D), row)],
        out_shape=[jax.ShapeDtypeStruct((T, D), F32), jax.ShapeDtypeStruct((T, D), BF16)],
        compiler_params=_cparams(("parallel",)),
        name="merge",
    )(x2d, ya, ys, yh, p, p, p, gt1, g2, sc2, sh2, wa, ws, wh, wo)


def _peer_kernel(hT_ref, s2_ref, e2_ref, c_ref, e1_ref, u_ref, vT_ref, o_ref, gw_scr, *, nb):
    @pl.when(pl.program_id(1) == 0)
    def _():
        o_ref[...] = jnp.zeros(o_ref.shape, F32)

    hT = hT_ref[...]
    for ii in range(nb):
        a = jnp.dot(u_ref[ii * PEER_NKEYS:(ii + 1) * PEER_NKEYS, :], hT, preferred_element_type=F32)
        w = jnp.zeros(a.shape, F32)
        for h in range(PEER_HEADS):
            keep = s2_ref[h] >= c_ref[h, 0, ii:ii + 1, :]
            w = w + jnp.where(keep, e2_ref[h], 0.0) * e1_ref[h, 0, ii:ii + 1, :]
        act = 0.5 * a * (1.0 + lax.erf(a * (2.0 ** -0.5)))
        gw_scr[ii * PEER_NKEYS:(ii + 1) * PEER_NKEYS, :] = (act * w).astype(BF16)
    o_ref[...] += jnp.dot(vT_ref[...], gw_scr[...], preferred_element_type=F32)


def _peer_dense(hT, s2, e2, c, e1, u, vT, tc, nb):
    D, T = hT.shape
    N = u.shape[0]
    eb = nb * PEER_NKEYS
    return pl.pallas_call(
        functools.partial(_peer_kernel, nb=nb),
        grid=(T // tc, N // eb),
        in_specs=[
            pl.BlockSpec((D, tc), lambda t, e: (0, t)),
            pl.BlockSpec((PEER_HEADS, PEER_NKEYS, tc), lambda t, e: (0, 0, t)),
            pl.BlockSpec((PEER_HEADS, PEER_NKEYS, tc), lambda t, e: (0, 0, t)),
            pl.BlockSpec((PEER_HEADS, 1, nb, tc), lambda t, e: (0, e, 0, t)),
            pl.BlockSpec((PEER_HEADS, 1, nb, tc), lambda t, e: (0, e, 0, t)),
            pl.BlockSpec((eb, D), lambda t, e: (e, 0)),
            pl.BlockSpec((D, eb), lambda t, e: (0, e)),
        ],
        out_specs=pl.BlockSpec((D, tc), lambda t, e: (0, t)),
        out_shape=jax.ShapeDtypeStruct((D, T), F32),
        scratch_shapes=[pltpu.VMEM((eb, tc), BF16)],
        compiler_params=_cparams(("parallel", "arbitrary")),
        name="peer_dense",
    )(hT, s2, e2, c, e1, u, vT)


ROUTE_ROWS = 24
NEG_INF = float("-inf")


def _top_rows(s, n):
    tc = s.shape[1]
    rid = lax.broadcasted_iota(jnp.int32, (ROUTE_ROWS, tc), 0)
    packed = jnp.full((ROUTE_ROWS, tc), NEG_INF, F32)
    for k in range(n):
        m = jnp.max(s, axis=0, keepdims=True)
        packed = jnp.where(rid == k, m, packed)
        s = jnp.where(s == m, NEG_INF, s)
    return packed


def _row_penalty(tc, lo, hi):
    rid = lax.broadcasted_iota(jnp.int32, (8, tc), 0)
    return jnp.where((rid >= lo) & (rid < hi), 0.0, NEG_INF).astype(F32)


def _route_kernel(hT_ref, wqh_ref, wql_ref, sk_ref, c_ref, e1_ref, s2_ref, e2_ref, q_scr):
    hT = hT_ref[...]
    q_scr[...] = (jnp.dot(wqh_ref[...], hT, preferred_element_type=F32)
                  + jnp.dot(wql_ref[...], hT, preferred_element_type=F32))
    tc = hT.shape[1]
    n = PEER_TOPK + 1
    half = PEER_DK // 2

    def head(h, carry):
        r0 = pl.multiple_of(h * PEER_DK, PEER_DK)
        hp = lax.Precision.HIGHEST
        s1 = jnp.dot(sk_ref[h, 0], q_scr[pl.ds(r0, half), :], preferred_element_type=F32, precision=hp)
        s2 = jnp.dot(sk_ref[h, 1], q_scr[pl.ds(r0 + half, half), :], preferred_element_type=F32, precision=hp)
        v1 = _top_rows(s1, n)
        v2 = _top_rows(s2, n)
        tiles = [v1[0:1] + v2[0:8], v1[0:1] + v2[8:16], v1[0:1] + v2[16:24], v1[1:2] + v2[0:8],
                 v1[2:3] + v2[0:8] + _row_penalty(tc, 0, n // 3), v1[3:4] + v2[0:8] + _row_penalty(tc, 0, n // 4),
                 v2[0:1] + v1[0:8] + _row_penalty(tc, 4, 8), v2[0:1] + v1[8:16], v2[0:1] + v1[16:24],
                 v2[1:2] + v1[0:8] + _row_penalty(tc, 4, n // 2), v2[2:3] + v1[0:8] + _row_penalty(tc, 4, n // 3)]
        top = _top_rows(jnp.concatenate(tiles, axis=0), n)
        theta = 0.5 * (top[PEER_TOPK - 1:PEER_TOPK] + top[PEER_TOPK:PEER_TOPK + 1])
        z = jnp.sum(jnp.exp(top[0:16] - top[0:1]), axis=0, keepdims=True)
        c_ref[h] = theta - s1
        e1_ref[h] = jnp.exp(s1 - v1[0:1]) / z
        s2_ref[h] = s2
        e2_ref[h] = jnp.exp(s2 - v2[0:1])
        return carry

    lax.fori_loop(0, PEER_HEADS, head, 0)


def _peer_route(hT, wqT_hi, wqT_lo, subkeys, tc):
    D, T = hT.shape
    W = wqT_hi.shape[0]
    out = jax.ShapeDtypeStruct((PEER_HEADS, PEER_NKEYS, T), F32)
    ospec = pl.BlockSpec((PEER_HEADS, PEER_NKEYS, tc), lambda t: (0, 0, t))
    return pl.pallas_call(
        _route_kernel,
        grid=(T // tc,),
        in_specs=[
            pl.BlockSpec((D, tc), lambda t: (0, t)),
            pl.BlockSpec((W, D), lambda t: (0, 0)),
            pl.BlockSpec((W, D), lambda t: (0, 0)),
            pl.BlockSpec(subkeys.shape, lambda t: (0, 0, 0, 0)),
        ],
        out_specs=[ospec, ospec, ospec, ospec],
        out_shape=[out, out, out, out],
        scratch_shapes=[pltpu.VMEM((W, tc), F32)],
        compiler_params=_cparams(("parallel",)),
        name="peer_route",
    )(hT, wqT_hi, wqT_lo, subkeys)


def _rms_heads(x, gain):
    xf = x.astype(F32)
    return xf * lax.rsqrt(jnp.mean(xf * xf, axis=-1, keepdims=True) + EPS) * gain


def _rope_tables(L):
    pos = jnp.arange(L, dtype=jnp.int32)
    row = (pos // GRID_W).astype(F32)
    col = (pos % GRID_W).astype(F32)
    inv = jnp.power(ROPE_THETA, -jnp.arange(0, AXIS_DIM, 2, dtype=F32) / AXIS_DIM)
    ar = row[:, None] * inv[None, :]
    ac = col[:, None] * inv[None, :]
    cos = jnp.concatenate([jnp.cos(ar), jnp.cos(ar), jnp.cos(ac), jnp.cos(ac)], axis=-1)
    sin = jnp.concatenate([-jnp.sin(ar), jnp.sin(ar), -jnp.sin(ac), jnp.sin(ac)], axis=-1)
    return cos, sin


def _rope(x, cos, sin):
    q = AXIS_DIM // 2
    xr = x.reshape(x.shape[:-1] + (2, 2, q))
    partner = jnp.flip(xr, axis=-2).reshape(x.shape)
    return x * cos[None, :, None, :] + partner * sin[None, :, None, :]


def _dwconv3(u, w):
    up = jnp.pad(u, ((0, 0), (1, 1), (0, 0)))
    return up[:, :-2] * w[0] + up[:, 1:-1] * w[1] + up[:, 2:] * w[2]


def _hyena_filter_fft(L, w1, b1, f1, w2, b2, f2, w3, decay):
    t = jnp.linspace(0.0, 1.0, L, dtype=F32)[:, None]
    w = 2.0 * math.pi * jnp.arange(L, dtype=F32)[:, None] / L
    bands = jnp.linspace(1e-4, HY_BANDS - 1, HY_BANDS, dtype=F32)[None, :]
    z = jnp.concatenate([t, jnp.cos(bands * w), -jnp.sin(bands * w)], axis=-1)
    hp = lax.Precision.HIGHEST
    h = jnp.sin(f1 * (jnp.dot(z, w1, precision=hp) + b1))
    h = jnp.sin(f2 * (jnp.dot(h, w2, precision=hp) + b2))
    h = jnp.dot(h, w3, precision=hp).reshape(L, HY_ORDER, 2, HY_W)
    window = jnp.exp(-t[:, :, None] * jnp.abs(decay))
    h = h * window[:, :, None, :]
    h = h / jnp.sum(jnp.abs(h), axis=(0, 2), keepdims=True)
    hf, hb = h[:, :, 0], h[:, :, 1]
    k = jnp.concatenate([hf, jnp.zeros_like(hf[:1]), hb[:0:-1]], axis=0)
    return jnp.fft.rfft(k, axis=0)


def _long_conv(u, kf, bias):
    L = u.shape[1]
    y = jnp.fft.irfft(jnp.fft.rfft(u, n=2 * L, axis=1) * kf[None], n=2 * L, axis=1)[:, :L]
    return y + u * bias


def _hyena(p, conv_w, conv_b, kf, bias):
    u = _dwconv3(p.astype(F32), conv_w) + conv_b
    v, x1, x2 = jnp.split(u, HY_ORDER + 1, axis=-1)
    z = v
    for o, gate in enumerate((x1, x2)):
        z = gate * _long_conv(z, kf[:, o], bias[o])
    return z


def _short_conv(p, w):
    bg, cg, xs = jnp.split(p.astype(F32), 3, axis=-1)
    return bg * _dwconv3(cg * xs, w)


def _peer(h2, wq, subkeys, u_bf, vT_bf, tc, nb):
    T, D = h2.shape
    hT = h2.T
    wqT = wq.T
    wqT_hi = wqT.astype(BF16)
    wqT_lo = (wqT - wqT_hi.astype(F32)).astype(BF16)
    c, e1, s2, e2 = _peer_route(hT, wqT_hi, wqT_lo, subkeys, tc)
    c4 = c.reshape(PEER_HEADS, PEER_NKEYS // nb, nb, T)
    e14 = e1.reshape(PEER_HEADS, PEER_NKEYS // nb, nb, T)
    oT = _peer_dense(hT, s2, e2, c4, e14, u_bf, vT_bf, tc, nb)
    return oT.T


def kernel(x, c, ctx, c_ctx, w_mod, b_mod, g_norm1, g_norm2, w_in, q_gain, k_gain, sc_conv_w, hy_conv_w,
           hy_conv_b, hy_w1, hy_b1, hy_f1, hy_w2, hy_b2, hy_f2, hy_w3, hy_decay, hy_bias, w_br_att, w_br_sc,
           w_br_hy, w_out, peer_wq, peer_subkeys, peer_u, peer_v, g_final):
    B, S, D = x.shape
    Lc = ctx.shape[1]
    hp = lax.Precision.HIGHEST
    cond = jnp.concatenate([jax.nn.silu(c), jnp.broadcast_to(jax.nn.silu(c_ctx), (B, D))], axis=0)
    cos, sin = _rope_tables(S)
    tm = min(1024, S)
    tq = min(128, S)
    x2 = x.reshape(B * S, D)
    ctx2 = ctx.reshape(B * Lc, D)

    for l in range(DEPTH):
        need_ctx = l < DEPTH - 1
        mod = (jnp.dot(cond, w_mod[l], precision=hp) + b_mod[l]).reshape(2, B, 1, 6, D)
        sh1, sc1, gt1, sh2, sc2, gt2 = (mod[0, :, :, i] for i in range(6))
        csh1, csc1, cgt1, csh2, csc2, cgt2 = (mod[1, :, :, i] for i in range(6))
        w_in_bf = jnp.concatenate([w_in[l][:, REF_GT0:], w_in[l][:, :REF_GT0]], axis=1).astype(BF16)
        wa, ws, wh, wo = (w.astype(BF16) for w in (w_br_att[l], w_br_sc[l], w_br_hy[l], w_out[l]))
        g1 = g_norm1[l][None, :]
        g2 = g_norm2[l][None, :]
        hy_params = (hy_w1[l], hy_b1[l], hy_f1[l], hy_w2[l], hy_b2[l], hy_f2[l], hy_w3[l], hy_decay[l])

        p = _in_proj(x2, g1, sc1, sh1, w_in_bf, S, tm, 768)
        p3 = p.reshape(B, S, -1)
        pc = _in_proj(ctx2, g1, csc1, csh1, w_in_bf, Lc, Lc, 768)
        pc3 = pc.reshape(B, Lc, -1)

        q = _rope(_rms_heads(p3[..., Q0:K0].reshape(B, S, N_HEADS, HEAD_DIM), q_gain[l]), cos, sin)
        k = _rope(_rms_heads(p3[..., K0:V0].reshape(B, S, N_KV_HEADS, HEAD_DIM), k_gain[l]), cos, sin)
        v = p3[..., V0:SC0]
        kc = _rms_heads(pc3[..., K0:V0].reshape(B, Lc, N_KV_HEADS, HEAD_DIM), k_gain[l])
        vc = pc3[..., V0:SC0]
        qs = (q * ATT_SCALE).astype(BF16).reshape(B, S, ATT_W)
        k_all = jnp.concatenate([kc.astype(BF16).reshape(B, Lc, KV_W), k.astype(BF16).reshape(B, S, KV_W)], axis=1)
        v_all = jnp.concatenate([vc, v], axis=1)
        tk = (S + Lc) // 13 if (S + Lc) % (13 * 128) == 0 else 128
        y_att = _attention(qs, k_all, v_all, tq, tk).reshape(B * S, ATT_W)

        y_sc = _short_conv(p3[..., SC0:HY0], sc_conv_w[l]).astype(BF16).reshape(B * S, SC_W)
        y_hy = _hyena(p3[..., HY0:IN_W], hy_conv_w[l], hy_conv_b[l], _hyena_filter_fft(S, *hy_params), hy_bias[l])
        y_hy = y_hy.astype(BF16).reshape(B * S, HY_W)
        x2, h2 = _merge(x2, y_att, y_sc, y_hy, p, gt1, g2, sc2, sh2, wa, ws, wh, wo, S, min(512, S))

        if need_ctx:
            qc = _rms_heads(pc3[..., Q0:K0].reshape(B, Lc, N_HEADS, HEAD_DIM), q_gain[l])
            qcs = (qc * ATT_SCALE).astype(BF16).reshape(B, Lc, ATT_W)
            yc_att = _attention(qcs, kc.astype(BF16).reshape(B, Lc, KV_W), vc, Lc, Lc).reshape(B * Lc, ATT_W)
            yc_sc = _short_conv(pc3[..., SC0:HY0], sc_conv_w[l]).astype(BF16).reshape(B * Lc, SC_W)
            yc_hy = _hyena(pc3[..., HY0:IN_W], hy_conv_w[l], hy_conv_b[l], _hyena_filter_fft(Lc, *hy_params),
                           hy_bias[l]).astype(BF16).reshape(B * Lc, HY_W)
            ctx2, h2c = _merge(ctx2, yc_att, yc_sc, yc_hy, pc, cgt1, g2, csc2, csh2, wa, ws, wh, wo, Lc, Lc)
            tok = jnp.concatenate([h2, h2c], axis=0)
        else:
            tok = h2

        u_bf = peer_u[l].astype(BF16)
        vT_bf = peer_v[l].astype(BF16).T
        f = _peer(tok, peer_wq[l], peer_subkeys[l], u_bf, vT_bf, min(512, tok.shape[0]), 4)
        x2 = x2 + (gt2 * f[:B * S].reshape(B, S, D)).reshape(B * S, D)
        if need_ctx:
            ctx2 = ctx2 + (cgt2 * f[B * S:].reshape(B, Lc, D)).reshape(B * Lc, D)

    xf = x2.reshape(B, S, D)
    y = xf * lax.rsqrt(jnp.mean(xf * xf, axis=-1, keepdims=True) + EPS)
    return y * g_final
```

```python
import functools
import math

import jax
import jax.numpy as jnp
from jax import lax
from jax.experimental import pallas as pl
from jax.experimental.pallas import tpu as pltpu

F32 = jnp.float32
BF16 = jnp.bfloat16

DEPTH = 2
GRID_W = 64
EPS = 1e-6
N_HEADS = 8
N_KV_HEADS = 2
GQA_GROUP = N_HEADS // N_KV_HEADS
HEAD_DIM = 64
AXIS_DIM = HEAD_DIM // 2
ATT_W = N_HEADS * HEAD_DIM
KV_W = N_KV_HEADS * HEAD_DIM
ATT_SCALE = HEAD_DIM ** -0.5
ROPE_THETA = 10000.0
SC_W = 512
HY_W = 512
HY_ORDER = 2
HY_BANDS = 16
PEER_HEADS = 8
PEER_NKEYS = 128
PEER_DK = 128
PEER_TOPK = 16
D_MODEL = 1024
QKV_W = ATT_W + 2 * KV_W
REF_SC0 = QKV_W
REF_GT0 = QKV_W + 3 * SC_W + (HY_ORDER + 1) * HY_W
GT0 = 0
SC0 = GT0 + 3 * D_MODEL
HY0 = SC0 + 3 * SC_W
Q0 = HY0 + (HY_ORDER + 1) * HY_W
K0 = Q0 + ATT_W
V0 = K0 + KV_W
IN_W = V0 + KV_W
HALO = 8
HY_N2 = 256
HY_FEAT = 128

VMEM_LIMIT = 56 * 1024 * 1024


def _cparams(sem):
    return pltpu.CompilerParams(dimension_semantics=sem, vmem_limit_bytes=VMEM_LIMIT)


def _in_proj_kernel(x_ref, g_ref, sc_ref, sh_ref, w_ref, o_ref, h_scr):
    @pl.when(pl.program_id(1) == 0)
    def _():
        x = x_ref[...]
        ms = jnp.mean(x * x, axis=-1, keepdims=True)
        y = x * lax.rsqrt(ms + EPS) * g_ref[...]
        h_scr[...] = (y * (1.0 + sc_ref[0]) + sh_ref[0]).astype(BF16)

    o_ref[...] = jnp.dot(h_scr[...], w_ref[...], preferred_element_type=F32).astype(o_ref.dtype)


def _in_proj(x2d, g, sc, sh, w, rows_per_batch, tm, tn):
    T, D = x2d.shape
    N = w.shape[1]
    tpb = rows_per_batch // tm
    return pl.pallas_call(
        _in_proj_kernel,
        grid=(T // tm, N // tn),
        in_specs=[
            pl.BlockSpec((tm, D), lambda i, j: (i, 0)),
            pl.BlockSpec((1, D), lambda i, j: (0, 0)),
            pl.BlockSpec((1, 1, D), lambda i, j: (i // tpb, 0, 0)),
            pl.BlockSpec((1, 1, D), lambda i, j: (i // tpb, 0, 0)),
            pl.BlockSpec((D, tn), lambda i, j: (0, j)),
        ],
        out_specs=pl.BlockSpec((tm, tn), lambda i, j: (i, j)),
        out_shape=jax.ShapeDtypeStruct((T, N), BF16),
        scratch_shapes=[pltpu.VMEM((tm, D), BF16)],
        compiler_params=_cparams(("parallel", "arbitrary")),
        name="in_proj",
    )(x2d, g, sc, sh, w)


def _attn_kernel(q_ref, kT_ref, v_ref, o_ref, m_scr, acc_scr, *, nkb):
    q = q_ref[0, 0, 0]
    m_scr[...] = jnp.full(m_scr.shape, -jnp.inf, F32)
    acc_scr[...] = jnp.zeros(acc_scr.shape, F32)

    def body(j, carry):
        s = jnp.dot(q, kT_ref[0, 0, j], preferred_element_type=F32)
        m_prev = m_scr[...]
        m_new = jnp.maximum(m_prev, jnp.max(s, axis=-1, keepdims=True))
        p = jnp.exp(s - m_new).astype(BF16)
        alpha = jnp.exp(m_prev - m_new)
        acc_scr[...] = acc_scr[...] * alpha + jnp.dot(p, v_ref[0, 0, j], preferred_element_type=F32)
        m_scr[...] = m_new
        return carry

    lax.fori_loop(0, nkb, body, 0)
    acc = acc_scr[...]
    o_ref[0, 0, 0] = (acc[:, :HEAD_DIM] / acc[:, HEAD_DIM:HEAD_DIM + 1]).astype(o_ref.dtype)


def _attention(q, k, v, tq, tk):
    B, Lq, _ = q.shape
    Lk = k.shape[1]
    nqb, nkb = Lq // tq, Lk // tk
    R = GQA_GROUP * tq
    qb = q.reshape(B, nqb, tq, N_KV_HEADS, GQA_GROUP, HEAD_DIM).transpose(0, 3, 1, 4, 2, 5)
    qb = qb.reshape(B, N_KV_HEADS, nqb, R, HEAD_DIM)
    kT = k.reshape(B, nkb, tk, N_KV_HEADS, HEAD_DIM).transpose(0, 3, 1, 4, 2)
    vb = v.reshape(B, nkb, tk, N_KV_HEADS, HEAD_DIM).transpose(0, 3, 1, 2, 4)
    ones = jnp.ones(vb.shape[:-1] + (1,), BF16)
    zeros = jnp.zeros(vb.shape[:-1] + (HEAD_DIM - 1,), BF16)
    vb = jnp.concatenate([vb, ones, zeros], axis=-1)
    ob = pl.pallas_call(
        functools.partial(_attn_kernel, nkb=nkb),
        grid=(B, N_KV_HEADS, nqb),
        in_specs=[
            pl.BlockSpec((1, 1, 1, R, HEAD_DIM), lambda b, h, i: (b, h, i, 0, 0)),
            pl.BlockSpec((1, 1, nkb, HEAD_DIM, tk), lambda b, h, i: (b, h, 0, 0, 0)),
            pl.BlockSpec((1, 1, nkb, tk, 2 * HEAD_DIM), lambda b, h, i: (b, h, 0, 0, 0)),
        ],
        out_specs=pl.BlockSpec((1, 1, 1, R, HEAD_DIM), lambda b, h, i: (b, h, i, 0, 0)),
        out_shape=jax.ShapeDtypeStruct((B, N_KV_HEADS, nqb, R, HEAD_DIM), BF16),
        scratch_shapes=[pltpu.VMEM((R, 1), F32), pltpu.VMEM((R, 2 * HEAD_DIM), F32)],
        compiler_params=_cparams(("parallel", "parallel", "arbitrary")),
        name="attention",
    )(qb, kT, vb)
    ob = ob.reshape(B, N_KV_HEADS, nqb, GQA_GROUP, tq, HEAD_DIM).transpose(0, 2, 4, 1, 3, 5)
    return ob.reshape(B, Lq, ATT_W)


def _merge_kernel(x_ref, ya_ref, ys_ref, yh_ref, ga_ref, gs_ref, gh_ref, gt_ref, g2_ref, sc_ref, sh_ref,
                  wa_ref, ws_ref, wh_ref, wo_ref, xo_ref, h2_ref):
    def br(y_ref, g_ref, w_ref):
        gate = jax.nn.sigmoid(g_ref[...].astype(F32))
        return gate * jnp.dot(y_ref[...], w_ref[...], preferred_element_type=F32)

    m = br(ya_ref, ga_ref, wa_ref) + br(ys_ref, gs_ref, ws_ref) + br(yh_ref, gh_ref, wh_ref)
    o = jnp.dot(m.astype(BF16), wo_ref[...], preferred_element_type=F32)
    xn = x_ref[...] + gt_ref[0] * o
    xo_ref[...] = xn
    ms = jnp.mean(xn * xn, axis=-1, keepdims=True)
    y = xn * lax.rsqrt(ms + EPS) * g2_ref[...]
    h2_ref[...] = (y * (1.0 + sc_ref[0]) + sh_ref[0]).astype(BF16)


def _merge(x2d, ya, ys, yh, p, gt1, g2, sc2, sh2, wa, ws, wh, wo, rows_per_batch, tm):
    T, D = x2d.shape
    tpb = rows_per_batch // tm
    gblk = GT0 // D
    row = lambda i: (i, 0)
    mod = lambda i: (i // tpb, 0, 0)
    full = lambda i: (0, 0)
    return pl.pallas_call(
        _merge_kernel,
        grid=(T // tm,),
        in_specs=[
            pl.BlockSpec((tm, D), row),
            pl.BlockSpec((tm, ATT_W), row),
            pl.BlockSpec((tm, SC_W), row),
            pl.BlockSpec((tm, HY_W), row),
            pl.BlockSpec((tm, D), lambda i: (i, gblk)),
            pl.BlockSpec((tm, D), lambda i: (i, gblk + 1)),
            pl.BlockSpec((tm, D), lambda i: (i, gblk + 2)),
            pl.BlockSpec((1, 1, D), mod),
            pl.BlockSpec((1, D), full),
            pl.BlockSpec((1, 1, D), mod),
            pl.BlockSpec((1, 1, D), mod),
            pl.BlockSpec((ATT_W, D), full),
            pl.BlockSpec((SC_W, D), full),
            pl.BlockSpec((HY_W, D), full),
            pl.BlockSpec((D, D), full),
        ],
        out_specs=[pl.BlockSpec((tm, D), row), pl.BlockSpec((tm, D), row)],
        out_shape=[jax.ShapeDtypeStruct((T, D), F32), jax.ShapeDtypeStruct((T, D), BF16)],
        compiler_params=_cparams(("parallel",)),
        name="merge",
    )(x2d, ya, ys, yh, p, p, p, gt1, g2, sc2, sh2, wa, ws, wh, wo)


def _peer_kernel(hT_ref, s2_ref, e2_ref, c_ref, e1_ref, u_ref, vT_ref, o_ref, gw_scr, *, nb):
    @pl.when(pl.program_id(1) == 0)
    def _():
        o_ref[...] = jnp.zeros(o_ref.shape, F32)

    hT = hT_ref[...]
    for ii in range(nb):
        a = jnp.dot(u_ref[ii * PEER_NKEYS:(ii + 1) * PEER_NKEYS, :], hT, preferred_element_type=F32)
        w = jnp.zeros(a.shape, F32)
        for h in range(PEER_HEADS):
            keep = s2_ref[h] >= c_ref[h, 0, ii:ii + 1, :]
            w = w + jnp.where(keep, e2_ref[h], 0.0) * e1_ref[h, 0, ii:ii + 1, :]
        act = 0.5 * a * (1.0 + lax.erf(a * (2.0 ** -0.5)))
        gw_scr[ii * PEER_NKEYS:(ii + 1) * PEER_NKEYS, :] = (act * w).astype(BF16)
    o_ref[...] += jnp.dot(vT_ref[...], gw_scr[...], preferred_element_type=F32)


def _peer_dense(hT, s2, e2, c, e1, u, vT, tc, nb):
    D, T = hT.shape
    N = u.shape[0]
    eb = nb * PEER_NKEYS
    return pl.pallas_call(
        functools.partial(_peer_kernel, nb=nb),
        grid=(T // tc, N // eb),
        in_specs=[
            pl.BlockSpec((D, tc), lambda t, e: (0, t)),
            pl.BlockSpec((PEER_HEADS, PEER_NKEYS, tc), lambda t, e: (0, 0, t)),
            pl.BlockSpec((PEER_HEADS, PEER_NKEYS, tc), lambda t, e: (0, 0, t)),
            pl.BlockSpec((PEER_HEADS, 1, nb, tc), lambda t, e: (0, e, 0, t)),
            pl.BlockSpec((PEER_HEADS, 1, nb, tc), lambda t, e: (0, e, 0, t)),
            pl.BlockSpec((eb, D), lambda t, e: (e, 0)),
            pl.BlockSpec((D, eb), lambda t, e: (0, e)),
        ],
        out_specs=pl.BlockSpec((D, tc), lambda t, e: (0, t)),
        out_shape=jax.ShapeDtypeStruct((D, T), F32),
        scratch_shapes=[pltpu.VMEM((eb, tc), BF16)],
        compiler_params=_cparams(("parallel", "arbitrary")),
        name="peer_dense",
    )(hT, s2, e2, c, e1, u, vT)


ROUTE_ROWS = 24
NEG_INF = float("-inf")


def _top_rows(s, n):
    tc = s.shape[1]
    rid = lax.broadcasted_iota(jnp.int32, (ROUTE_ROWS, tc), 0)
    packed = jnp.full((ROUTE_ROWS, tc), NEG_INF, F32)
    for k in range(n):
        m = jnp.max(s, axis=0, keepdims=True)
        packed = jnp.where(rid == k, m, packed)
        s = jnp.where(s == m, NEG_INF, s)
    return packed


def _row_penalty(tc, lo, hi):
    rid = lax.broadcasted_iota(jnp.int32, (8, tc), 0)
    return jnp.where((rid >= lo) & (rid < hi), 0.0, NEG_INF).astype(F32)


def _route_kernel(hT_ref, wqh_ref, wql_ref, sk_ref, c_ref, e1_ref, s2_ref, e2_ref, q_scr):
    hT = hT_ref[...]
    q_scr[...] = (jnp.dot(wqh_ref[...], hT, preferred_element_type=F32)
                  + jnp.dot(wql_ref[...], hT, preferred_element_type=F32))
    tc = hT.shape[1]
    n = PEER_TOPK + 1
    half = PEER_DK // 2

    def head(h, carry):
        r0 = pl.multiple_of(h * PEER_DK, PEER_DK)
        hp = lax.Precision.HIGHEST
        s1 = jnp.dot(sk_ref[h, 0], q_scr[pl.ds(r0, half), :], preferred_element_type=F32, precision=hp)
        s2 = jnp.dot(sk_ref[h, 1], q_scr[pl.ds(r0 + half, half), :], preferred_element_type=F32, precision=hp)
        v1 = _top_rows(s1, n)
        v2 = _top_rows(s2, n)
        tiles = [v1[0:1] + v2[0:8], v1[0:1] + v2[8:16], v1[0:1] + v2[16:24], v1[1:2] + v2[0:8],
                 v1[2:3] + v2[0:8] + _row_penalty(tc, 0, n // 3), v1[3:4] + v2[0:8] + _row_penalty(tc, 0, n // 4),
                 v2[0:1] + v1[0:8] + _row_penalty(tc, 4, 8), v2[0:1] + v1[8:16], v2[0:1] + v1[16:24],
                 v2[1:2] + v1[0:8] + _row_penalty(tc, 4, n // 2), v2[2:3] + v1[0:8] + _row_penalty(tc, 4, n // 3)]
        top = _top_rows(jnp.concatenate(tiles, axis=0), n)
        theta = 0.5 * (top[PEER_TOPK - 1:PEER_TOPK] + top[PEER_TOPK:PEER_TOPK + 1])
        z = jnp.sum(jnp.exp(top[0:16] - top[0:1]), axis=0, keepdims=True)
        c_ref[h] = theta - s1
        e1_ref[h] = jnp.exp(s1 - v1[0:1]) / z
        s2_ref[h] = s2
        e2_ref[h] = jnp.exp(s2 - v2[0:1])
        return carry

    lax.fori_loop(0, PEER_HEADS, head, 0)


def _peer_route(hT, wqT_hi, wqT_lo, subkeys, tc):
    D, T = hT.shape
    W = wqT_hi.shape[0]
    out = jax.ShapeDtypeStruct((PEER_HEADS, PEER_NKEYS, T), F32)
    ospec = pl.BlockSpec((PEER_HEADS, PEER_NKEYS, tc), lambda t: (0, 0, t))
    return pl.pallas_call(
        _route_kernel,
        grid=(T // tc,),
        in_specs=[
            pl.BlockSpec((D, tc), lambda t: (0, t)),
            pl.BlockSpec((W, D), lambda t: (0, 0)),
            pl.BlockSpec((W, D), lambda t: (0, 0)),
            pl.BlockSpec(subkeys.shape, lambda t: (0, 0, 0, 0)),
        ],
        out_specs=[ospec, ospec, ospec, ospec],
        out_shape=[out, out, out, out],
        scratch_shapes=[pltpu.VMEM((W, tc), F32)],
        compiler_params=_cparams(("parallel",)),
        name="peer_route",
    )(hT, wqT_hi, wqT_lo, subkeys)


def _rope_tables(L, rope):
    if not rope:
        return jnp.ones((L, 2 * HEAD_DIM), F32), jnp.zeros((L, 2 * HEAD_DIM), F32)
    pos = jnp.arange(L, dtype=jnp.int32)
    row = (pos // GRID_W).astype(F32)
    col = (pos % GRID_W).astype(F32)
    inv = jnp.power(ROPE_THETA, -jnp.arange(0, AXIS_DIM, 2, dtype=F32) / AXIS_DIM)
    ar = row[:, None] * inv[None, :]
    ac = col[:, None] * inv[None, :]
    cos = jnp.concatenate([jnp.cos(ar), jnp.cos(ar), jnp.cos(ac), jnp.cos(ac)], axis=-1)
    sin = jnp.concatenate([-jnp.sin(ar), jnp.sin(ar), -jnp.sin(ac), jnp.sin(ac)], axis=-1)
    return jnp.tile(cos, (1, 2)), jnp.tile(sin, (1, 2))


def _qk_prep_kernel(p_ref, cos_ref, sin_ref, qg_ref, kg_ref, bdq_ref, bdk_ref, q_ref, k_ref):
    x = p_ref[...].astype(F32)
    cos = cos_ref[...]
    sin = sin_ref[...]
    quarter = AXIS_DIM // 2

    def prep(xh, gain, bd):
        w = xh.shape[1]
        sq = xh * xh
        hi = sq.astype(BF16)
        lo = (sq - hi.astype(F32)).astype(BF16)
        ms = jnp.dot(hi, bd, preferred_element_type=F32) + jnp.dot(lo, bd, preferred_element_type=F32)
        y = xh * lax.rsqrt(ms + EPS) * gain
        lane = lax.broadcasted_iota(jnp.int32, y.shape, 1)
        first = (lane & (AXIS_DIM - 1)) < quarter
        partner = jnp.where(first, pltpu.roll(y, w - quarter, 1), pltpu.roll(y, quarter, 1))
        reps = w // cos.shape[1]
        c = jnp.concatenate([cos] * reps, axis=1) if reps > 1 else cos
        s = jnp.concatenate([sin] * reps, axis=1) if reps > 1 else sin
        return y * c + partner * s

    q_ref[...] = (prep(x[:, :ATT_W], qg_ref[...], bdq_ref[...]) * ATT_SCALE).astype(BF16)
    k_ref[...] = prep(x[:, ATT_W:ATT_W + KV_W], kg_ref[...], bdk_ref[...]).astype(BF16)


def _qk_prep(p, cos, sin, q_gain, k_gain, rows_per_batch, tm):
    T = p.shape[0]
    tpb = rows_per_batch // tm
    qg = jnp.tile(q_gain, N_HEADS)[None, :]
    kg = jnp.tile(k_gain, N_KV_HEADS)[None, :]

    def block_avg(w):
        hid = jnp.arange(w) // HEAD_DIM
        return jnp.where(hid[:, None] == hid[None, :], 1.0 / HEAD_DIM, 0.0).astype(BF16)

    full = lambda i: (0, 0)
    return pl.pallas_call(
        _qk_prep_kernel,
        grid=(T // tm,),
        in_specs=[
            pl.BlockSpec((tm, QKV_W), lambda i: (i, Q0 // QKV_W)),
            pl.BlockSpec((tm, 2 * HEAD_DIM), lambda i: (i % tpb, 0)),
            pl.BlockSpec((tm, 2 * HEAD_DIM), lambda i: (i % tpb, 0)),
            pl.BlockSpec((1, ATT_W), full),
            pl.BlockSpec((1, KV_W), full),
            pl.BlockSpec((ATT_W, ATT_W), full),
            pl.BlockSpec((KV_W, KV_W), full),
        ],
        out_specs=[pl.BlockSpec((tm, ATT_W), lambda i: (i, 0)), pl.BlockSpec((tm, KV_W), lambda i: (i, 0))],
        out_shape=[jax.ShapeDtypeStruct((T, ATT_W), BF16), jax.ShapeDtypeStruct((T, KV_W), BF16)],
        compiler_params=_cparams(("parallel",)),
        name="qk_prep",
    )(p, cos, sin, qg, kg, block_avg(ATT_W), block_avg(KV_W))


def _shift_rows(cur, prev8, next8, first, last):
    tm = cur.shape[0]
    rid = lax.broadcasted_iota(jnp.int32, cur.shape, 0)
    pr = jnp.where(first, 0.0, prev8[HALO - 1:HALO, :])
    nx = jnp.where(last, 0.0, next8[0:1, :])
    up = jnp.where(rid == 0, pr, pltpu.roll(cur, 1, 0))
    dn = jnp.where(rid == tm - 1, nx, pltpu.roll(cur, tm - 1, 0))
    return up, dn


def _halo_specs(tm, width, col_block, n_rows):
    r = tm // HALO
    last = n_rows // HALO - 1
    prev = pl.BlockSpec((1, HALO, width), lambda b, i: (b, jnp.maximum(i * r - 1, 0), col_block))
    nxt = pl.BlockSpec((1, HALO, width), lambda b, i: (b, jnp.minimum((i + 1) * r, last), col_block))
    return prev, nxt


def _short_conv_kernel(bg_ref, cg_ref, xs_ref, cgp_ref, xsp_ref, cgn_ref, xsn_ref, w_ref, o_ref):
    i = pl.program_id(1)
    f = lambda r: r[0].astype(F32)
    cur = f(cg_ref) * f(xs_ref)
    up, dn = _shift_rows(cur, f(cgp_ref) * f(xsp_ref), f(cgn_ref) * f(xsn_ref), i == 0, i == pl.num_programs(1) - 1)
    o_ref[0] = (f(bg_ref) * (up * w_ref[0:1] + cur * w_ref[1:2] + dn * w_ref[2:3])).astype(o_ref.dtype)


def _short_conv(p3, w, tm):
    B, L, _ = p3.shape
    c0 = SC0 // SC_W
    blk = lambda j: pl.BlockSpec((1, tm, SC_W), lambda b, i: (b, i, c0 + j))
    cgp, cgn = _halo_specs(tm, SC_W, c0 + 1, L)
    xsp, xsn = _halo_specs(tm, SC_W, c0 + 2, L)
    return pl.pallas_call(
        _short_conv_kernel,
        grid=(B, L // tm),
        in_specs=[blk(0), blk(1), blk(2), cgp, xsp, cgn, xsn, pl.BlockSpec((3, SC_W), lambda b, i: (0, 0))],
        out_specs=pl.BlockSpec((1, tm, SC_W), lambda b, i: (b, i, 0)),
        out_shape=jax.ShapeDtypeStruct((B, L, SC_W), BF16),
        compiler_params=_cparams(("parallel", "parallel")),
        name="short_conv",
    )(p3, p3, p3, p3, p3, p3, p3, w)


def _hy_pre_kernel(p_ref, pp_ref, pn_ref, w_ref, b_ref, v_ref, x1_ref, x2_ref):
    i = pl.program_id(1)
    cur = p_ref[0].astype(F32)
    up, dn = _shift_rows(cur, pp_ref[0].astype(F32), pn_ref[0].astype(F32), i == 0, i == pl.num_programs(1) - 1)
    u = up * w_ref[0:1] + cur * w_ref[1:2] + dn * w_ref[2:3] + b_ref[...]
    v_ref[0] = u[:, :HY_W].astype(BF16)
    x1_ref[0] = u[:, HY_W:2 * HY_W].astype(BF16)
    x2_ref[0] = u[:, 2 * HY_W:].astype(BF16)


def _hy_pre(p3, w, b, tm):
    B, L, _ = p3.shape
    W = (HY_ORDER + 1) * HY_W
    c0 = HY0 // W
    prev, nxt = _halo_specs(tm, W, c0, L)
    out = jax.ShapeDtypeStruct((B, L, HY_W), BF16)
    ospec = pl.BlockSpec((1, tm, HY_W), lambda b_, i: (b_, i, 0))
    return pl.pallas_call(
        _hy_pre_kernel,
        grid=(B, L // tm),
        in_specs=[pl.BlockSpec((1, tm, W), lambda b_, i: (b_, i, c0)), prev, nxt,
                  pl.BlockSpec((3, W), lambda b_, i: (0, 0)), pl.BlockSpec((1, W), lambda b_, i: (0, 0))],
        out_specs=[ospec, ospec, ospec],
        out_shape=[out, out, out],
        compiler_params=_cparams(("parallel", "parallel")),
        name="hyena_pre",
    )(p3, p3, p3, w, b[None, :])


def _hy_features(L):
    n = jnp.arange(2 * L, dtype=jnp.int32)
    j = jnp.where(n < L, n, jnp.where(n == L, 0, 2 * L - n)).astype(F32)[:, None]
    t = j / (L - 1)
    w = 2.0 * math.pi * j / L
    bands = jnp.linspace(1e-4, HY_BANDS - 1, HY_BANDS, dtype=F32)[None, :]
    z = jnp.concatenate([t, jnp.cos(bands * w), -jnp.sin(bands * w)], axis=-1)
    return jnp.pad(z, ((0, 0), (0, HY_FEAT - z.shape[1])))


def _hy_filter_kernel(z_ref, w1_ref, b1_ref, f1_ref, w2_ref, b2_ref, f2_ref, w3_ref, dec_ref,
                      k0_ref, k1_ref, sum_ref):
    i = pl.program_id(0)
    half = pl.num_programs(0) // 2

    @pl.when(i == 0)
    def _():
        sum_ref[...] = jnp.zeros(sum_ref.shape, F32)

    hp = lax.Precision.HIGHEST
    z = z_ref[...]
    h = jnp.sin(f1_ref[...] * (jnp.dot(z, w1_ref[...], preferred_element_type=F32, precision=hp) + b1_ref[...]))
    h = jnp.sin(f2_ref[...] * (jnp.dot(h, w2_ref[...], preferred_element_type=F32, precision=hp) + b2_ref[...]))
    k = jnp.dot(h, w3_ref[0], preferred_element_type=F32, precision=hp)
    k = k * jnp.exp(-z[:, 0:1] * dec_ref[...])
    sum_ref[...] += jnp.sum(jnp.abs(k), axis=0, keepdims=True)
    rid = lax.broadcasted_iota(jnp.int32, k.shape, 0)
    k = jnp.where(i == half, jnp.where(rid == 0, 0.0, k), k)
    k0_ref[...] = k[:, :HY_W].astype(BF16)
    k1_ref[...] = k[:, HY_W:].astype(BF16)


def _hy_filter(L, w1, b1, f1, w2, b2, f2, w3, decay, nb):
    N = 2 * L
    fh = w1.shape[1]
    z = _hy_features(L)
    w1p = jnp.pad(w1, ((0, HY_FEAT - w1.shape[0]), (0, 0)))
    w3d = w3.reshape(fh, HY_ORDER, 2, HY_W).transpose(2, 0, 1, 3).reshape(2, fh, HY_ORDER * HY_W)
    dec = jnp.abs(decay).reshape(1, HY_ORDER * HY_W)
    row = lambda a: a[None, :]
    full = lambda i: (0, 0)
    half = N // nb // 2
    k0, k1, tot = pl.pallas_call(
        _hy_filter_kernel,
        grid=(N // nb,),
        in_specs=[
            pl.BlockSpec((nb, HY_FEAT), lambda i: (i, 0)),
            pl.BlockSpec((HY_FEAT, fh), full), pl.BlockSpec((1, fh), full), pl.BlockSpec((1, fh), full),
            pl.BlockSpec((fh, fh), full), pl.BlockSpec((1, fh), full), pl.BlockSpec((1, fh), full),
            pl.BlockSpec((1, fh, HY_ORDER * HY_W), lambda i: (i // half, 0, 0)),
            pl.BlockSpec((1, HY_ORDER * HY_W), full),
        ],
        out_specs=[pl.BlockSpec((nb, HY_W), lambda i: (i, 0)), pl.BlockSpec((nb, HY_W), lambda i: (i, 0)),
                   pl.BlockSpec((1, HY_ORDER * HY_W), full)],
        out_shape=[jax.ShapeDtypeStruct((N, HY_W), BF16), jax.ShapeDtypeStruct((N, HY_W), BF16),
                   jax.ShapeDtypeStruct((1, HY_ORDER * HY_W), F32)],
        compiler_params=_cparams(("arbitrary",)),
        name="hyena_filter",
    )(z, w1p, row(b1), row(f1), w2, row(b2), row(f2), w3d, dec)
    return k0, k1, 1.0 / tot


def _dft_tables(N1, N2):
    N = N1 * N2
    ar = lambda n: jnp.arange(n, dtype=jnp.int32)

    def cs(m, period):
        ang = (-2.0 * math.pi / period) * (m % period).astype(F32)
        return jnp.cos(ang), jnp.sin(ang)

    fr, fi = cs(ar(N1)[:, None] * ar(N1)[None, :], N1)
    hr, hi = fr[:, :max(N1 // 2, 1)], fi[:, :max(N1 // 2, 1)]
    w_fwd = jnp.block([[hr, -hi], [hi, hr]])
    w_real = jnp.concatenate([fr, fi], axis=0)
    w_inv = jnp.block([[hr.T, hi.T], [-hi.T, hr.T]]) / N1
    gr, gi = cs(ar(N2)[None, None, :] * (ar(N1)[:, None, None] + N1 * ar(N2)[None, :, None]), N)
    g = jnp.concatenate([jnp.concatenate([gr, -gi], axis=2), jnp.concatenate([gi, gr], axis=2)], axis=1)
    grt, git = gr.transpose(0, 2, 1), gi.transpose(0, 2, 1)
    gh = jnp.concatenate([jnp.concatenate([grt, git], axis=2), jnp.concatenate([-git, grt], axis=2)], axis=1) / N2
    return tuple(a.astype(BF16) for a in (w_fwd, w_real, w_inv, g, gh))


def _colmm_kernel(w_ref, x_ref, o_ref):
    o_ref[...] = jnp.dot(w_ref[...], x_ref[...], preferred_element_type=F32).astype(o_ref.dtype)


def _colmm(w, x, cb):
    M, K = w.shape
    C = x.shape[1]
    return pl.pallas_call(
        _colmm_kernel,
        grid=(C // cb,),
        in_specs=[pl.BlockSpec((M, K), lambda j: (0, 0)), pl.BlockSpec((K, cb), lambda j: (0, j))],
        out_specs=pl.BlockSpec((M, cb), lambda j: (0, j)),
        out_shape=jax.ShapeDtypeStruct((M, C), BF16),
        compiler_params=_cparams(("parallel",)),
        name="dft_outer",
    )(w, x)


def _colmm_gate_kernel(w_ref, d_ref, u_ref, g_ref, b_ref, o_ref):
    y = jnp.dot(w_ref[...], d_ref[...], preferred_element_type=F32)
    o_ref[...] = (g_ref[...].astype(F32) * (y + u_ref[...].astype(F32) * b_ref[...])).astype(o_ref.dtype)


def _colmm_gate(w, d, u, gate, bias_row, cb):
    M, K = w.shape
    C = d.shape[1]
    col = lambda j: (0, j)
    return pl.pallas_call(
        _colmm_gate_kernel,
        grid=(C // cb,),
        in_specs=[pl.BlockSpec((M, K), lambda j: (0, 0)), pl.BlockSpec((K, cb), col), pl.BlockSpec((M, cb), col),
                  pl.BlockSpec((M, cb), col), pl.BlockSpec((1, cb), lambda j: (0, 0))],
        out_specs=pl.BlockSpec((M, cb), col),
        out_shape=jax.ShapeDtypeStruct((M, C), BF16),
        compiler_params=_cparams(("parallel",)),
        name="dft_outer_gate",
    )(w, d, u, gate, bias_row)


def _mid_fwd_kernel(a_ref, g_ref, s_ref, k_ref):
    n2 = a_ref.shape[2]
    a = a_ref[:, 0].reshape(2 * n2, a_ref.shape[3])
    x = jnp.dot(g_ref[0], a, preferred_element_type=F32) * s_ref[...]
    k_ref[:, 0] = x.reshape(2, n2, x.shape[1])


def _mid_fwd(a, g, scale):
    _, N1, N2, C = a.shape
    blk = pl.BlockSpec((2, 1, N2, C), lambda i: (0, i, 0, 0))
    return pl.pallas_call(
        _mid_fwd_kernel,
        grid=(N1,),
        in_specs=[blk, pl.BlockSpec((1, 2 * N2, 2 * N2), lambda i: (i, 0, 0)), pl.BlockSpec((1, C), lambda i: (0, 0))],
        out_specs=blk,
        out_shape=jax.ShapeDtypeStruct((2, N1, N2, C), F32),
        compiler_params=_cparams(("parallel",)),
        name="dft_inner_filter",
    )(a, g, scale)


def _mid_kernel(a_ref, g_ref, gh_ref, k_ref, d_ref):
    n2 = a_ref.shape[2]
    c = a_ref.shape[3]
    a = a_ref[:, 0].reshape(2 * n2, c)
    x = jnp.dot(g_ref[0], a, preferred_element_type=F32)
    xr, xi = x[:n2], x[n2:]
    kr, ki = k_ref[0, 0], k_ref[1, 0]
    y = jnp.concatenate([xr * kr - xi * ki, xr * ki + xi * kr], axis=0).astype(BF16)
    d = jnp.dot(gh_ref[0], y, preferred_element_type=F32)
    d_ref[:, 0] = d.reshape(2, n2, c).astype(d_ref.dtype)


def _mid(a, g, gh, kf):
    _, N1, N2, C = a.shape
    blk = pl.BlockSpec((2, 1, N2, C), lambda i: (0, i, 0, 0))
    tab = pl.BlockSpec((1, 2 * N2, 2 * N2), lambda i: (i, 0, 0))
    return pl.pallas_call(
        _mid_kernel,
        grid=(N1,),
        in_specs=[blk, tab, tab, blk],
        out_specs=blk,
        out_shape=jax.ShapeDtypeStruct((2, N1, N2, C), BF16),
        compiler_params=_cparams(("parallel",)),
        name="dft_inner",
    )(a, g, gh, kf)


def _hyena_long(v, x1, x2, filt, bias, tabs, cb):
    B, L, C = v.shape
    assert B == 2, "the two batches are packed as real / imaginary parts of one complex signal"
    w_fwd, w_real, w_inv, g, gh = tabs
    N1 = g.shape[0]
    N2 = g.shape[1] // 2
    k0, k1, inv_norm = filt
    flat = lambda a: a.reshape(-1, N2 * C)
    z = flat(v)
    for o, (ker, gate) in enumerate(((k0, x1), (k1, x2))):
        if N1 > 1:
            ka = _colmm(w_real, flat(ker), cb).reshape(2, N1, N2, C)
            za = _colmm(w_fwd, z, cb).reshape(2, N1, N2, C)
        else:
            ka = jnp.stack([ker, jnp.zeros_like(ker)]).reshape(2, 1, N2, C)
            za = jnp.pad(z.reshape(2, L, C), ((0, 0), (0, L), (0, 0))).reshape(2, 1, N2, C)
        kf = _mid_fwd(ka, g, inv_norm[:, o * C:(o + 1) * C])
        d = _mid(za, g, gh, kf)
        if N1 > 1:
            z = _colmm_gate(w_inv, flat(d), z, flat(gate), jnp.tile(bias[o], cb // C)[None, :], cb)
        else:
            y = d.reshape(2, N2, C)[:, :L].astype(F32)
            zf = z.reshape(2, L, C).astype(F32)
            z = (gate.astype(F32) * (y + zf * bias[o])).astype(BF16).reshape(-1, N2 * C // 2)
    return z.reshape(B, L, C)


def _resid_kernel(x_ref, f_ref, gt_ref, g_ref, o_ref, *, final):
    xn = x_ref[...] + gt_ref[0] * f_ref[...]
    if final:
        xn = xn * lax.rsqrt(jnp.mean(xn * xn, axis=-1, keepdims=True) + EPS) * g_ref[...]
    o_ref[...] = xn


def _resid(x2d, f2d, gt, g, rows_per_batch, tm, final):
    T, D = x2d.shape
    tpb = rows_per_batch // tm
    row = lambda i: (i, 0)
    return pl.pallas_call(
        functools.partial(_resid_kernel, final=final),
        grid=(T // tm,),
        in_specs=[pl.BlockSpec((tm, D), row), pl.BlockSpec((tm, D), row),
                  pl.BlockSpec((1, 1, D), lambda i: (i // tpb, 0, 0)), pl.BlockSpec((1, D), lambda i: (0, 0))],
        out_specs=pl.BlockSpec((tm, D), row),
        out_shape=jax.ShapeDtypeStruct((T, D), F32),
        compiler_params=_cparams(("parallel",)),
        name="residual",
    )(x2d, f2d, gt, g)


def _peer(h2, wq, subkeys, u_bf, vT_bf, tc, nb):
    T, D = h2.shape
    hT = h2.T
    wqT = wq.T
    wqT_hi = wqT.astype(BF16)
    wqT_lo = (wqT - wqT_hi.astype(F32)).astype(BF16)
    c, e1, s2, e2 = _peer_route(hT, wqT_hi, wqT_lo, subkeys, tc)
    c4 = c.reshape(PEER_HEADS, PEER_NKEYS // nb, nb, T)
    e14 = e1.reshape(PEER_HEADS, PEER_NKEYS // nb, nb, T)
    oT = _peer_dense(hT, s2, e2, c4, e14, u_bf, vT_bf, tc, nb)
    return oT.T


def kernel(x, c, ctx, c_ctx, w_mod, b_mod, g_norm1, g_norm2, w_in, q_gain, k_gain, sc_conv_w, hy_conv_w,
           hy_conv_b, hy_w1, hy_b1, hy_f1, hy_w2, hy_b2, hy_f2, hy_w3, hy_decay, hy_bias, w_br_att, w_br_sc,
           w_br_hy, w_out, peer_wq, peer_subkeys, peer_u, peer_v, g_final):
    B, S, D = x.shape
    Lc = ctx.shape[1]
    hp = lax.Precision.HIGHEST
    cond = jnp.concatenate([jax.nn.silu(c), jnp.broadcast_to(jax.nn.silu(c_ctx), (B, D))], axis=0)
    cos, sin = _rope_tables(S, True)
    cos_c, sin_c = _rope_tables(Lc, False)
    tm = min(1024, S)
    tq = min(128, S)
    tr = min(512, S)
    cb = 16 * HY_W
    tabs = _dft_tables(2 * S // HY_N2, HY_N2)
    tabs_c = _dft_tables(1, 2 * Lc)
    x2 = x.reshape(B * S, D)
    ctx2 = ctx.reshape(B * Lc, D)

    for l in range(DEPTH):
        need_ctx = l < DEPTH - 1
        mod = (jnp.dot(cond, w_mod[l], precision=hp) + b_mod[l]).reshape(2, B, 1, 6, D)
        sh1, sc1, gt1, sh2, sc2, gt2 = (mod[0, :, :, i] for i in range(6))
        csh1, csc1, cgt1, csh2, csc2, cgt2 = (mod[1, :, :, i] for i in range(6))
        w_in_bf = jnp.concatenate([w_in[l][:, REF_GT0:], w_in[l][:, REF_SC0:REF_GT0], w_in[l][:, :REF_SC0]],
                                  axis=1).astype(BF16)
        wa, ws, wh, wo = (w.astype(BF16) for w in (w_br_att[l], w_br_sc[l], w_br_hy[l], w_out[l]))
        g1 = g_norm1[l][None, :]
        g2 = g_norm2[l][None, :]
        hy_params = (hy_w1[l], hy_b1[l], hy_f1[l], hy_w2[l], hy_b2[l], hy_f2[l], hy_w3[l], hy_decay[l])

        p = _in_proj(x2, g1, sc1, sh1, w_in_bf, S, tm, 768)
        p3 = p.reshape(B, S, -1)
        pc = _in_proj(ctx2, g1, csc1, csh1, w_in_bf, Lc, Lc, 768)
        pc3 = pc.reshape(B, Lc, -1)

        qs, k = _qk_prep(p, cos, sin, q_gain[l], k_gain[l], S, tr)
        qcs, kc = _qk_prep(pc, cos_c, sin_c, q_gain[l], k_gain[l], Lc, Lc)
        k_all = jnp.concatenate([kc.reshape(B, Lc, KV_W), k.reshape(B, S, KV_W)], axis=1)
        v_all = jnp.concatenate([pc3[..., V0:IN_W], p3[..., V0:IN_W]], axis=1)
        tk = (S + Lc) // 13 if (S + Lc) % (13 * 128) == 0 else 128
        y_att = _attention(qs.reshape(B, S, ATT_W), k_all, v_all, tq, tk).reshape(B * S, ATT_W)

        y_sc = _short_conv(p3, sc_conv_w[l], tr).reshape(B * S, SC_W)
        y_hy = _hyena_long(*_hy_pre(p3, hy_conv_w[l], hy_conv_b[l], tr), _hy_filter(S, *hy_params, tr),
                           hy_bias[l], tabs, cb).reshape(B * S, HY_W)
        x2, h2 = _merge(x2, y_att, y_sc, y_hy, p, gt1, g2, sc2, sh2, wa, ws, wh, wo, S, tr)

        if need_ctx:
            yc_att = _attention(qcs.reshape(B, Lc, ATT_W), kc.reshape(B, Lc, KV_W), pc3[..., V0:IN_W], Lc, Lc)
            yc_sc = _short_conv(pc3, sc_conv_w[l], Lc).reshape(B * Lc, SC_W)
            yc_hy = _hyena_long(*_hy_pre(pc3, hy_conv_w[l], hy_conv_b[l], Lc), _hy_filter(Lc, *hy_params, Lc),
                                hy_bias[l], tabs_c, cb).reshape(B * Lc, HY_W)
            ctx2, h2c = _merge(ctx2, yc_att.reshape(B * Lc, ATT_W), yc_sc, yc_hy, pc, cgt1, g2, csc2, csh2,
                               wa, ws, wh, wo, Lc, Lc)
            tok = jnp.concatenate([h2, h2c], axis=0)
        else:
            tok = h2

        u_bf = peer_u[l].astype(BF16)
        vT_bf = peer_v[l].astype(BF16).T
        f = _peer(tok, peer_wq[l], peer_subkeys[l], u_bf, vT_bf, min(512, tok.shape[0]), 4)
        last = l == DEPTH - 1
        x2 = _resid(x2, f[:B * S], gt2, g_final[None, :], S, tr, last)
        if need_ctx:
            ctx2 = _resid(ctx2, f[B * S:], cgt2, g_final[None, :], Lc, Lc, False)

    return x2.reshape(B, S, D)
```

```python
import functools
import math

import jax
import jax.numpy as jnp
from jax import lax
from jax.experimental import pallas as pl
from jax.experimental.pallas import tpu as pltpu

F32 = jnp.float32
BF16 = jnp.bfloat16

DEPTH = 2
GRID_W = 64
EPS = 1e-6
N_HEADS = 8
N_KV_HEADS = 2
GQA_GROUP = N_HEADS // N_KV_HEADS
HEAD_DIM = 64
AXIS_DIM = HEAD_DIM // 2
ATT_W = N_HEADS * HEAD_DIM
KV_W = N_KV_HEADS * HEAD_DIM
ATT_SCALE = HEAD_DIM ** -0.5
ROPE_THETA = 10000.0
SC_W = 512
HY_W = 512
HY_ORDER = 2
HY_BANDS = 16
PEER_HEADS = 8
PEER_NKEYS = 128
PEER_DK = 128
PEER_TOPK = 16
D_MODEL = 1024
QKV_W = ATT_W + 2 * KV_W
REF_SC0 = QKV_W
REF_GT0 = QKV_W + 3 * SC_W + (HY_ORDER + 1) * HY_W
GT0 = 0
SC0 = GT0 + 3 * D_MODEL
HY0 = SC0 + 3 * SC_W
Q0 = HY0 + (HY_ORDER + 1) * HY_W
K0 = Q0 + ATT_W
V0 = K0 + KV_W
IN_W = V0 + KV_W
HALO = 8
HY_N2 = 256
HY_FEAT = 128

VMEM_LIMIT = 56 * 1024 * 1024


def _cparams(sem):
    return pltpu.CompilerParams(dimension_semantics=sem, vmem_limit_bytes=VMEM_LIMIT)


def _in_proj_kernel(x_ref, g_ref, sc_ref, sh_ref, w_ref, o_ref, h_scr):
    @pl.when(pl.program_id(1) == 0)
    def _():
        x = x_ref[...]
        ms = jnp.mean(x * x, axis=-1, keepdims=True)
        y = x * lax.rsqrt(ms + EPS) * g_ref[...]
        h_scr[...] = (y * (1.0 + sc_ref[0]) + sh_ref[0]).astype(BF16)

    o_ref[...] = jnp.dot(h_scr[...], w_ref[...], preferred_element_type=F32).astype(o_ref.dtype)


def _in_proj(x2d, g, sc, sh, w, rows_per_batch, tm, tn):
    T, D = x2d.shape
    N = w.shape[1]
    tpb = rows_per_batch // tm
    return pl.pallas_call(
        _in_proj_kernel,
        grid=(T // tm, N // tn),
        in_specs=[
            pl.BlockSpec((tm, D), lambda i, j: (i, 0)),
            pl.BlockSpec((1, D), lambda i, j: (0, 0)),
            pl.BlockSpec((1, 1, D), lambda i, j: (i // tpb, 0, 0)),
            pl.BlockSpec((1, 1, D), lambda i, j: (i // tpb, 0, 0)),
            pl.BlockSpec((D, tn), lambda i, j: (0, j)),
        ],
        out_specs=pl.BlockSpec((tm, tn), lambda i, j: (i, j)),
        out_shape=jax.ShapeDtypeStruct((T, N), BF16),
        scratch_shapes=[pltpu.VMEM((tm, D), BF16)],
        compiler_params=_cparams(("parallel", "arbitrary")),
        name="in_proj",
    )(x2d, g, sc, sh, w)


def _attn_kernel(q_ref, kT_ref, v_ref, o_ref, m_scr, acc_scr, sa_scr, sb_scr, *, nkb):
    m_scr[...] = jnp.full(m_scr.shape, -jnp.inf, F32)
    acc_scr[...] = jnp.zeros(acc_scr.shape, F32)

    def scores(j, dst_scr):
        dst_scr[...] = jnp.dot(q_ref[0, 0, 0], kT_ref[0, 0, j], preferred_element_type=F32)

    def update(j, src_scr):
        s = src_scr[...]
        m_prev = m_scr[...]
        m_new = jnp.maximum(m_prev, jnp.max(s, axis=-1, keepdims=True))
        p = jnp.exp(s - m_new).astype(BF16)
        alpha = jnp.exp(m_prev - m_new)
        acc_scr[...] = acc_scr[...] * alpha + jnp.dot(p, v_ref[0, 0, j], preferred_element_type=F32)
        m_scr[...] = m_new

    scores(0, sa_scr)

    def pair(i, carry):
        j = 2 * i
        scores(j + 1, sb_scr)
        update(j, sa_scr)
        scores(j + 2, sa_scr)
        update(j + 1, sb_scr)
        return carry

    lax.fori_loop(0, (nkb - 1) // 2, pair, 0)
    if nkb % 2 == 1:
        update(nkb - 1, sa_scr)
    else:
        scores(nkb - 1, sb_scr)
        update(nkb - 2, sa_scr)
        update(nkb - 1, sb_scr)
    acc = acc_scr[...]
    o_ref[0, 0, 0] = (acc[:, :HEAD_DIM] / acc[:, HEAD_DIM:HEAD_DIM + 1]).astype(o_ref.dtype)


def _attention(q, k, v, tq, tk):
    B, Lq, _ = q.shape
    Lk = k.shape[1]
    nqb, nkb = Lq // tq, Lk // tk
    R = GQA_GROUP * tq
    qb = q.reshape(B, nqb, tq, N_KV_HEADS, GQA_GROUP, HEAD_DIM).transpose(0, 3, 1, 4, 2, 5)
    qb = qb.reshape(B, N_KV_HEADS, nqb, R, HEAD_DIM)
    kT = k.reshape(B, nkb, tk, N_KV_HEADS, HEAD_DIM).transpose(0, 3, 1, 4, 2)
    vb = v.reshape(B, nkb, tk, N_KV_HEADS, HEAD_DIM).transpose(0, 3, 1, 2, 4)
    ones = jnp.ones(vb.shape[:-1] + (1,), BF16)
    zeros = jnp.zeros(vb.shape[:-1] + (HEAD_DIM - 1,), BF16)
    vb = jnp.concatenate([vb, ones, zeros], axis=-1)
    ob = pl.pallas_call(
        functools.partial(_attn_kernel, nkb=nkb),
        grid=(B, N_KV_HEADS, nqb),
        in_specs=[
            pl.BlockSpec((1, 1, 1, R, HEAD_DIM), lambda b, h, i: (b, h, i, 0, 0)),
            pl.BlockSpec((1, 1, nkb, HEAD_DIM, tk), lambda b, h, i: (b, h, 0, 0, 0)),
            pl.BlockSpec((1, 1, nkb, tk, 2 * HEAD_DIM), lambda b, h, i: (b, h, 0, 0, 0)),
        ],
        out_specs=pl.BlockSpec((1, 1, 1, R, HEAD_DIM), lambda b, h, i: (b, h, i, 0, 0)),
        out_shape=jax.ShapeDtypeStruct((B, N_KV_HEADS, nqb, R, HEAD_DIM), BF16),
        scratch_shapes=[pltpu.VMEM((R, 1), F32), pltpu.VMEM((R, 2 * HEAD_DIM), F32),
                        pltpu.VMEM((R, tk), F32), pltpu.VMEM((R, tk), F32)],
        compiler_params=_cparams(("parallel", "parallel", "arbitrary")),
        name="attention",
    )(qb, kT, vb)
    ob = ob.reshape(B, N_KV_HEADS, nqb, GQA_GROUP, tq, HEAD_DIM).transpose(0, 2, 4, 1, 3, 5)
    return ob.reshape(B, Lq, ATT_W)


def _merge_kernel(x_ref, ya_ref, ys_ref, yh_ref, ga_ref, gs_ref, gh_ref, gt_ref, g2_ref, sc_ref, sh_ref,
                  wa_ref, ws_ref, wh_ref, wo_ref, xo_ref, h2_ref):
    def br(y_ref, g_ref, w_ref):
        gate = jax.nn.sigmoid(g_ref[...].astype(F32))
        return gate * jnp.dot(y_ref[...], w_ref[...], preferred_element_type=F32)

    m = br(ya_ref, ga_ref, wa_ref) + br(ys_ref, gs_ref, ws_ref) + br(yh_ref, gh_ref, wh_ref)
    o = jnp.dot(m.astype(BF16), wo_ref[...], preferred_element_type=F32)
    xn = x_ref[...] + gt_ref[0] * o
    xo_ref[...] = xn
    ms = jnp.mean(xn * xn, axis=-1, keepdims=True)
    y = xn * lax.rsqrt(ms + EPS) * g2_ref[...]
    h2_ref[...] = (y * (1.0 + sc_ref[0]) + sh_ref[0]).astype(BF16)


def _merge(x2d, ya, ys, yh, p, gt1, g2, sc2, sh2, wa, ws, wh, wo, rows_per_batch, tm):
    T, D = x2d.shape
    tpb = rows_per_batch // tm
    gblk = GT0 // D
    row = lambda i: (i, 0)
    mod = lambda i: (i // tpb, 0, 0)
    full = lambda i: (0, 0)
    return pl.pallas_call(
        _merge_kernel,
        grid=(T // tm,),
        in_specs=[
            pl.BlockSpec((tm, D), row),
            pl.BlockSpec((tm, ATT_W), row),
            pl.BlockSpec((tm, SC_W), row),
            pl.BlockSpec((tm, HY_W), row),
            pl.BlockSpec((tm, D), lambda i: (i, gblk)),
            pl.BlockSpec((tm, D), lambda i: (i, gblk + 1)),
            pl.BlockSpec((tm, D), lambda i: (i, gblk + 2)),
            pl.BlockSpec((1, 1, D), mod),
            pl.BlockSpec((1, D), full),
            pl.BlockSpec((1, 1, D), mod),
            pl.BlockSpec((1, 1, D), mod),
            pl.BlockSpec((ATT_W, D), full),
            pl.BlockSpec((SC_W, D), full),
            pl.BlockSpec((HY_W, D), full),
            pl.BlockSpec((D, D), full),
        ],
        out_specs=[pl.BlockSpec((tm, D), row), pl.BlockSpec((tm, D), row)],
        out_shape=[jax.ShapeDtypeStruct((T, D), F32), jax.ShapeDtypeStruct((T, D), BF16)],
        compiler_params=_cparams(("parallel",)),
        name="merge",
    )(x2d, ya, ys, yh, p, p, p, gt1, g2, sc2, sh2, wa, ws, wh, wo)


LANES = 128
PACKED_ROWS = 16


def _peer_kernel(hT_ref, r2_ref, e2_ref, n_ref, e1_ref, u_ref, vT_ref, o_ref, gwa_scr, gwb_scr, act_scr, *, nb):
    e = pl.program_id(1)
    tc = hT_ref.shape[1]

    @pl.when(e == 0)
    def _():
        o_ref[...] = jnp.zeros(o_ref.shape, F32)
        gwb_scr[...] = jnp.zeros(gwb_scr.shape, BF16)

    def step(cur_scr, prev_scr):
        o_ref[...] += jnp.dot(vT_ref[...], prev_scr[...], preferred_element_type=F32)
        hT = hT_ref[...]
        for ii in range(nb):
            rows = slice(ii * PEER_NKEYS, (ii + 1) * PEER_NKEYS)
            a = jnp.dot(u_ref[rows, :], hT, preferred_element_type=F32)
            act_scr[rows, :] = (0.5 * a * (1.0 + lax.erf(a * (2.0 ** -0.5)))).astype(BF16)
        shape3 = (PEER_NKEYS // PACKED_ROWS, PACKED_ROWS, LANES)
        for lt in range(tc // LANES):
            lanes = slice(lt * LANES, (lt + 1) * LANES)
            ws = [jnp.zeros(shape3, BF16) for _ in range(nb)]
            for h in range(PEER_HEADS):
                r2t = pltpu.bitcast(r2_ref[h, :, lanes], BF16).reshape(shape3)
                e2t = pltpu.bitcast(e2_ref[h, :, lanes], BF16).reshape(shape3)
                for ii in range(nb):
                    cnt = jnp.broadcast_to(n_ref[h, 0, ii:ii + 1, lanes], (PACKED_ROWS, LANES)).astype(BF16)
                    e1row = jnp.broadcast_to(e1_ref[h, 0, ii:ii + 1, lanes], (PACKED_ROWS, LANES)).astype(BF16)
                    ws[ii] = ws[ii] + jnp.where(r2t < cnt[None], e2t, jnp.zeros_like(e2t)) * e1row[None]
            for ii in range(nb):
                rows = slice(ii * PEER_NKEYS, (ii + 1) * PEER_NKEYS)
                cur_scr[rows, lanes] = act_scr[rows, lanes] * ws[ii].reshape(PEER_NKEYS, LANES)

    @pl.when(e % 2 == 0)
    def _():
        step(gwa_scr, gwb_scr)

    @pl.when(e % 2 == 1)
    def _():
        step(gwb_scr, gwa_scr)


def _peer_dense(hT, r2, e2, cnt, e1, u, vT, tc, nb):
    D, T = hT.shape
    N = u.shape[0]
    eb = nb * PEER_NKEYS
    ne = N // eb
    assert ne % 2 == 0, "the drain step must find the last block in the buffer the parity rule reads"
    cur = lambda e: jnp.minimum(e, ne - 1)
    return pl.pallas_call(
        functools.partial(_peer_kernel, nb=nb),
        grid=(T // tc, ne + 1),
        in_specs=[
            pl.BlockSpec((D, tc), lambda t, e: (0, t)),
            pl.BlockSpec((PEER_HEADS, PEER_NKEYS // 2, tc), lambda t, e: (0, 0, t)),
            pl.BlockSpec((PEER_HEADS, PEER_NKEYS // 2, tc), lambda t, e: (0, 0, t)),
            pl.BlockSpec((PEER_HEADS, 1, nb, tc), lambda t, e: (0, cur(e), 0, t)),
            pl.BlockSpec((PEER_HEADS, 1, nb, tc), lambda t, e: (0, cur(e), 0, t)),
            pl.BlockSpec((eb, D), lambda t, e: (cur(e), 0)),
            pl.BlockSpec((D, eb), lambda t, e: (0, jnp.maximum(e - 1, 0))),
        ],
        out_specs=pl.BlockSpec((D, tc), lambda t, e: (0, t)),
        out_shape=jax.ShapeDtypeStruct((D, T), F32),
        scratch_shapes=[pltpu.VMEM((eb, tc), BF16), pltpu.VMEM((eb, tc), BF16), pltpu.VMEM((eb, tc), BF16)],
        compiler_params=_cparams(("parallel", "arbitrary")),
        name="peer_dense",
    )(hT, r2, e2, cnt, e1, u, vT)


ROUTE_ROWS = 24
NEG_INF = float("-inf")


def _top_rows(s, n):
    tc = s.shape[1]
    rid = lax.broadcasted_iota(jnp.int32, (ROUTE_ROWS, tc), 0)
    packed = jnp.full((ROUTE_ROWS, tc), NEG_INF, F32)
    for k in range(n):
        m = jnp.max(s, axis=0, keepdims=True)
        packed = jnp.where(rid == k, m, packed)
        s = jnp.where(s == m, NEG_INF, s)
    return packed


def _row_penalty(tc, lo, hi):
    rid = lax.broadcasted_iota(jnp.int32, (8, tc), 0)
    return jnp.where((rid >= lo) & (rid < hi), 0.0, NEG_INF).astype(F32)


def _route_kernel(hT_ref, wqh_ref, wql_ref, sk_ref, n_ref, e1_ref, r2_ref, e2_ref, q_scr):
    hT = hT_ref[...]
    q_scr[...] = (jnp.dot(wqh_ref[...], hT, preferred_element_type=F32)
                  + jnp.dot(wql_ref[...], hT, preferred_element_type=F32))
    tc = hT.shape[1]
    n = PEER_TOPK + 1
    half = PEER_DK // 2

    def head(h, carry):
        r0 = pl.multiple_of(h * PEER_DK, PEER_DK)
        hp = lax.Precision.HIGHEST
        s1 = jnp.dot(sk_ref[h, 0], q_scr[pl.ds(r0, half), :], preferred_element_type=F32, precision=hp)
        s2 = jnp.dot(sk_ref[h, 1], q_scr[pl.ds(r0 + half, half), :], preferred_element_type=F32, precision=hp)
        v1 = _top_rows(s1, n)
        v2 = _top_rows(s2, n)
        tiles = [v1[0:1] + v2[0:8], v1[0:1] + v2[8:16], v1[0:1] + v2[16:24], v1[1:2] + v2[0:8],
                 v1[2:3] + v2[0:8] + _row_penalty(tc, 0, n // 3), v1[3:4] + v2[0:8] + _row_penalty(tc, 0, n // 4),
                 v2[0:1] + v1[0:8] + _row_penalty(tc, 4, 8), v2[0:1] + v1[8:16], v2[0:1] + v1[16:24],
                 v2[1:2] + v1[0:8] + _row_penalty(tc, 4, n // 2), v2[2:3] + v1[0:8] + _row_penalty(tc, 4, n // 3)]
        top = _top_rows(jnp.concatenate(tiles, axis=0), n)
        theta = 0.5 * (top[PEER_TOPK - 1:PEER_TOPK] + top[PEER_TOPK:PEER_TOPK + 1])
        z = jnp.sum(jnp.exp(top[0:16] - top[0:1]), axis=0, keepdims=True)
        c = theta - s1
        cnt = jnp.zeros(c.shape, F32)
        rank = jnp.zeros(c.shape, F32)
        for b in range(n):
            cnt = cnt + jnp.where(v2[b:b + 1] >= c, 1.0, 0.0)
            rank = rank + jnp.where(v2[b:b + 1] > s2, 1.0, 0.0)
        n_ref[h] = cnt
        e1_ref[h] = jnp.exp(s1 - v1[0:1]) / z
        r2_ref[h] = pltpu.bitcast(rank.astype(BF16), jnp.uint32)
        e2_ref[h] = pltpu.bitcast(jnp.exp(s2 - v2[0:1]).astype(BF16), jnp.uint32)
        return carry

    lax.fori_loop(0, PEER_HEADS, head, 0)


def _peer_route(hT, wqT_hi, wqT_lo, subkeys, tc):
    D, T = hT.shape
    W = wqT_hi.shape[0]
    out = jax.ShapeDtypeStruct((PEER_HEADS, PEER_NKEYS, T), F32)
    out16 = jax.ShapeDtypeStruct((PEER_HEADS, PEER_NKEYS // 2, T), jnp.uint32)
    ospec = pl.BlockSpec((PEER_HEADS, PEER_NKEYS, tc), lambda t: (0, 0, t))
    ospec16 = pl.BlockSpec((PEER_HEADS, PEER_NKEYS // 2, tc), lambda t: (0, 0, t))
    return pl.pallas_call(
        _route_kernel,
        grid=(T // tc,),
        in_specs=[
            pl.BlockSpec((D, tc), lambda t: (0, t)),
            pl.BlockSpec((W, D), lambda t: (0, 0)),
            pl.BlockSpec((W, D), lambda t: (0, 0)),
            pl.BlockSpec(subkeys.shape, lambda t: (0, 0, 0, 0)),
        ],
        out_specs=[ospec, ospec, ospec16, ospec16],
        out_shape=[out, out, out16, out16],
        scratch_shapes=[pltpu.VMEM((W, tc), F32)],
        compiler_params=_cparams(("parallel",)),
        name="peer_route",
    )(hT, wqT_hi, wqT_lo, subkeys)


def _rope_tables(L, rope):
    if not rope:
        return jnp.ones((L, 2 * HEAD_DIM), F32), jnp.zeros((L, 2 * HEAD_DIM), F32)
    pos = jnp.arange(L, dtype=jnp.int32)
    row = (pos // GRID_W).astype(F32)
    col = (pos % GRID_W).astype(F32)
    inv = jnp.power(ROPE_THETA, -jnp.arange(0, AXIS_DIM, 2, dtype=F32) / AXIS_DIM)
    ar = row[:, None] * inv[None, :]
    ac = col[:, None] * inv[None, :]
    cos = jnp.concatenate([jnp.cos(ar), jnp.cos(ar), jnp.cos(ac), jnp.cos(ac)], axis=-1)
    sin = jnp.concatenate([-jnp.sin(ar), jnp.sin(ar), -jnp.sin(ac), jnp.sin(ac)], axis=-1)
    return jnp.tile(cos, (1, 2)), jnp.tile(sin, (1, 2))


def _qk_prep_kernel(p_ref, cos_ref, sin_ref, qg_ref, kg_ref, bdq_ref, bdk_ref, q_ref, k_ref):
    x = p_ref[...].astype(F32)
    cos = cos_ref[...]
    sin = sin_ref[...]
    quarter = AXIS_DIM // 2

    def prep(xh, gain, bd):
        w = xh.shape[1]
        sq = xh * xh
        hi = sq.astype(BF16)
        lo = (sq - hi.astype(F32)).astype(BF16)
        ms = jnp.dot(hi, bd, preferred_element_type=F32) + jnp.dot(lo, bd, preferred_element_type=F32)
        y = xh * lax.rsqrt(ms + EPS) * gain
        lane = lax.broadcasted_iota(jnp.int32, y.shape, 1)
        first = (lane & (AXIS_DIM - 1)) < quarter
        partner = jnp.where(first, pltpu.roll(y, w - quarter, 1), pltpu.roll(y, quarter, 1))
        reps = w // cos.shape[1]
        c = jnp.concatenate([cos] * reps, axis=1) if reps > 1 else cos
        s = jnp.concatenate([sin] * reps, axis=1) if reps > 1 else sin
        return y * c + partner * s

    q_ref[...] = (prep(x[:, :ATT_W], qg_ref[...], bdq_ref[...]) * ATT_SCALE).astype(BF16)
    k_ref[...] = prep(x[:, ATT_W:ATT_W + KV_W], kg_ref[...], bdk_ref[...]).astype(BF16)


def _qk_prep(p, cos, sin, q_gain, k_gain, rows_per_batch, tm):
    T = p.shape[0]
    tpb = rows_per_batch // tm
    qg = jnp.tile(q_gain, N_HEADS)[None, :]
    kg = jnp.tile(k_gain, N_KV_HEADS)[None, :]

    def block_avg(w):
        hid = jnp.arange(w) // HEAD_DIM
        return jnp.where(hid[:, None] == hid[None, :], 1.0 / HEAD_DIM, 0.0).astype(BF16)

    full = lambda i: (0, 0)
    return pl.pallas_call(
        _qk_prep_kernel,
        grid=(T // tm,),
        in_specs=[
            pl.BlockSpec((tm, QKV_W), lambda i: (i, Q0 // QKV_W)),
            pl.BlockSpec((tm, 2 * HEAD_DIM), lambda i: (i % tpb, 0)),
            pl.BlockSpec((tm, 2 * HEAD_DIM), lambda i: (i % tpb, 0)),
            pl.BlockSpec((1, ATT_W), full),
            pl.BlockSpec((1, KV_W), full),
            pl.BlockSpec((ATT_W, ATT_W), full),
            pl.BlockSpec((KV_W, KV_W), full),
        ],
        out_specs=[pl.BlockSpec((tm, ATT_W), lambda i: (i, 0)), pl.BlockSpec((tm, KV_W), lambda i: (i, 0))],
        out_shape=[jax.ShapeDtypeStruct((T, ATT_W), BF16), jax.ShapeDtypeStruct((T, KV_W), BF16)],
        compiler_params=_cparams(("parallel",)),
        name="qk_prep",
    )(p, cos, sin, qg, kg, block_avg(ATT_W), block_avg(KV_W))


def _shift_rows(cur, prev8, next8, first, last):
    tm = cur.shape[0]
    rid = lax.broadcasted_iota(jnp.int32, cur.shape, 0)
    pr = jnp.where(first, 0.0, prev8[HALO - 1:HALO, :])
    nx = jnp.where(last, 0.0, next8[0:1, :])
    up = jnp.where(rid == 0, pr, pltpu.roll(cur, 1, 0))
    dn = jnp.where(rid == tm - 1, nx, pltpu.roll(cur, tm - 1, 0))
    return up, dn


def _halo_specs(tm, width, col_block, n_rows):
    r = tm // HALO
    last = n_rows // HALO - 1
    prev = pl.BlockSpec((1, HALO, width), lambda b, i: (b, jnp.maximum(i * r - 1, 0), col_block))
    nxt = pl.BlockSpec((1, HALO, width), lambda b, i: (b, jnp.minimum((i + 1) * r, last), col_block))
    return prev, nxt


def _short_conv_kernel(bg_ref, cg_ref, xs_ref, cgp_ref, xsp_ref, cgn_ref, xsn_ref, w_ref, o_ref):
    i = pl.program_id(1)
    f = lambda r: r[0].astype(F32)
    cur = f(cg_ref) * f(xs_ref)
    up, dn = _shift_rows(cur, f(cgp_ref) * f(xsp_ref), f(cgn_ref) * f(xsn_ref), i == 0, i == pl.num_programs(1) - 1)
    o_ref[0] = (f(bg_ref) * (up * w_ref[0:1] + cur * w_ref[1:2] + dn * w_ref[2:3])).astype(o_ref.dtype)


def _short_conv(p3, w, tm):
    B, L, _ = p3.shape
    c0 = SC0 // SC_W
    blk = lambda j: pl.BlockSpec((1, tm, SC_W), lambda b, i: (b, i, c0 + j))
    cgp, cgn = _halo_specs(tm, SC_W, c0 + 1, L)
    xsp, xsn = _halo_specs(tm, SC_W, c0 + 2, L)
    return pl.pallas_call(
        _short_conv_kernel,
        grid=(B, L // tm),
        in_specs=[blk(0), blk(1), blk(2), cgp, xsp, cgn, xsn, pl.BlockSpec((3, SC_W), lambda b, i: (0, 0))],
        out_specs=pl.BlockSpec((1, tm, SC_W), lambda b, i: (b, i, 0)),
        out_shape=jax.ShapeDtypeStruct((B, L, SC_W), BF16),
        compiler_params=_cparams(("parallel", "parallel")),
        name="short_conv",
    )(p3, p3, p3, p3, p3, p3, p3, w)


def _hy_pre_kernel(p_ref, pp_ref, pn_ref, w_ref, b_ref, v_ref, x1_ref, x2_ref):
    i = pl.program_id(1)
    cur = p_ref[0].astype(F32)
    up, dn = _shift_rows(cur, pp_ref[0].astype(F32), pn_ref[0].astype(F32), i == 0, i == pl.num_programs(1) - 1)
    u = up * w_ref[0:1] + cur * w_ref[1:2] + dn * w_ref[2:3] + b_ref[...]
    v_ref[0] = u[:, :HY_W].astype(BF16)
    x1_ref[0] = u[:, HY_W:2 * HY_W].astype(BF16)
    x2_ref[0] = u[:, 2 * HY_W:].astype(BF16)


def _hy_pre(p3, w, b, tm):
    B, L, _ = p3.shape
    W = (HY_ORDER + 1) * HY_W
    c0 = HY0 // W
    prev, nxt = _halo_specs(tm, W, c0, L)
    out = jax.ShapeDtypeStruct((B, L, HY_W), BF16)
    ospec = pl.BlockSpec((1, tm, HY_W), lambda b_, i: (b_, i, 0))
    return pl.pallas_call(
        _hy_pre_kernel,
        grid=(B, L // tm),
        in_specs=[pl.BlockSpec((1, tm, W), lambda b_, i: (b_, i, c0)), prev, nxt,
                  pl.BlockSpec((3, W), lambda b_, i: (0, 0)), pl.BlockSpec((1, W), lambda b_, i: (0, 0))],
        out_specs=[ospec, ospec, ospec],
        out_shape=[out, out, out],
        compiler_params=_cparams(("parallel", "parallel")),
        name="hyena_pre",
    )(p3, p3, p3, w, b[None, :])


def _hy_features(L):
    n = jnp.arange(2 * L, dtype=jnp.int32)
    j = jnp.where(n < L, n, jnp.where(n == L, 0, 2 * L - n)).astype(F32)[:, None]
    t = j / (L - 1)
    w = 2.0 * math.pi * j / L
    bands = jnp.linspace(1e-4, HY_BANDS - 1, HY_BANDS, dtype=F32)[None, :]
    z = jnp.concatenate([t, jnp.cos(bands * w), -jnp.sin(bands * w)], axis=-1)
    return jnp.pad(z, ((0, 0), (0, HY_FEAT - z.shape[1])))


def _hy_filter_kernel(z_ref, w1_ref, b1_ref, f1_ref, w2_ref, b2_ref, f2_ref, w3_ref, dec_ref,
                      k0_ref, k1_ref, sum_ref):
    i = pl.program_id(0)
    half = pl.num_programs(0) // 2

    @pl.when(i == 0)
    def _():
        sum_ref[...] = jnp.zeros(sum_ref.shape, F32)

    hp = lax.Precision.HIGHEST
    z = z_ref[...]
    h = jnp.sin(f1_ref[...] * (jnp.dot(z, w1_ref[...], preferred_element_type=F32, precision=hp) + b1_ref[...]))
    h = jnp.sin(f2_ref[...] * (jnp.dot(h, w2_ref[...], preferred_element_type=F32, precision=hp) + b2_ref[...]))
    k = jnp.dot(h, w3_ref[0], preferred_element_type=F32, precision=hp)
    k = k * jnp.exp(-z[:, 0:1] * dec_ref[...])
    sum_ref[...] += jnp.sum(jnp.abs(k), axis=0, keepdims=True)
    rid = lax.broadcasted_iota(jnp.int32, k.shape, 0)
    k = jnp.where(i == half, jnp.where(rid == 0, 0.0, k), k)
    k0_ref[...] = k[:, :HY_W].astype(BF16)
    k1_ref[...] = k[:, HY_W:].astype(BF16)


def _hy_filter(L, w1, b1, f1, w2, b2, f2, w3, decay, nb):
    N = 2 * L
    fh = w1.shape[1]
    z = _hy_features(L)
    w1p = jnp.pad(w1, ((0, HY_FEAT - w1.shape[0]), (0, 0)))
    w3d = w3.reshape(fh, HY_ORDER, 2, HY_W).transpose(2, 0, 1, 3).reshape(2, fh, HY_ORDER * HY_W)
    dec = jnp.abs(decay).reshape(1, HY_ORDER * HY_W)
    row = lambda a: a[None, :]
    full = lambda i: (0, 0)
    half = N // nb // 2
    k0, k1, tot = pl.pallas_call(
        _hy_filter_kernel,
        grid=(N // nb,),
        in_specs=[
            pl.BlockSpec((nb, HY_FEAT), lambda i: (i, 0)),
            pl.BlockSpec((HY_FEAT, fh), full), pl.BlockSpec((1, fh), full), pl.BlockSpec((1, fh), full),
            pl.BlockSpec((fh, fh), full), pl.BlockSpec((1, fh), full), pl.BlockSpec((1, fh), full),
            pl.BlockSpec((1, fh, HY_ORDER * HY_W), lambda i: (i // half, 0, 0)),
            pl.BlockSpec((1, HY_ORDER * HY_W), full),
        ],
        out_specs=[pl.BlockSpec((nb, HY_W), lambda i: (i, 0)), pl.BlockSpec((nb, HY_W), lambda i: (i, 0)),
                   pl.BlockSpec((1, HY_ORDER * HY_W), full)],
        out_shape=[jax.ShapeDtypeStruct((N, HY_W), BF16), jax.ShapeDtypeStruct((N, HY_W), BF16),
                   jax.ShapeDtypeStruct((1, HY_ORDER * HY_W), F32)],
        compiler_params=_cparams(("arbitrary",)),
        name="hyena_filter",
    )(z, w1p, row(b1), row(f1), w2, row(b2), row(f2), w3d, dec)
    return k0, k1, 1.0 / tot


def _dft_tables(N1, N2):
    N = N1 * N2
    ar = lambda n: jnp.arange(n, dtype=jnp.int32)

    def cs(m, period):
        ang = (-2.0 * math.pi / period) * (m % period).astype(F32)
        return jnp.cos(ang), jnp.sin(ang)

    fr, fi = cs(ar(N1)[:, None] * ar(N1)[None, :], N1)
    hr, hi = fr[:, :max(N1 // 2, 1)], fi[:, :max(N1 // 2, 1)]
    w_fwd = jnp.block([[hr, -hi], [hi, hr]])
    w_real = jnp.concatenate([fr, fi], axis=0)
    w_inv = jnp.block([[hr.T, hi.T], [-hi.T, hr.T]]) / N1
    gr, gi = cs(ar(N2)[None, None, :] * (ar(N1)[:, None, None] + N1 * ar(N2)[None, :, None]), N)
    g = jnp.concatenate([jnp.concatenate([gr, -gi], axis=2), jnp.concatenate([gi, gr], axis=2)], axis=1)
    grt, git = gr.transpose(0, 2, 1), gi.transpose(0, 2, 1)
    gh = jnp.concatenate([jnp.concatenate([grt, git], axis=2), jnp.concatenate([-git, grt], axis=2)], axis=1) / N2
    return tuple(a.astype(BF16) for a in (w_fwd, w_real, w_inv, g, gh))


def _colmm_kernel(w_ref, x_ref, o_ref):
    o_ref[...] = jnp.dot(w_ref[...], x_ref[...], preferred_element_type=F32).astype(o_ref.dtype)


def _colmm(w, x, cb):
    M, K = w.shape
    C = x.shape[1]
    return pl.pallas_call(
        _colmm_kernel,
        grid=(C // cb,),
        in_specs=[pl.BlockSpec((M, K), lambda j: (0, 0)), pl.BlockSpec((K, cb), lambda j: (0, j))],
        out_specs=pl.BlockSpec((M, cb), lambda j: (0, j)),
        out_shape=jax.ShapeDtypeStruct((M, C), BF16),
        compiler_params=_cparams(("parallel",)),
        name="dft_outer",
    )(w, x)


def _colmm_gate_kernel(w_ref, d_ref, u_ref, g_ref, b_ref, o_ref):
    y = jnp.dot(w_ref[...], d_ref[...], preferred_element_type=F32)
    o_ref[...] = (g_ref[...].astype(F32) * (y + u_ref[...].astype(F32) * b_ref[...])).astype(o_ref.dtype)


def _colmm_gate(w, d, u, gate, bias_row, cb):
    M, K = w.shape
    C = d.shape[1]
    col = lambda j: (0, j)
    return pl.pallas_call(
        _colmm_gate_kernel,
        grid=(C // cb,),
        in_specs=[pl.BlockSpec((M, K), lambda j: (0, 0)), pl.BlockSpec((K, cb), col), pl.BlockSpec((M, cb), col),
                  pl.BlockSpec((M, cb), col), pl.BlockSpec((1, cb), lambda j: (0, 0))],
        out_specs=pl.BlockSpec((M, cb), col),
        out_shape=jax.ShapeDtypeStruct((M, C), BF16),
        compiler_params=_cparams(("parallel",)),
        name="dft_outer_gate",
    )(w, d, u, gate, bias_row)


def _mid_fwd_kernel(a_ref, g_ref, s_ref, k_ref):
    n2 = a_ref.shape[2]
    a = a_ref[:, 0].reshape(2 * n2, a_ref.shape[3])
    x = jnp.dot(g_ref[0], a, preferred_element_type=F32) * s_ref[...]
    k_ref[:, 0] = x.reshape(2, n2, x.shape[1])


def _mid_fwd(a, g, scale):
    _, N1, N2, C = a.shape
    blk = pl.BlockSpec((2, 1, N2, C), lambda i: (0, i, 0, 0))
    return pl.pallas_call(
        _mid_fwd_kernel,
        grid=(N1,),
        in_specs=[blk, pl.BlockSpec((1, 2 * N2, 2 * N2), lambda i: (i, 0, 0)), pl.BlockSpec((1, C), lambda i: (0, 0))],
        out_specs=blk,
        out_shape=jax.ShapeDtypeStruct((2, N1, N2, C), F32),
        compiler_params=_cparams(("parallel",)),
        name="dft_inner_filter",
    )(a, g, scale)


def _mid_kernel(a_ref, g_ref, gh_ref, k_ref, d_ref):
    n2 = a_ref.shape[2]
    c = a_ref.shape[3]
    a = a_ref[:, 0].reshape(2 * n2, c)
    x = jnp.dot(g_ref[0], a, preferred_element_type=F32)
    xr, xi = x[:n2], x[n2:]
    kr, ki = k_ref[0, 0], k_ref[1, 0]
    y = jnp.concatenate([xr * kr - xi * ki, xr * ki + xi * kr], axis=0).astype(BF16)
    d = jnp.dot(gh_ref[0], y, preferred_element_type=F32)
    d_ref[:, 0] = d.reshape(2, n2, c).astype(d_ref.dtype)


def _mid(a, g, gh, kf):
    _, N1, N2, C = a.shape
    blk = pl.BlockSpec((2, 1, N2, C), lambda i: (0, i, 0, 0))
    tab = pl.BlockSpec((1, 2 * N2, 2 * N2), lambda i: (i, 0, 0))
    return pl.pallas_call(
        _mid_kernel,
        grid=(N1,),
        in_specs=[blk, tab, tab, blk],
        out_specs=blk,
        out_shape=jax.ShapeDtypeStruct((2, N1, N2, C), BF16),
        compiler_params=_cparams(("parallel",)),
        name="dft_inner",
    )(a, g, gh, kf)


def _hyena_long(v, x1, x2, filt, bias, tabs, cb):
    B, L, C = v.shape
    assert B == 2, "the two batches are packed as real / imaginary parts of one complex signal"
    w_fwd, w_real, w_inv, g, gh = tabs
    N1 = g.shape[0]
    N2 = g.shape[1] // 2
    k0, k1, inv_norm = filt
    flat = lambda a: a.reshape(-1, N2 * C)
    z = flat(v)
    for o, (ker, gate) in enumerate(((k0, x1), (k1, x2))):
        if N1 > 1:
            ka = _colmm(w_real, flat(ker), cb).reshape(2, N1, N2, C)
            za = _colmm(w_fwd, z, cb).reshape(2, N1, N2, C)
        else:
            ka = jnp.stack([ker, jnp.zeros_like(ker)]).reshape(2, 1, N2, C)
            za = jnp.pad(z.reshape(2, L, C), ((0, 0), (0, L), (0, 0))).reshape(2, 1, N2, C)
        kf = _mid_fwd(ka, g, inv_norm[:, o * C:(o + 1) * C])
        d = _mid(za, g, gh, kf)
        if N1 > 1:
            z = _colmm_gate(w_inv, flat(d), z, flat(gate), jnp.tile(bias[o], cb // C)[None, :], cb)
        else:
            y = d.reshape(2, N2, C)[:, :L].astype(F32)
            zf = z.reshape(2, L, C).astype(F32)
            z = (gate.astype(F32) * (y + zf * bias[o])).astype(BF16).reshape(-1, N2 * C // 2)
    return z.reshape(B, L, C)


def _resid_kernel(x_ref, f_ref, gt_ref, g_ref, o_ref, *, final):
    xn = x_ref[...] + gt_ref[0] * f_ref[...]
    if final:
        xn = xn * lax.rsqrt(jnp.mean(xn * xn, axis=-1, keepdims=True) + EPS) * g_ref[...]
    o_ref[...] = xn


def _resid(x2d, f2d, gt, g, rows_per_batch, tm, final):
    T, D = x2d.shape
    tpb = rows_per_batch // tm
    row = lambda i: (i, 0)
    return pl.pallas_call(
        functools.partial(_resid_kernel, final=final),
        grid=(T // tm,),
        in_specs=[pl.BlockSpec((tm, D), row), pl.BlockSpec((tm, D), row),
                  pl.BlockSpec((1, 1, D), lambda i: (i // tpb, 0, 0)), pl.BlockSpec((1, D), lambda i: (0, 0))],
        out_specs=pl.BlockSpec((tm, D), row),
        out_shape=jax.ShapeDtypeStruct((T, D), F32),
        compiler_params=_cparams(("parallel",)),
        name="residual",
    )(x2d, f2d, gt, g)


def _peer(h2, wq, subkeys, u_bf, vT_bf, tc, nb):
    T, D = h2.shape
    hT = h2.T
    wqT = wq.T
    wqT_hi = wqT.astype(BF16)
    wqT_lo = (wqT - wqT_hi.astype(F32)).astype(BF16)
    cnt, e1, r2, e2 = _peer_route(hT, wqT_hi, wqT_lo, subkeys, tc)
    cnt4 = cnt.reshape(PEER_HEADS, PEER_NKEYS // nb, nb, T)
    e14 = e1.reshape(PEER_HEADS, PEER_NKEYS // nb, nb, T)
    oT = _peer_dense(hT, r2, e2, cnt4, e14, u_bf, vT_bf, tc, nb)
    return oT.T


def kernel(x, c, ctx, c_ctx, w_mod, b_mod, g_norm1, g_norm2, w_in, q_gain, k_gain, sc_conv_w, hy_conv_w,
           hy_conv_b, hy_w1, hy_b1, hy_f1, hy_w2, hy_b2, hy_f2, hy_w3, hy_decay, hy_bias, w_br_att, w_br_sc,
           w_br_hy, w_out, peer_wq, peer_subkeys, peer_u, peer_v, g_final):
    B, S, D = x.shape
    Lc = ctx.shape[1]
    hp = lax.Precision.HIGHEST
    cond = jnp.concatenate([jax.nn.silu(c), jnp.broadcast_to(jax.nn.silu(c_ctx), (B, D))], axis=0)
    cos, sin = _rope_tables(S, True)
    cos_c, sin_c = _rope_tables(Lc, False)
    tm = min(1024, S)
    tq = min(128, S)
    tr = min(512, S)
    cb = 16 * HY_W
    tabs = _dft_tables(2 * S // HY_N2, HY_N2)
    tabs_c = _dft_tables(1, 2 * Lc)
    x2 = x.reshape(B * S, D)
    ctx2 = ctx.reshape(B * Lc, D)

    for l in range(DEPTH):
        need_ctx = l < DEPTH - 1
        mod = (jnp.dot(cond, w_mod[l], precision=hp) + b_mod[l]).reshape(2, B, 1, 6, D)
        sh1, sc1, gt1, sh2, sc2, gt2 = (mod[0, :, :, i] for i in range(6))
        csh1, csc1, cgt1, csh2, csc2, cgt2 = (mod[1, :, :, i] for i in range(6))
        w_in_bf = jnp.concatenate([w_in[l][:, REF_GT0:], w_in[l][:, REF_SC0:REF_GT0], w_in[l][:, :REF_SC0]],
                                  axis=1).astype(BF16)
        wa, ws, wh, wo = (w.astype(BF16) for w in (w_br_att[l], w_br_sc[l], w_br_hy[l], w_out[l]))
        g1 = g_norm1[l][None, :]
        g2 = g_norm2[l][None, :]
        hy_params = (hy_w1[l], hy_b1[l], hy_f1[l], hy_w2[l], hy_b2[l], hy_f2[l], hy_w3[l], hy_decay[l])

        p = _in_proj(x2, g1, sc1, sh1, w_in_bf, S, tm, 768)
        p3 = p.reshape(B, S, -1)
        pc = _in_proj(ctx2, g1, csc1, csh1, w_in_bf, Lc, Lc, 768)
        pc3 = pc.reshape(B, Lc, -1)

        qs, k = _qk_prep(p, cos, sin, q_gain[l], k_gain[l], S, tr)
        qcs, kc = _qk_prep(pc, cos_c, sin_c, q_gain[l], k_gain[l], Lc, Lc)
        k_all = jnp.concatenate([kc.reshape(B, Lc, KV_W), k.reshape(B, S, KV_W)], axis=1)
        v_all = jnp.concatenate([pc3[..., V0:IN_W], p3[..., V0:IN_W]], axis=1)
        tk = (S + Lc) // 13 if (S + Lc) % (13 * 128) == 0 else 128
        y_att = _attention(qs.reshape(B, S, ATT_W), k_all, v_all, tq, tk).reshape(B * S, ATT_W)

        y_sc = _short_conv(p3, sc_conv_w[l], tr).reshape(B * S, SC_W)
        y_hy = _hyena_long(*_hy_pre(p3, hy_conv_w[l], hy_conv_b[l], tr), _hy_filter(S, *hy_params, tr),
                           hy_bias[l], tabs, cb).reshape(B * S, HY_W)
        x2, h2 = _merge(x2, y_att, y_sc, y_hy, p, gt1, g2, sc2, sh2, wa, ws, wh, wo, S, tr)

        if need_ctx:
            yc_att = _attention(qcs.reshape(B, Lc, ATT_W), kc.reshape(B, Lc, KV_W), pc3[..., V0:IN_W], Lc, Lc)
            yc_sc = _short_conv(pc3, sc_conv_w[l], Lc).reshape(B * Lc, SC_W)
            yc_hy = _hyena_long(*_hy_pre(pc3, hy_conv_w[l], hy_conv_b[l], Lc), _hy_filter(Lc, *hy_params, Lc),
                                hy_bias[l], tabs_c, cb).reshape(B * Lc, HY_W)
            ctx2, h2c = _merge(ctx2, yc_att.reshape(B * Lc, ATT_W), yc_sc, yc_hy, pc, cgt1, g2, csc2, csh2,
                               wa, ws, wh, wo, Lc, Lc)
            tok = jnp.concatenate([h2, h2c], axis=0)
        else:
            tok = h2

        u_bf = peer_u[l].astype(BF16)
        vT_bf = peer_v[l].astype(BF16).T
        f = _peer(tok, peer_wq[l], peer_subkeys[l], u_bf, vT_bf, min(512, tok.shape[0]), 4)
        last = l == DEPTH - 1
        x2 = _resid(x2, f[:B * S], gt2, g_final[None, :], S, tr, last)
        if need_ctx:
            ctx2 = _resid(ctx2, f[B * S:], cgt2, g_final[None, :], Lc, Lc, False)

    return x2.reshape(B, S, D)
```

```python
import functools
import math

import jax
import jax.numpy as jnp
from jax import lax
from jax.experimental import pallas as pl
from jax.experimental.pallas import tpu as pltpu

F32 = jnp.float32
BF16 = jnp.bfloat16

DEPTH = 2
GRID_W = 64
EPS = 1e-6
N_HEADS = 8
N_KV_HEADS = 2
GQA_GROUP = N_HEADS // N_KV_HEADS
HEAD_DIM = 64
AXIS_DIM = HEAD_DIM // 2
ATT_W = N_HEADS * HEAD_DIM
KV_W = N_KV_HEADS * HEAD_DIM
ATT_SCALE = HEAD_DIM ** -0.5
ROPE_THETA = 10000.0
SC_W = 512
HY_W = 512
HY_ORDER = 2
HY_BANDS = 16
PEER_HEADS = 8
PEER_NKEYS = 128
PEER_DK = 128
PEER_TOPK = 16
D_MODEL = 1024
QKV_W = ATT_W + 2 * KV_W
REF_SC0 = QKV_W
REF_GT0 = QKV_W + 3 * SC_W + (HY_ORDER + 1) * HY_W
GT0 = 0
SC0 = GT0 + 3 * D_MODEL
HY0 = SC0 + 3 * SC_W
Q0 = HY0 + (HY_ORDER + 1) * HY_W
K0 = Q0 + ATT_W
V0 = K0 + KV_W
IN_W = V0 + KV_W
HALO = 8
HY_N2 = 256
HY_FEAT = 128

VMEM_LIMIT = 56 * 1024 * 1024


def _cparams(sem):
    return pltpu.CompilerParams(dimension_semantics=sem, vmem_limit_bytes=VMEM_LIMIT)


def _in_proj_kernel(x_ref, g_ref, sc_ref, sh_ref, w_ref, o_ref, h_scr):
    @pl.when(pl.program_id(1) == 0)
    def _():
        x = x_ref[...]
        ms = jnp.mean(x * x, axis=-1, keepdims=True)
        y = x * lax.rsqrt(ms + EPS) * g_ref[...]
        h_scr[...] = (y * (1.0 + sc_ref[0]) + sh_ref[0]).astype(BF16)

    o_ref[...] = jnp.dot(h_scr[...], w_ref[...], preferred_element_type=F32).astype(o_ref.dtype)


def _in_proj(x2d, g, sc, sh, w, rows_per_batch, tm, tn):
    T, D = x2d.shape
    N = w.shape[1]
    tpb = rows_per_batch // tm
    return pl.pallas_call(
        _in_proj_kernel,
        grid=(T // tm, N // tn),
        in_specs=[
            pl.BlockSpec((tm, D), lambda i, j: (i, 0)),
            pl.BlockSpec((1, D), lambda i, j: (0, 0)),
            pl.BlockSpec((1, 1, D), lambda i, j: (i // tpb, 0, 0)),
            pl.BlockSpec((1, 1, D), lambda i, j: (i // tpb, 0, 0)),
            pl.BlockSpec((D, tn), lambda i, j: (0, j)),
        ],
        out_specs=pl.BlockSpec((tm, tn), lambda i, j: (i, j)),
        out_shape=jax.ShapeDtypeStruct((T, N), BF16),
        scratch_shapes=[pltpu.VMEM((tm, D), BF16)],
        compiler_params=_cparams(("parallel", "arbitrary")),
        name="in_proj",
    )(x2d, g, sc, sh, w)


def _attn_kernel(q_ref, kT_ref, v_ref, o_ref, q_scr, m_scr, acc_scr, sa_scr, sb_scr, *, nkb):
    tq = q_ref.shape[1]
    m_scr[...] = jnp.full(m_scr.shape, -jnp.inf, F32)
    acc_scr[...] = jnp.zeros(acc_scr.shape, F32)
    for g in range(GQA_GROUP):
        q_scr[g * tq:(g + 1) * tq, :] = q_ref[0, :, g * HEAD_DIM:(g + 1) * HEAD_DIM]

    def scores(j, dst_scr):
        dst_scr[...] = jnp.dot(q_scr[...], kT_ref[0, 0, j], preferred_element_type=F32)

    def update(j, src_scr):
        s = src_scr[...]
        m_prev = m_scr[...]
        m_new = jnp.maximum(m_prev, jnp.max(s, axis=-1, keepdims=True))
        p = jnp.exp(s - m_new).astype(BF16)
        alpha = jnp.exp(m_prev - m_new)
        acc_scr[...] = acc_scr[...] * alpha + jnp.dot(p, v_ref[0, 0, j], preferred_element_type=F32)
        m_scr[...] = m_new

    scores(0, sa_scr)

    def pair(i, carry):
        j = 2 * i
        scores(j + 1, sb_scr)
        update(j, sa_scr)
        scores(j + 2, sa_scr)
        update(j + 1, sb_scr)
        return carry

    lax.fori_loop(0, (nkb - 1) // 2, pair, 0)
    if nkb % 2 == 1:
        update(nkb - 1, sa_scr)
    else:
        scores(nkb - 1, sb_scr)
        update(nkb - 2, sa_scr)
        update(nkb - 1, sb_scr)
    acc = acc_scr[...]
    o = (acc[:, :HEAD_DIM] / acc[:, HEAD_DIM:HEAD_DIM + 1]).astype(o_ref.dtype)
    o_ref[0] = jnp.concatenate([o[g * tq:(g + 1) * tq] for g in range(GQA_GROUP)], axis=1)


def _attention(q, k, v, tq, tk):
    B, Lq, _ = q.shape
    Lk = k.shape[1]
    nqb, nkb = Lq // tq, Lk // tk
    R = GQA_GROUP * tq
    GW = GQA_GROUP * HEAD_DIM
    kT = k.reshape(B, nkb, tk, N_KV_HEADS, HEAD_DIM).transpose(0, 3, 1, 4, 2)
    vb = v.reshape(B, nkb, tk, N_KV_HEADS, HEAD_DIM).transpose(0, 3, 1, 2, 4)
    ones = jnp.ones(vb.shape[:-1] + (1,), BF16)
    zeros = jnp.zeros(vb.shape[:-1] + (HEAD_DIM - 1,), BF16)
    vb = jnp.concatenate([vb, ones, zeros], axis=-1)
    return pl.pallas_call(
        functools.partial(_attn_kernel, nkb=nkb),
        grid=(B, N_KV_HEADS, nqb),
        in_specs=[
            pl.BlockSpec((1, tq, GW), lambda b, h, i: (b, i, h)),
            pl.BlockSpec((1, 1, nkb, HEAD_DIM, tk), lambda b, h, i: (b, h, 0, 0, 0)),
            pl.BlockSpec((1, 1, nkb, tk, 2 * HEAD_DIM), lambda b, h, i: (b, h, 0, 0, 0)),
        ],
        out_specs=pl.BlockSpec((1, tq, GW), lambda b, h, i: (b, i, h)),
        out_shape=jax.ShapeDtypeStruct((B, Lq, ATT_W), BF16),
        scratch_shapes=[pltpu.VMEM((R, HEAD_DIM), BF16), pltpu.VMEM((R, 1), F32), pltpu.VMEM((R, 2 * HEAD_DIM), F32),
                        pltpu.VMEM((R, tk), F32), pltpu.VMEM((R, tk), F32)],
        compiler_params=_cparams(("parallel", "parallel", "arbitrary")),
        name="attention",
    )(q, kT, vb)


def _merge_kernel(x_ref, ya_ref, ys_ref, yh_ref, ga_ref, gs_ref, gh_ref, gt_ref, g2_ref, sc_ref, sh_ref,
                  wa_ref, ws_ref, wh_ref, wo_ref, xo_ref, h2_ref):
    def br(y_ref, g_ref, w_ref):
        gate = jax.nn.sigmoid(g_ref[...].astype(F32))
        return gate * jnp.dot(y_ref[...], w_ref[...], preferred_element_type=F32)

    m = br(ya_ref, ga_ref, wa_ref) + br(ys_ref, gs_ref, ws_ref) + br(yh_ref, gh_ref, wh_ref)
    o = jnp.dot(m.astype(BF16), wo_ref[...], preferred_element_type=F32)
    xn = x_ref[...] + gt_ref[0] * o
    xo_ref[...] = xn
    ms = jnp.mean(xn * xn, axis=-1, keepdims=True)
    y = xn * lax.rsqrt(ms + EPS) * g2_ref[...]
    h2_ref[...] = (y * (1.0 + sc_ref[0]) + sh_ref[0]).astype(BF16)


def _merge(x2d, ya, ys, yh, p, gt1, g2, sc2, sh2, wa, ws, wh, wo, rows_per_batch, tm):
    T, D = x2d.shape
    tpb = rows_per_batch // tm
    gblk = GT0 // D
    row = lambda i: (i, 0)
    mod = lambda i: (i // tpb, 0, 0)
    full = lambda i: (0, 0)
    return pl.pallas_call(
        _merge_kernel,
        grid=(T // tm,),
        in_specs=[
            pl.BlockSpec((tm, D), row),
            pl.BlockSpec((tm, ATT_W), row),
            pl.BlockSpec((tm, SC_W), row),
            pl.BlockSpec((tm, HY_W), row),
            pl.BlockSpec((tm, D), lambda i: (i, gblk)),
            pl.BlockSpec((tm, D), lambda i: (i, gblk + 1)),
            pl.BlockSpec((tm, D), lambda i: (i, gblk + 2)),
            pl.BlockSpec((1, 1, D), mod),
            pl.BlockSpec((1, D), full),
            pl.BlockSpec((1, 1, D), mod),
            pl.BlockSpec((1, 1, D), mod),
            pl.BlockSpec((ATT_W, D), full),
            pl.BlockSpec((SC_W, D), full),
            pl.BlockSpec((HY_W, D), full),
            pl.BlockSpec((D, D), full),
        ],
        out_specs=[pl.BlockSpec((tm, D), row), pl.BlockSpec((tm, D), row)],
        out_shape=[jax.ShapeDtypeStruct((T, D), F32), jax.ShapeDtypeStruct((T, D), BF16)],
        compiler_params=_cparams(("parallel",)),
        name="merge",
    )(x2d, ya, ys, yh, p, p, p, gt1, g2, sc2, sh2, wa, ws, wh, wo)


LANES = 128
PACKED_ROWS = 16


def _peer_kernel(hT_ref, r2_ref, e2_ref, n_ref, e1_ref, u_ref, vT_ref, o_ref, gwa_scr, gwb_scr, act_scr, *, nb):
    e = pl.program_id(1)
    tc = hT_ref.shape[1]

    @pl.when(e == 0)
    def _():
        o_ref[...] = jnp.zeros(o_ref.shape, F32)
        gwb_scr[...] = jnp.zeros(gwb_scr.shape, BF16)

    def step(cur_scr, prev_scr):
        hT = hT_ref[...]
        for ii in range(nb):
            rows = slice(ii * PEER_NKEYS, (ii + 1) * PEER_NKEYS)
            a = jnp.dot(u_ref[rows, :], hT, preferred_element_type=F32)
            act_scr[rows, :] = (0.5 * a * (1.0 + lax.erf(a * (2.0 ** -0.5)))).astype(BF16)
        shape3 = (PEER_NKEYS // PACKED_ROWS, PACKED_ROWS, LANES)
        for lt in range(tc // LANES):
            lanes = slice(lt * LANES, (lt + 1) * LANES)
            ws = [jnp.zeros(shape3, BF16) for _ in range(nb)]
            for h in range(PEER_HEADS):
                r2t = pltpu.bitcast(r2_ref[h, :, lanes], BF16).reshape(shape3)
                e2t = pltpu.bitcast(e2_ref[h, :, lanes], BF16).reshape(shape3)
                for ii in range(nb):
                    cnt = jnp.broadcast_to(n_ref[h, 0, ii:ii + 1, lanes], (PACKED_ROWS, LANES)).astype(BF16)
                    e1row = jnp.broadcast_to(e1_ref[h, 0, ii:ii + 1, lanes], (PACKED_ROWS, LANES)).astype(BF16)
                    ws[ii] = ws[ii] + jnp.where(r2t < cnt[None], e2t, jnp.zeros_like(e2t)) * e1row[None]
            for ii in range(nb):
                rows = slice(ii * PEER_NKEYS, (ii + 1) * PEER_NKEYS)
                cur_scr[rows, lanes] = act_scr[rows, lanes] * ws[ii].reshape(PEER_NKEYS, LANES)
        o_ref[...] += jnp.dot(vT_ref[...], prev_scr[...], preferred_element_type=F32)

    @pl.when(e % 2 == 0)
    def _():
        step(gwa_scr, gwb_scr)

    @pl.when(e % 2 == 1)
    def _():
        step(gwb_scr, gwa_scr)


def _peer_dense(hT, r2, e2, cnt, e1, u, vT, tc, nb):
    D, T = hT.shape
    N = u.shape[0]
    eb = nb * PEER_NKEYS
    ne = N // eb
    assert ne % 2 == 0, "the drain step must find the last block in the buffer the parity rule reads"
    cur = lambda e: jnp.minimum(e, ne - 1)
    return pl.pallas_call(
        functools.partial(_peer_kernel, nb=nb),
        grid=(T // tc, ne + 1),
        in_specs=[
            pl.BlockSpec((D, tc), lambda t, e: (0, t)),
            pl.BlockSpec((PEER_HEADS, PEER_NKEYS // 2, tc), lambda t, e: (0, 0, t)),
            pl.BlockSpec((PEER_HEADS, PEER_NKEYS // 2, tc), lambda t, e: (0, 0, t)),
            pl.BlockSpec((PEER_HEADS, 1, nb, tc), lambda t, e: (0, cur(e), 0, t)),
            pl.BlockSpec((PEER_HEADS, 1, nb, tc), lambda t, e: (0, cur(e), 0, t)),
            pl.BlockSpec((eb, D), lambda t, e: (cur(e), 0)),
            pl.BlockSpec((D, eb), lambda t, e: (0, jnp.maximum(e - 1, 0))),
        ],
        out_specs=pl.BlockSpec((D, tc), lambda t, e: (0, t)),
        out_shape=jax.ShapeDtypeStruct((D, T), F32),
        scratch_shapes=[pltpu.VMEM((eb, tc), BF16), pltpu.VMEM((eb, tc), BF16), pltpu.VMEM((eb, tc), BF16)],
        compiler_params=_cparams(("parallel", "arbitrary")),
        name="peer_dense",
    )(hT, r2, e2, cnt, e1, u, vT)


ROUTE_ROWS = 24
NEG_INF = float("-inf")


def _top_rows(s, n):
    tc = s.shape[1]
    rid = lax.broadcasted_iota(jnp.int32, (ROUTE_ROWS, tc), 0)
    packed = jnp.full((ROUTE_ROWS, tc), NEG_INF, F32)
    rank = jnp.full(s.shape, float(n), F32)
    for k in range(n):
        m = jnp.max(s, axis=0, keepdims=True)
        packed = jnp.where(rid == k, m, packed)
        hit = s == m
        rank = jnp.where(hit, float(k), rank)
        s = jnp.where(hit, NEG_INF, s)
    return packed, rank


def _row_penalty(tc, lo, hi):
    rid = lax.broadcasted_iota(jnp.int32, (8, tc), 0)
    return jnp.where((rid >= lo) & (rid < hi), 0.0, NEG_INF).astype(F32)


def _route_kernel(hT_ref, wqh_ref, wql_ref, sk_ref, n_ref, e1_ref, r2_ref, e2_ref, q_scr):
    hT = hT_ref[...]
    q_scr[...] = (jnp.dot(wqh_ref[...], hT, preferred_element_type=F32)
                  + jnp.dot(wql_ref[...], hT, preferred_element_type=F32))
    tc = hT.shape[1]
    n = PEER_TOPK + 1
    half = PEER_DK // 2

    def head(h, carry):
        r0 = pl.multiple_of(h * PEER_DK, PEER_DK)
        hp = lax.Precision.HIGHEST
        s1 = jnp.dot(sk_ref[h, 0], q_scr[pl.ds(r0, half), :], preferred_element_type=F32, precision=hp)
        s2 = jnp.dot(sk_ref[h, 1], q_scr[pl.ds(r0 + half, half), :], preferred_element_type=F32, precision=hp)
        v1, _ = _top_rows(s1, n)
        v2, rank = _top_rows(s2, n)
        tiles = [v1[0:1] + v2[0:8], v1[0:1] + v2[8:16], v1[0:1] + v2[16:24], v1[1:2] + v2[0:8],
                 v1[2:3] + v2[0:8] + _row_penalty(tc, 0, n // 3), v1[3:4] + v2[0:8] + _row_penalty(tc, 0, n // 4),
                 v2[0:1] + v1[0:8] + _row_penalty(tc, 4, 8), v2[0:1] + v1[8:16], v2[0:1] + v1[16:24],
                 v2[1:2] + v1[0:8] + _row_penalty(tc, 4, n // 2), v2[2:3] + v1[0:8] + _row_penalty(tc, 4, n // 3)]
        top, _ = _top_rows(jnp.concatenate(tiles, axis=0), n)
        theta = 0.5 * (top[PEER_TOPK - 1:PEER_TOPK] + top[PEER_TOPK:PEER_TOPK + 1])
        z = jnp.sum(jnp.exp(top[0:16] - top[0:1]), axis=0, keepdims=True)
        c = theta - s1
        cnt = jnp.zeros(c.shape, F32)
        for b in range(n):
            cnt = cnt + jnp.where(v2[b:b + 1] >= c, 1.0, 0.0)
        n_ref[h] = cnt
        e1_ref[h] = jnp.exp(s1 - v1[0:1]) / z
        r2_ref[h] = pltpu.bitcast(rank.astype(BF16), jnp.uint32)
        e2_ref[h] = pltpu.bitcast(jnp.exp(s2 - v2[0:1]).astype(BF16), jnp.uint32)
        return carry

    lax.fori_loop(0, PEER_HEADS, head, 0)


def _peer_route(hT, wqT_hi, wqT_lo, subkeys, tc):
    D, T = hT.shape
    W = wqT_hi.shape[0]
    out = jax.ShapeDtypeStruct((PEER_HEADS, PEER_NKEYS, T), F32)
    out16 = jax.ShapeDtypeStruct((PEER_HEADS, PEER_NKEYS // 2, T), jnp.uint32)
    ospec = pl.BlockSpec((PEER_HEADS, PEER_NKEYS, tc), lambda t: (0, 0, t))
    ospec16 = pl.BlockSpec((PEER_HEADS, PEER_NKEYS // 2, tc), lambda t: (0, 0, t))
    return pl.pallas_call(
        _route_kernel,
        grid=(T // tc,),
        in_specs=[
            pl.BlockSpec((D, tc), lambda t: (0, t)),
            pl.BlockSpec((W, D), lambda t: (0, 0)),
            pl.BlockSpec((W, D), lambda t: (0, 0)),
            pl.BlockSpec(subkeys.shape, lambda t: (0, 0, 0, 0)),
        ],
        out_specs=[ospec, ospec, ospec16, ospec16],
        out_shape=[out, out, out16, out16],
        scratch_shapes=[pltpu.VMEM((W, tc), F32)],
        compiler_params=_cparams(("parallel",)),
        name="peer_route",
    )(hT, wqT_hi, wqT_lo, subkeys)


def _rope_tables(L, rope):
    if not rope:
        return jnp.ones((L, 2 * HEAD_DIM), F32), jnp.zeros((L, 2 * HEAD_DIM), F32)
    pos = jnp.arange(L, dtype=jnp.int32)
    row = (pos // GRID_W).astype(F32)
    col = (pos % GRID_W).astype(F32)
    inv = jnp.power(ROPE_THETA, -jnp.arange(0, AXIS_DIM, 2, dtype=F32) / AXIS_DIM)
    ar = row[:, None] * inv[None, :]
    ac = col[:, None] * inv[None, :]
    cos = jnp.concatenate([jnp.cos(ar), jnp.cos(ar), jnp.cos(ac), jnp.cos(ac)], axis=-1)
    sin = jnp.concatenate([-jnp.sin(ar), jnp.sin(ar), -jnp.sin(ac), jnp.sin(ac)], axis=-1)
    return jnp.tile(cos, (1, 2)), jnp.tile(sin, (1, 2))


def _qk_prep_kernel(p_ref, cos_ref, sin_ref, qg_ref, kg_ref, bdq_ref, bdk_ref, q_ref, k_ref):
    x = p_ref[...].astype(F32)
    cos = cos_ref[...]
    sin = sin_ref[...]
    quarter = AXIS_DIM // 2

    def prep(xh, gain, bd):
        w = xh.shape[1]
        sq = xh * xh
        hi = sq.astype(BF16)
        lo = (sq - hi.astype(F32)).astype(BF16)
        ms = jnp.dot(hi, bd, preferred_element_type=F32) + jnp.dot(lo, bd, preferred_element_type=F32)
        y = xh * lax.rsqrt(ms + EPS) * gain
        lane = lax.broadcasted_iota(jnp.int32, y.shape, 1)
        first = (lane & (AXIS_DIM - 1)) < quarter
        partner = jnp.where(first, pltpu.roll(y, w - quarter, 1), pltpu.roll(y, quarter, 1))
        reps = w // cos.shape[1]
        c = jnp.concatenate([cos] * reps, axis=1) if reps > 1 else cos
        s = jnp.concatenate([sin] * reps, axis=1) if reps > 1 else sin
        return y * c + partner * s

    q_ref[...] = (prep(x[:, :ATT_W], qg_ref[...], bdq_ref[...]) * ATT_SCALE).astype(BF16)
    k_ref[...] = prep(x[:, ATT_W:ATT_W + KV_W], kg_ref[...], bdk_ref[...]).astype(BF16)


def _qk_prep(p, cos, sin, q_gain, k_gain, rows_per_batch, tm):
    T = p.shape[0]
    tpb = rows_per_batch // tm
    qg = jnp.tile(q_gain, N_HEADS)[None, :]
    kg = jnp.tile(k_gain, N_KV_HEADS)[None, :]

    def block_avg(w):
        hid = jnp.arange(w) // HEAD_DIM
        return jnp.where(hid[:, None] == hid[None, :], 1.0 / HEAD_DIM, 0.0).astype(BF16)

    full = lambda i: (0, 0)
    return pl.pallas_call(
        _qk_prep_kernel,
        grid=(T // tm,),
        in_specs=[
            pl.BlockSpec((tm, QKV_W), lambda i: (i, Q0 // QKV_W)),
            pl.BlockSpec((tm, 2 * HEAD_DIM), lambda i: (i % tpb, 0)),
            pl.BlockSpec((tm, 2 * HEAD_DIM), lambda i: (i % tpb, 0)),
            pl.BlockSpec((1, ATT_W), full),
            pl.BlockSpec((1, KV_W), full),
            pl.BlockSpec((ATT_W, ATT_W), full),
            pl.BlockSpec((KV_W, KV_W), full),
        ],
        out_specs=[pl.BlockSpec((tm, ATT_W), lambda i: (i, 0)), pl.BlockSpec((tm, KV_W), lambda i: (i, 0))],
        out_shape=[jax.ShapeDtypeStruct((T, ATT_W), BF16), jax.ShapeDtypeStruct((T, KV_W), BF16)],
        compiler_params=_cparams(("parallel",)),
        name="qk_prep",
    )(p, cos, sin, qg, kg, block_avg(ATT_W), block_avg(KV_W))


def _shift_rows(cur, prev8, next8, first, last):
    tm = cur.shape[0]
    rid = lax.broadcasted_iota(jnp.int32, cur.shape, 0)
    pr = jnp.where(first, 0.0, prev8[HALO - 1:HALO, :])
    nx = jnp.where(last, 0.0, next8[0:1, :])
    up = jnp.where(rid == 0, pr, pltpu.roll(cur, 1, 0))
    dn = jnp.where(rid == tm - 1, nx, pltpu.roll(cur, tm - 1, 0))
    return up, dn


def _halo_specs(tm, width, col_block, n_rows):
    r = tm // HALO
    last = n_rows // HALO - 1
    prev = pl.BlockSpec((1, HALO, width), lambda b, i: (b, jnp.maximum(i * r - 1, 0), col_block))
    nxt = pl.BlockSpec((1, HALO, width), lambda b, i: (b, jnp.minimum((i + 1) * r, last), col_block))
    return prev, nxt


def _short_conv_kernel(bg_ref, cg_ref, xs_ref, cgp_ref, xsp_ref, cgn_ref, xsn_ref, w_ref, o_ref):
    i = pl.program_id(1)
    f = lambda r: r[0].astype(F32)
    cur = f(cg_ref) * f(xs_ref)
    up, dn = _shift_rows(cur, f(cgp_ref) * f(xsp_ref), f(cgn_ref) * f(xsn_ref), i == 0, i == pl.num_programs(1) - 1)
    o_ref[0] = (f(bg_ref) * (up * w_ref[0:1] + cur * w_ref[1:2] + dn * w_ref[2:3])).astype(o_ref.dtype)


def _short_conv(p3, w, tm):
    B, L, _ = p3.shape
    c0 = SC0 // SC_W
    blk = lambda j: pl.BlockSpec((1, tm, SC_W), lambda b, i: (b, i, c0 + j))
    cgp, cgn = _halo_specs(tm, SC_W, c0 + 1, L)
    xsp, xsn = _halo_specs(tm, SC_W, c0 + 2, L)
    return pl.pallas_call(
        _short_conv_kernel,
        grid=(B, L // tm),
        in_specs=[blk(0), blk(1), blk(2), cgp, xsp, cgn, xsn, pl.BlockSpec((3, SC_W), lambda b, i: (0, 0))],
        out_specs=pl.BlockSpec((1, tm, SC_W), lambda b, i: (b, i, 0)),
        out_shape=jax.ShapeDtypeStruct((B, L, SC_W), BF16),
        compiler_params=_cparams(("parallel", "parallel")),
        name="short_conv",
    )(p3, p3, p3, p3, p3, p3, p3, w)


def _hy_pre_kernel(p_ref, pp_ref, pn_ref, w_ref, b_ref, v_ref, x1_ref, x2_ref):
    i = pl.program_id(1)
    cur = p_ref[0].astype(F32)
    up, dn = _shift_rows(cur, pp_ref[0].astype(F32), pn_ref[0].astype(F32), i == 0, i == pl.num_programs(1) - 1)
    u = up * w_ref[0:1] + cur * w_ref[1:2] + dn * w_ref[2:3] + b_ref[...]
    v_ref[0] = u[:, :HY_W].astype(BF16)
    x1_ref[0] = u[:, HY_W:2 * HY_W].astype(BF16)
    x2_ref[0] = u[:, 2 * HY_W:].astype(BF16)


def _hy_pre(p3, w, b, tm):
    B, L, _ = p3.shape
    W = (HY_ORDER + 1) * HY_W
    c0 = HY0 // W
    prev, nxt = _halo_specs(tm, W, c0, L)
    out = jax.ShapeDtypeStruct((B, L, HY_W), BF16)
    ospec = pl.BlockSpec((1, tm, HY_W), lambda b_, i: (b_, i, 0))
    return pl.pallas_call(
        _hy_pre_kernel,
        grid=(B, L // tm),
        in_specs=[pl.BlockSpec((1, tm, W), lambda b_, i: (b_, i, c0)), prev, nxt,
                  pl.BlockSpec((3, W), lambda b_, i: (0, 0)), pl.BlockSpec((1, W), lambda b_, i: (0, 0))],
        out_specs=[ospec, ospec, ospec],
        out_shape=[out, out, out],
        compiler_params=_cparams(("parallel", "parallel")),
        name="hyena_pre",
    )(p3, p3, p3, w, b[None, :])


def _hy_features(L):
    n = jnp.arange(2 * L, dtype=jnp.int32)
    j = jnp.where(n < L, n, jnp.where(n == L, 0, 2 * L - n)).astype(F32)[:, None]
    t = j / (L - 1)
    w = 2.0 * math.pi * j / L
    bands = jnp.linspace(1e-4, HY_BANDS - 1, HY_BANDS, dtype=F32)[None, :]
    z = jnp.concatenate([t, jnp.cos(bands * w), -jnp.sin(bands * w)], axis=-1)
    return jnp.pad(z, ((0, 0), (0, HY_FEAT - z.shape[1])))


def _hy_filter_kernel(z_ref, w1_ref, b1_ref, f1_ref, w2_ref, b2_ref, f2_ref, w3_ref, dec_ref,
                      k0_ref, k1_ref, sum_ref):
    i = pl.program_id(0)
    half = pl.num_programs(0) // 2

    @pl.when(i == 0)
    def _():
        sum_ref[...] = jnp.zeros(sum_ref.shape, F32)

    hp = lax.Precision.HIGHEST
    z = z_ref[...]
    h = jnp.sin(f1_ref[...] * (jnp.dot(z, w1_ref[...], preferred_element_type=F32, precision=hp) + b1_ref[...]))
    h = jnp.sin(f2_ref[...] * (jnp.dot(h, w2_ref[...], preferred_element_type=F32, precision=hp) + b2_ref[...]))
    k = jnp.dot(h, w3_ref[0], preferred_element_type=F32, precision=hp)
    k = k * jnp.exp(-z[:, 0:1] * dec_ref[...])
    sum_ref[...] += jnp.sum(jnp.abs(k), axis=0, keepdims=True)
    rid = lax.broadcasted_iota(jnp.int32, k.shape, 0)
    k = jnp.where(i == half, jnp.where(rid == 0, 0.0, k), k)
    k0_ref[...] = k[:, :HY_W].astype(BF16)
    k1_ref[...] = k[:, HY_W:].astype(BF16)


def _hy_filter(L, w1, b1, f1, w2, b2, f2, w3, decay, nb):
    N = 2 * L
    fh = w1.shape[1]
    z = _hy_features(L)
    w1p = jnp.pad(w1, ((0, HY_FEAT - w1.shape[0]), (0, 0)))
    w3d = w3.reshape(fh, HY_ORDER, 2, HY_W).transpose(2, 0, 1, 3).reshape(2, fh, HY_ORDER * HY_W)
    dec = jnp.abs(decay).reshape(1, HY_ORDER * HY_W)
    row = lambda a: a[None, :]
    full = lambda i: (0, 0)
    half = N // nb // 2
    k0, k1, tot = pl.pallas_call(
        _hy_filter_kernel,
        grid=(N // nb,),
        in_specs=[
            pl.BlockSpec((nb, HY_FEAT), lambda i: (i, 0)),
            pl.BlockSpec((HY_FEAT, fh), full), pl.BlockSpec((1, fh), full), pl.BlockSpec((1, fh), full),
            pl.BlockSpec((fh, fh), full), pl.BlockSpec((1, fh), full), pl.BlockSpec((1, fh), full),
            pl.BlockSpec((1, fh, HY_ORDER * HY_W), lambda i: (i // half, 0, 0)),
            pl.BlockSpec((1, HY_ORDER * HY_W), full),
        ],
        out_specs=[pl.BlockSpec((nb, HY_W), lambda i: (i, 0)), pl.BlockSpec((nb, HY_W), lambda i: (i, 0)),
                   pl.BlockSpec((1, HY_ORDER * HY_W), full)],
        out_shape=[jax.ShapeDtypeStruct((N, HY_W), BF16), jax.ShapeDtypeStruct((N, HY_W), BF16),
                   jax.ShapeDtypeStruct((1, HY_ORDER * HY_W), F32)],
        compiler_params=_cparams(("arbitrary",)),
        name="hyena_filter",
    )(z, w1p, row(b1), row(f1), w2, row(b2), row(f2), w3d, dec)
    return k0, k1, 1.0 / tot


def _dft_tables(N1, N2):
    N = N1 * N2
    ar = lambda n: jnp.arange(n, dtype=jnp.int32)

    def cs(m, period):
        ang = (-2.0 * math.pi / period) * (m % period).astype(F32)
        return jnp.cos(ang), jnp.sin(ang)

    fr, fi = cs(ar(N1)[:, None] * ar(N1)[None, :], N1)
    hr, hi = fr[:, :max(N1 // 2, 1)], fi[:, :max(N1 // 2, 1)]
    w_fwd = jnp.block([[hr, -hi], [hi, hr]])
    w_real = jnp.concatenate([fr, fi], axis=0)
    w_inv = jnp.block([[hr.T, hi.T], [-hi.T, hr.T]]) / N1
    f2r, f2i = cs(ar(N2)[:, None] * ar(N2)[None, :], N2)
    tr, ti = cs(ar(N1)[:, None] * ar(N2)[None, :], N)
    gr = f2r[None] * tr[:, None, :] - f2i[None] * ti[:, None, :]
    gi = f2r[None] * ti[:, None, :] + f2i[None] * tr[:, None, :]
    g = jnp.concatenate([jnp.concatenate([gr, -gi], axis=2), jnp.concatenate([gi, gr], axis=2)], axis=1)
    grt, git = gr.transpose(0, 2, 1), gi.transpose(0, 2, 1)
    gh = jnp.concatenate([jnp.concatenate([grt, git], axis=2), jnp.concatenate([-git, grt], axis=2)], axis=1) / N2
    return tuple(a.astype(BF16) for a in (w_fwd, w_real, w_inv, g, gh))


def _colmm_kernel(w_ref, x_ref, o_ref):
    o_ref[...] = jnp.dot(w_ref[...], x_ref[...], preferred_element_type=F32).astype(o_ref.dtype)


def _colmm(w, x, cb):
    M, K = w.shape
    C = x.shape[1]
    return pl.pallas_call(
        _colmm_kernel,
        grid=(C // cb,),
        in_specs=[pl.BlockSpec((M, K), lambda j: (0, 0)), pl.BlockSpec((K, cb), lambda j: (0, j))],
        out_specs=pl.BlockSpec((M, cb), lambda j: (0, j)),
        out_shape=jax.ShapeDtypeStruct((M, C), BF16),
        compiler_params=_cparams(("parallel",)),
        name="dft_outer",
    )(w, x)


def _colmm_gate_kernel(w_ref, d_ref, u_ref, g_ref, b_ref, o_ref):
    y = jnp.dot(w_ref[...], d_ref[...], preferred_element_type=F32)
    o_ref[...] = (g_ref[...].astype(F32) * (y + u_ref[...].astype(F32) * b_ref[...])).astype(o_ref.dtype)


def _colmm_gate(w, d, u, gate, bias_row, cb):
    M, K = w.shape
    C = d.shape[1]
    col = lambda j: (0, j)
    return pl.pallas_call(
        _colmm_gate_kernel,
        grid=(C // cb,),
        in_specs=[pl.BlockSpec((M, K), lambda j: (0, 0)), pl.BlockSpec((K, cb), col), pl.BlockSpec((M, cb), col),
                  pl.BlockSpec((M, cb), col), pl.BlockSpec((1, cb), lambda j: (0, 0))],
        out_specs=pl.BlockSpec((M, cb), col),
        out_shape=jax.ShapeDtypeStruct((M, C), BF16),
        compiler_params=_cparams(("parallel",)),
        name="dft_outer_gate",
    )(w, d, u, gate, bias_row)


def _mid_fwd_kernel(a_ref, g_ref, s_ref, k_ref):
    n2 = a_ref.shape[2]
    a = a_ref[:, 0].reshape(2 * n2, a_ref.shape[3])
    x = jnp.dot(g_ref[0], a, preferred_element_type=F32) * s_ref[...]
    k_ref[:, 0] = x.reshape(2, n2, x.shape[1])


def _mid_fwd(a, g, scale):
    _, N1, N2, C = a.shape
    blk = pl.BlockSpec((2, 1, N2, C), lambda i: (0, i, 0, 0))
    return pl.pallas_call(
        _mid_fwd_kernel,
        grid=(N1,),
        in_specs=[blk, pl.BlockSpec((1, 2 * N2, 2 * N2), lambda i: (i, 0, 0)), pl.BlockSpec((1, C), lambda i: (0, 0))],
        out_specs=blk,
        out_shape=jax.ShapeDtypeStruct((2, N1, N2, C), F32),
        compiler_params=_cparams(("parallel",)),
        name="dft_inner_filter",
    )(a, g, scale)


def _mid_kernel(a_ref, g_ref, gh_ref, k_ref, d_ref):
    n2 = a_ref.shape[2]
    c = a_ref.shape[3]
    a = a_ref[:, 0].reshape(2 * n2, c)
    x = jnp.dot(g_ref[0], a, preferred_element_type=F32)
    xr, xi = x[:n2], x[n2:]
    kr, ki = k_ref[0, 0], k_ref[1, 0]
    y = jnp.concatenate([xr * kr - xi * ki, xr * ki + xi * kr], axis=0).astype(BF16)
    d = jnp.dot(gh_ref[0], y, preferred_element_type=F32)
    d_ref[:, 0] = d.reshape(2, n2, c).astype(d_ref.dtype)


def _mid(a, g, gh, kf):
    _, N1, N2, C = a.shape
    blk = pl.BlockSpec((2, 1, N2, C), lambda i: (0, i, 0, 0))
    tab = pl.BlockSpec((1, 2 * N2, 2 * N2), lambda i: (i, 0, 0))
    return pl.pallas_call(
        _mid_kernel,
        grid=(N1,),
        in_specs=[blk, tab, tab, blk],
        out_specs=blk,
        out_shape=jax.ShapeDtypeStruct((2, N1, N2, C), BF16),
        compiler_params=_cparams(("parallel",)),
        name="dft_inner",
    )(a, g, gh, kf)


def _hyena_long(v, x1, x2, filt, bias, tabs, cb):
    B, L, C = v.shape
    assert B == 2, "the two batches are packed as real / imaginary parts of one complex signal"
    w_fwd, w_real, w_inv, g, gh = tabs
    N1 = g.shape[0]
    N2 = g.shape[1] // 2
    k0, k1, inv_norm = filt
    flat = lambda a: a.reshape(-1, N2 * C)
    z = flat(v)
    for o, (ker, gate) in enumerate(((k0, x1), (k1, x2))):
        if N1 > 1:
            ka = _colmm(w_real, flat(ker), cb).reshape(2, N1, N2, C)
            za = _colmm(w_fwd, z, cb).reshape(2, N1, N2, C)
        else:
            ka = jnp.stack([ker, jnp.zeros_like(ker)]).reshape(2, 1, N2, C)
            za = jnp.pad(z.reshape(2, L, C), ((0, 0), (0, L), (0, 0))).reshape(2, 1, N2, C)
        kf = _mid_fwd(ka, g, inv_norm[:, o * C:(o + 1) * C])
        d = _mid(za, g, gh, kf)
        if N1 > 1:
            z = _colmm_gate(w_inv, flat(d), z, flat(gate), jnp.tile(bias[o], cb // C)[None, :], cb)
        else:
            y = d.reshape(2, N2, C)[:, :L].astype(F32)
            zf = z.reshape(2, L, C).astype(F32)
            z = (gate.astype(F32) * (y + zf * bias[o])).astype(BF16).reshape(-1, N2 * C // 2)
    return z.reshape(B, L, C)


def _resid_kernel(x_ref, fT_ref, gt_ref, g_ref, o_ref, *, final):
    xn = x_ref[...] + gt_ref[0] * fT_ref[...].T
    if final:
        xn = xn * lax.rsqrt(jnp.mean(xn * xn, axis=-1, keepdims=True) + EPS) * g_ref[...]
    o_ref[...] = xn


def _resid(x2d, fT, col0, gt, g, rows_per_batch, tm, final):
    T, D = x2d.shape
    tpb = rows_per_batch // tm
    c0 = col0 // tm
    row = lambda i: (i, 0)
    return pl.pallas_call(
        functools.partial(_resid_kernel, final=final),
        grid=(T // tm,),
        in_specs=[pl.BlockSpec((tm, D), row), pl.BlockSpec((D, tm), lambda i: (0, c0 + i)),
                  pl.BlockSpec((1, 1, D), lambda i: (i // tpb, 0, 0)), pl.BlockSpec((1, D), lambda i: (0, 0))],
        out_specs=pl.BlockSpec((tm, D), row),
        out_shape=jax.ShapeDtypeStruct((T, D), F32),
        compiler_params=_cparams(("parallel",)),
        name="residual",
    )(x2d, fT, gt, g)


def _peer(h2, wq, subkeys, u_bf, vT_bf, tc, nb):
    T, D = h2.shape
    hT = h2.T
    wqT = wq.T
    wqT_hi = wqT.astype(BF16)
    wqT_lo = (wqT - wqT_hi.astype(F32)).astype(BF16)
    cnt, e1, r2, e2 = _peer_route(hT, wqT_hi, wqT_lo, subkeys, tc)
    cnt4 = cnt.reshape(PEER_HEADS, PEER_NKEYS // nb, nb, T)
    e14 = e1.reshape(PEER_HEADS, PEER_NKEYS // nb, nb, T)
    return _peer_dense(hT, r2, e2, cnt4, e14, u_bf, vT_bf, tc, nb)


def kernel(x, c, ctx, c_ctx, w_mod, b_mod, g_norm1, g_norm2, w_in, q_gain, k_gain, sc_conv_w, hy_conv_w,
           hy_conv_b, hy_w1, hy_b1, hy_f1, hy_w2, hy_b2, hy_f2, hy_w3, hy_decay, hy_bias, w_br_att, w_br_sc,
           w_br_hy, w_out, peer_wq, peer_subkeys, peer_u, peer_v, g_final):
    B, S, D = x.shape
    Lc = ctx.shape[1]
    hp = lax.Precision.HIGHEST
    cond = jnp.concatenate([jax.nn.silu(c), jnp.broadcast_to(jax.nn.silu(c_ctx), (B, D))], axis=0)
    cos, sin = _rope_tables(S, True)
    cos_c, sin_c = _rope_tables(Lc, False)
    tm = min(1024, S)
    tq = min(128, S)
    tr = min(512, S)
    cb = 16 * HY_W
    tabs = _dft_tables(2 * S // HY_N2, HY_N2)
    tabs_c = _dft_tables(1, 2 * Lc)
    x2 = x.reshape(B * S, D)
    ctx2 = ctx.reshape(B * Lc, D)

    for l in range(DEPTH):
        need_ctx = l < DEPTH - 1
        mod = (jnp.dot(cond, w_mod[l], precision=hp) + b_mod[l]).reshape(2, B, 1, 6, D)
        sh1, sc1, gt1, sh2, sc2, gt2 = (mod[0, :, :, i] for i in range(6))
        csh1, csc1, cgt1, csh2, csc2, cgt2 = (mod[1, :, :, i] for i in range(6))
        w_in_bf = jnp.concatenate([w_in[l][:, REF_GT0:], w_in[l][:, REF_SC0:REF_GT0], w_in[l][:, :REF_SC0]],
                                  axis=1).astype(BF16)
        wa, ws, wh, wo = (w.astype(BF16) for w in (w_br_att[l], w_br_sc[l], w_br_hy[l], w_out[l]))
        g1 = g_norm1[l][None, :]
        g2 = g_norm2[l][None, :]
        hy_params = (hy_w1[l], hy_b1[l], hy_f1[l], hy_w2[l], hy_b2[l], hy_f2[l], hy_w3[l], hy_decay[l])

        p = _in_proj(x2, g1, sc1, sh1, w_in_bf, S, tm, 768)
        p3 = p.reshape(B, S, -1)
        pc = _in_proj(ctx2, g1, csc1, csh1, w_in_bf, Lc, Lc, 768)
        pc3 = pc.reshape(B, Lc, -1)

        qs, k = _qk_prep(p, cos, sin, q_gain[l], k_gain[l], S, tr)
        qcs, kc = _qk_prep(pc, cos_c, sin_c, q_gain[l], k_gain[l], Lc, Lc)
        k_all = jnp.concatenate([kc.reshape(B, Lc, KV_W), k.reshape(B, S, KV_W)], axis=1)
        v_all = jnp.concatenate([pc3[..., V0:IN_W], p3[..., V0:IN_W]], axis=1)
        tk = (S + Lc) // 13 if (S + Lc) % (13 * 128) == 0 else 128
        y_att = _attention(qs.reshape(B, S, ATT_W), k_all, v_all, tq, tk).reshape(B * S, ATT_W)

        y_sc = _short_conv(p3, sc_conv_w[l], tr).reshape(B * S, SC_W)
        y_hy = _hyena_long(*_hy_pre(p3, hy_conv_w[l], hy_conv_b[l], tr), _hy_filter(S, *hy_params, tr),
                           hy_bias[l], tabs, cb).reshape(B * S, HY_W)
        x2, h2 = _merge(x2, y_att, y_sc, y_hy, p, gt1, g2, sc2, sh2, wa, ws, wh, wo, S, tr)

        if need_ctx:
            yc_att = _attention(qcs.reshape(B, Lc, ATT_W), kc.reshape(B, Lc, KV_W), pc3[..., V0:IN_W], Lc, Lc)
            yc_sc = _short_conv(pc3, sc_conv_w[l], Lc).reshape(B * Lc, SC_W)
            yc_hy = _hyena_long(*_hy_pre(pc3, hy_conv_w[l], hy_conv_b[l], Lc), _hy_filter(Lc, *hy_params, Lc),
                                hy_bias[l], tabs_c, cb).reshape(B * Lc, HY_W)
            ctx2, h2c = _merge(ctx2, yc_att.reshape(B * Lc, ATT_W), yc_sc, yc_hy, pc, cgt1, g2, csc2, csh2,
                               wa, ws, wh, wo, Lc, Lc)
            tok = jnp.concatenate([h2, h2c], axis=0)
        else:
            tok = h2

        u_bf = peer_u[l].astype(BF16)
        vT_bf = peer_v[l].astype(BF16).T
        fT = _peer(tok, peer_wq[l], peer_subkeys[l], u_bf, vT_bf, min(512, tok.shape[0]), 4)
        last = l == DEPTH - 1
        x2 = _resid(x2, fT, 0, gt2, g_final[None, :], S, tr, last)
        if need_ctx:
            ctx2 = _resid(ctx2, fT, B * S, cgt2, g_final[None, :], Lc, Lc, False)

    return x2.reshape(B, S, D)
```

```python
import functools
import math

import jax
import jax.numpy as jnp
from jax import lax
from jax.experimental import pallas as pl
from jax.experimental.pallas import tpu as pltpu

F32 = jnp.float32
BF16 = jnp.bfloat16

DEPTH = 2
GRID_W = 64
EPS = 1e-6
N_HEADS = 8
N_KV_HEADS = 2
GQA_GROUP = N_HEADS // N_KV_HEADS
HEAD_DIM = 64
AXIS_DIM = HEAD_DIM // 2
ATT_W = N_HEADS * HEAD_DIM
KV_W = N_KV_HEADS * HEAD_DIM
ATT_SCALE = HEAD_DIM ** -0.5
ROPE_THETA = 10000.0
SC_W = 512
HY_W = 512
HY_ORDER = 2
HY_BANDS = 16
PEER_HEADS = 8
PEER_NKEYS = 128
PEER_DK = 128
PEER_TOPK = 16
D_MODEL = 1024
QKV_W = ATT_W + 2 * KV_W
REF_SC0 = QKV_W
REF_GT0 = QKV_W + 3 * SC_W + (HY_ORDER + 1) * HY_W
GT0 = 0
SC0 = GT0 + 3 * D_MODEL
HY0 = SC0 + 3 * SC_W
Q0 = HY0 + (HY_ORDER + 1) * HY_W
K0 = Q0 + ATT_W
V0 = K0 + KV_W
IN_W = V0 + KV_W
HALO = 8
HY_N2 = 256
HY_FEAT = 128

VMEM_LIMIT = 56 * 1024 * 1024


def _cparams(sem):
    return pltpu.CompilerParams(dimension_semantics=sem, vmem_limit_bytes=VMEM_LIMIT)


def _in_proj_kernel(x_ref, g_ref, sc_ref, sh_ref, w_ref, o_ref, h_scr):
    @pl.when(pl.program_id(1) == 0)
    def _():
        x = x_ref[...]
        ms = jnp.mean(x * x, axis=-1, keepdims=True)
        y = x * lax.rsqrt(ms + EPS) * g_ref[...]
        h_scr[...] = (y * (1.0 + sc_ref[0]) + sh_ref[0]).astype(BF16)

    o_ref[...] = jnp.dot(h_scr[...], w_ref[...], preferred_element_type=F32).astype(o_ref.dtype)


def _in_proj(x2d, g, sc, sh, w, rows_per_batch, tm, tn):
    T, D = x2d.shape
    N = w.shape[1]
    tpb = rows_per_batch // tm
    return pl.pallas_call(
        _in_proj_kernel,
        grid=(T // tm, N // tn),
        in_specs=[
            pl.BlockSpec((tm, D), lambda i, j: (i, 0)),
            pl.BlockSpec((1, D), lambda i, j: (0, 0)),
            pl.BlockSpec((1, 1, D), lambda i, j: (i // tpb, 0, 0)),
            pl.BlockSpec((1, 1, D), lambda i, j: (i // tpb, 0, 0)),
            pl.BlockSpec((D, tn), lambda i, j: (0, j)),
        ],
        out_specs=pl.BlockSpec((tm, tn), lambda i, j: (i, j)),
        out_shape=jax.ShapeDtypeStruct((T, N), BF16),
        scratch_shapes=[pltpu.VMEM((tm, D), BF16)],
        compiler_params=_cparams(("parallel", "arbitrary")),
        name="in_proj",
    )(x2d, g, sc, sh, w)


def _attn_kernel(q_ref, kT_ref, v_ref, o_ref, q_scr, m_scr, acc_scr, sa_scr, sb_scr, *, nkb):
    tq = q_ref.shape[1]
    m_scr[...] = jnp.full(m_scr.shape, -jnp.inf, F32)
    acc_scr[...] = jnp.zeros(acc_scr.shape, F32)
    for g in range(GQA_GROUP):
        q_scr[g * tq:(g + 1) * tq, :] = q_ref[0, :, g * HEAD_DIM:(g + 1) * HEAD_DIM]

    def scores(j, dst_scr):
        dst_scr[...] = jnp.dot(q_scr[...], kT_ref[0, 0, j], preferred_element_type=F32)

    def update(j, src_scr):
        s = src_scr[...]
        m_prev = m_scr[...]
        m_new = jnp.maximum(m_prev, jnp.max(s, axis=-1, keepdims=True))
        p = jnp.exp(s - m_new).astype(BF16)
        alpha = jnp.exp(m_prev - m_new)
        acc_scr[...] = acc_scr[...] * alpha + jnp.dot(p, v_ref[0, 0, j], preferred_element_type=F32)
        m_scr[...] = m_new

    scores(0, sa_scr)

    def pair(i, carry):
        j = 2 * i
        scores(j + 1, sb_scr)
        update(j, sa_scr)
        scores(j + 2, sa_scr)
        update(j + 1, sb_scr)
        return carry

    lax.fori_loop(0, (nkb - 1) // 2, pair, 0)
    if nkb % 2 == 1:
        update(nkb - 1, sa_scr)
    else:
        scores(nkb - 1, sb_scr)
        update(nkb - 2, sa_scr)
        update(nkb - 1, sb_scr)
    acc = acc_scr[...]
    o = (acc[:, :HEAD_DIM] / acc[:, HEAD_DIM:HEAD_DIM + 1]).astype(o_ref.dtype)
    o_ref[0] = jnp.concatenate([o[g * tq:(g + 1) * tq] for g in range(GQA_GROUP)], axis=1)


def _attention(q, k, v, tq, tk):
    B, Lq, _ = q.shape
    Lk = k.shape[1]
    nqb, nkb = Lq // tq, Lk // tk
    R = GQA_GROUP * tq
    GW = GQA_GROUP * HEAD_DIM
    kT = k.reshape(B, nkb, tk, N_KV_HEADS, HEAD_DIM).transpose(0, 3, 1, 4, 2)
    vb = v.reshape(B, nkb, tk, N_KV_HEADS, HEAD_DIM).transpose(0, 3, 1, 2, 4)
    ones = jnp.ones(vb.shape[:-1] + (1,), BF16)
    zeros = jnp.zeros(vb.shape[:-1] + (HEAD_DIM - 1,), BF16)
    vb = jnp.concatenate([vb, ones, zeros], axis=-1)
    return pl.pallas_call(
        functools.partial(_attn_kernel, nkb=nkb),
        grid=(B, N_KV_HEADS, nqb),
        in_specs=[
            pl.BlockSpec((1, tq, GW), lambda b, h, i: (b, i, h)),
            pl.BlockSpec((1, 1, nkb, HEAD_DIM, tk), lambda b, h, i: (b, h, 0, 0, 0)),
            pl.BlockSpec((1, 1, nkb, tk, 2 * HEAD_DIM), lambda b, h, i: (b, h, 0, 0, 0)),
        ],
        out_specs=pl.BlockSpec((1, tq, GW), lambda b, h, i: (b, i, h)),
        out_shape=jax.ShapeDtypeStruct((B, Lq, ATT_W), BF16),
        scratch_shapes=[pltpu.VMEM((R, HEAD_DIM), BF16), pltpu.VMEM((R, 1), F32), pltpu.VMEM((R, 2 * HEAD_DIM), F32),
                        pltpu.VMEM((R, tk), F32), pltpu.VMEM((R, tk), F32)],
        compiler_params=_cparams(("parallel", "parallel", "arbitrary")),
        name="attention",
    )(q, kT, vb)


def _merge_kernel(x_ref, ya_ref, ys_ref, yh_ref, ga_ref, gs_ref, gh_ref, gt_ref, g2_ref, sc_ref, sh_ref,
                  wa_ref, ws_ref, wh_ref, wo_ref, xo_ref, h2_ref):
    def br(y_ref, g_ref, w_ref):
        gate = jax.nn.sigmoid(g_ref[...].astype(F32))
        return gate * jnp.dot(y_ref[...], w_ref[...], preferred_element_type=F32)

    m = br(ya_ref, ga_ref, wa_ref) + br(ys_ref, gs_ref, ws_ref) + br(yh_ref, gh_ref, wh_ref)
    o = jnp.dot(m.astype(BF16), wo_ref[...], preferred_element_type=F32)
    xn = x_ref[...] + gt_ref[0] * o
    xo_ref[...] = xn
    ms = jnp.mean(xn * xn, axis=-1, keepdims=True)
    y = xn * lax.rsqrt(ms + EPS) * g2_ref[...]
    h2_ref[...] = (y * (1.0 + sc_ref[0]) + sh_ref[0]).astype(BF16)


def _merge(x2d, ya, ys, yh, p, gt1, g2, sc2, sh2, wa, ws, wh, wo, rows_per_batch, tm):
    T, D = x2d.shape
    tpb = rows_per_batch // tm
    gblk = GT0 // D
    row = lambda i: (i, 0)
    mod = lambda i: (i // tpb, 0, 0)
    full = lambda i: (0, 0)
    return pl.pallas_call(
        _merge_kernel,
        grid=(T // tm,),
        in_specs=[
            pl.BlockSpec((tm, D), row),
            pl.BlockSpec((tm, ATT_W), row),
            pl.BlockSpec((tm, SC_W), row),
            pl.BlockSpec((tm, HY_W), row),
            pl.BlockSpec((tm, D), lambda i: (i, gblk)),
            pl.BlockSpec((tm, D), lambda i: (i, gblk + 1)),
            pl.BlockSpec((tm, D), lambda i: (i, gblk + 2)),
            pl.BlockSpec((1, 1, D), mod),
            pl.BlockSpec((1, D), full),
            pl.BlockSpec((1, 1, D), mod),
            pl.BlockSpec((1, 1, D), mod),
            pl.BlockSpec((ATT_W, D), full),
            pl.BlockSpec((SC_W, D), full),
            pl.BlockSpec((HY_W, D), full),
            pl.BlockSpec((D, D), full),
        ],
        out_specs=[pl.BlockSpec((tm, D), row), pl.BlockSpec((tm, D), row)],
        out_shape=[jax.ShapeDtypeStruct((T, D), F32), jax.ShapeDtypeStruct((T, D), BF16)],
        compiler_params=_cparams(("parallel",)),
        name="merge",
    )(x2d, ya, ys, yh, p, p, p, gt1, g2, sc2, sh2, wa, ws, wh, wo)


LANES = 128
PACKED_ROWS = 16
PEER_BLOCK = 4


def _peer_kernel(hT_ref, r2_ref, e2_ref, n_ref, e1_ref, u_ref, vT_ref, o_ref, gwa_scr, gwb_scr, act_scr, *, nb):
    e = pl.program_id(1)
    tc = hT_ref.shape[1]

    @pl.when(e == 0)
    def _():
        o_ref[...] = jnp.zeros(o_ref.shape, F32)
        gwb_scr[...] = jnp.zeros(gwb_scr.shape, BF16)

    def step(cur_scr, prev_scr):
        o_ref[...] += jnp.dot(vT_ref[0], prev_scr[...], preferred_element_type=F32)
        hT = hT_ref[...]
        blk = jnp.minimum(e, pl.num_programs(1) - 2)
        for ii in range(nb):
            rows = slice(ii * PEER_NKEYS, (ii + 1) * PEER_NKEYS)
            a = jnp.dot(u_ref[rows, :], hT, preferred_element_type=F32)
            act_scr[rows, :] = (0.5 * a * (1.0 + lax.erf(a * (2.0 ** -0.5)))).astype(BF16)
        shape3 = (PEER_NKEYS // PACKED_ROWS, PACKED_ROWS, LANES)
        for lt in range(tc // LANES):
            lanes = slice(lt * LANES, (lt + 1) * LANES)
            ws = [jnp.zeros(shape3, BF16) for _ in range(nb)]
            for h in range(PEER_HEADS):
                r2t = pltpu.bitcast(r2_ref[h, :, lanes], BF16).reshape(shape3)
                e2t = pltpu.bitcast(e2_ref[h, :, lanes], BF16).reshape(shape3)
                for ii in range(nb):
                    cnt = jnp.broadcast_to(n_ref[h, blk, ii:ii + 1, lanes], (PACKED_ROWS, LANES)).astype(BF16)
                    e1row = jnp.broadcast_to(e1_ref[h, blk, ii:ii + 1, lanes], (PACKED_ROWS, LANES)).astype(BF16)
                    ws[ii] = ws[ii] + jnp.where(r2t < cnt[None], e2t, jnp.zeros_like(e2t)) * e1row[None]
            for ii in range(nb):
                rows = slice(ii * PEER_NKEYS, (ii + 1) * PEER_NKEYS)
                cur_scr[rows, lanes] = act_scr[rows, lanes] * ws[ii].reshape(PEER_NKEYS, LANES)

    @pl.when(e % 2 == 0)
    def _():
        step(gwa_scr, gwb_scr)

    @pl.when(e % 2 == 1)
    def _():
        step(gwb_scr, gwa_scr)


def _peer_dense(hT, r2, e2, cnt, e1, u, vT, tc, nb):
    D, T = hT.shape
    N = u.shape[0]
    eb = nb * PEER_NKEYS
    ne = N // eb
    assert ne % 2 == 0, "the drain step must find the last block in the buffer the parity rule reads"
    cur = lambda e: jnp.minimum(e, ne - 1)
    return pl.pallas_call(
        functools.partial(_peer_kernel, nb=nb),
        grid=(T // tc, ne + 1),
        in_specs=[
            pl.BlockSpec((D, tc), lambda t, e: (0, t)),
            pl.BlockSpec((PEER_HEADS, PEER_NKEYS // 2, tc), lambda t, e: (0, 0, t)),
            pl.BlockSpec((PEER_HEADS, PEER_NKEYS // 2, tc), lambda t, e: (0, 0, t)),
            pl.BlockSpec((PEER_HEADS, ne, nb, tc), lambda t, e: (0, 0, 0, t)),
            pl.BlockSpec((PEER_HEADS, ne, nb, tc), lambda t, e: (0, 0, 0, t)),
            pl.BlockSpec((eb, D), lambda t, e: (cur(e), 0)),
            pl.BlockSpec((1, D, eb), lambda t, e: (jnp.maximum(e - 1, 0), 0, 0)),
        ],
        out_specs=pl.BlockSpec((D, tc), lambda t, e: (0, t)),
        out_shape=jax.ShapeDtypeStruct((D, T), F32),
        scratch_shapes=[pltpu.VMEM((eb, tc), BF16), pltpu.VMEM((eb, tc), BF16), pltpu.VMEM((eb, tc), BF16)],
        compiler_params=_cparams(("parallel", "arbitrary")),
        name="peer_dense",
    )(hT, r2, e2, cnt, e1, u, vT)


ROUTE_ROWS = 24
NEG_INF = float("-inf")


def _top_rows(s, n):
    tc = s.shape[1]
    rid = lax.broadcasted_iota(jnp.int32, (ROUTE_ROWS, tc), 0)
    packed = jnp.full((ROUTE_ROWS, tc), NEG_INF, F32)
    rank = jnp.full(s.shape, float(n), F32)
    for k in range(n):
        m = jnp.max(s, axis=0, keepdims=True)
        packed = jnp.where(rid == k, m, packed)
        hit = s == m
        rank = jnp.where(hit, float(k), rank)
        s = jnp.where(hit, NEG_INF, s)
    return packed, rank


def _row_penalty(tc, lo, hi):
    rid = lax.broadcasted_iota(jnp.int32, (8, tc), 0)
    return jnp.where((rid >= lo) & (rid < hi), 0.0, NEG_INF).astype(F32)


def _route_kernel(hT_ref, wqh_ref, wql_ref, sk_ref, n_ref, e1_ref, r2_ref, e2_ref, q_scr):
    hT = hT_ref[...]
    q_scr[...] = (jnp.dot(wqh_ref[...], hT, preferred_element_type=F32)
                  + jnp.dot(wql_ref[...], hT, preferred_element_type=F32))
    tc = hT.shape[1]
    n = PEER_TOPK + 1
    half = PEER_DK // 2

    def head(h, carry):
        r0 = pl.multiple_of(h * PEER_DK, PEER_DK)
        hp = lax.Precision.HIGHEST
        s1 = jnp.dot(sk_ref[h, 0], q_scr[pl.ds(r0, half), :], preferred_element_type=F32, precision=hp)
        s2 = jnp.dot(sk_ref[h, 1], q_scr[pl.ds(r0 + half, half), :], preferred_element_type=F32, precision=hp)
        v1, _ = _top_rows(s1, n)
        v2, rank = _top_rows(s2, n)
        tiles = [v1[0:1] + v2[0:8], v1[0:1] + v2[8:16], v1[0:1] + v2[16:24], v1[1:2] + v2[0:8],
                 v1[2:3] + v2[0:8] + _row_penalty(tc, 0, n // 3), v1[3:4] + v2[0:8] + _row_penalty(tc, 0, n // 4),
                 v2[0:1] + v1[0:8] + _row_penalty(tc, 4, 8), v2[0:1] + v1[8:16], v2[0:1] + v1[16:24],
                 v2[1:2] + v1[0:8] + _row_penalty(tc, 4, n // 2), v2[2:3] + v1[0:8] + _row_penalty(tc, 4, n // 3)]
        top, _ = _top_rows(jnp.concatenate(tiles, axis=0), n)
        theta = 0.5 * (top[PEER_TOPK - 1:PEER_TOPK] + top[PEER_TOPK:PEER_TOPK + 1])
        z = jnp.sum(jnp.exp(top[0:16] - top[0:1]), axis=0, keepdims=True)
        c = theta - s1
        cnt = jnp.zeros(c.shape, F32)
        for b in range(n):
            cnt = cnt + jnp.where(v2[b:b + 1] >= c, 1.0, 0.0)
        n_ref[h] = cnt
        e1_ref[h] = jnp.exp(s1 - v1[0:1]) / z
        r2_ref[h] = pltpu.bitcast(rank.astype(BF16), jnp.uint32)
        e2_ref[h] = pltpu.bitcast(jnp.exp(s2 - v2[0:1]).astype(BF16), jnp.uint32)
        return carry

    lax.fori_loop(0, PEER_HEADS, head, 0)


def _peer_route(hT, wqT_hi, wqT_lo, subkeys, tc):
    D, T = hT.shape
    W = wqT_hi.shape[0]
    out = jax.ShapeDtypeStruct((PEER_HEADS, PEER_NKEYS, T), F32)
    out16 = jax.ShapeDtypeStruct((PEER_HEADS, PEER_NKEYS // 2, T), jnp.uint32)
    ospec = pl.BlockSpec((PEER_HEADS, PEER_NKEYS, tc), lambda t: (0, 0, t))
    ospec16 = pl.BlockSpec((PEER_HEADS, PEER_NKEYS // 2, tc), lambda t: (0, 0, t))
    return pl.pallas_call(
        _route_kernel,
        grid=(T // tc,),
        in_specs=[
            pl.BlockSpec((D, tc), lambda t: (0, t)),
            pl.BlockSpec((W, D), lambda t: (0, 0)),
            pl.BlockSpec((W, D), lambda t: (0, 0)),
            pl.BlockSpec(subkeys.shape, lambda t: (0, 0, 0, 0)),
        ],
        out_specs=[ospec, ospec, ospec16, ospec16],
        out_shape=[out, out, out16, out16],
        scratch_shapes=[pltpu.VMEM((W, tc), F32)],
        compiler_params=_cparams(("parallel",)),
        name="peer_route",
    )(hT, wqT_hi, wqT_lo, subkeys)


def _rope_tables(L, rope):
    if not rope:
        return jnp.ones((L, 2 * HEAD_DIM), F32), jnp.zeros((L, 2 * HEAD_DIM), F32)
    pos = jnp.arange(L, dtype=jnp.int32)
    row = (pos // GRID_W).astype(F32)
    col = (pos % GRID_W).astype(F32)
    inv = jnp.power(ROPE_THETA, -jnp.arange(0, AXIS_DIM, 2, dtype=F32) / AXIS_DIM)
    ar = row[:, None] * inv[None, :]
    ac = col[:, None] * inv[None, :]
    cos = jnp.concatenate([jnp.cos(ar), jnp.cos(ar), jnp.cos(ac), jnp.cos(ac)], axis=-1)
    sin = jnp.concatenate([-jnp.sin(ar), jnp.sin(ar), -jnp.sin(ac), jnp.sin(ac)], axis=-1)
    return jnp.tile(cos, (1, 2)), jnp.tile(sin, (1, 2))


def _qk_prep_kernel(p_ref, cos_ref, sin_ref, qg_ref, kg_ref, bdq_ref, bdk_ref, q_ref, k_ref):
    x = p_ref[...].astype(F32)
    cos = cos_ref[...]
    sin = sin_ref[...]
    quarter = AXIS_DIM // 2

    def prep(xh, gain, bd):
        w = xh.shape[1]
        sq = xh * xh
        hi = sq.astype(BF16)
        lo = (sq - hi.astype(F32)).astype(BF16)
        ms = jnp.dot(hi, bd, preferred_element_type=F32) + jnp.dot(lo, bd, preferred_element_type=F32)
        y = xh * lax.rsqrt(ms + EPS) * gain
        lane = lax.broadcasted_iota(jnp.int32, y.shape, 1)
        first = (lane & (AXIS_DIM - 1)) < quarter
        partner = jnp.where(first, pltpu.roll(y, w - quarter, 1), pltpu.roll(y, quarter, 1))
        reps = w // cos.shape[1]
        c = jnp.concatenate([cos] * reps, axis=1) if reps > 1 else cos
        s = jnp.concatenate([sin] * reps, axis=1) if reps > 1 else sin
        return y * c + partner * s

    q_ref[...] = (prep(x[:, :ATT_W], qg_ref[...], bdq_ref[...]) * ATT_SCALE).astype(BF16)
    k_ref[...] = prep(x[:, ATT_W:ATT_W + KV_W], kg_ref[...], bdk_ref[...]).astype(BF16)


def _qk_prep(p, cos, sin, q_gain, k_gain, rows_per_batch, tm):
    T = p.shape[0]
    tpb = rows_per_batch // tm
    qg = jnp.tile(q_gain, N_HEADS)[None, :]
    kg = jnp.tile(k_gain, N_KV_HEADS)[None, :]

    def block_avg(w):
        hid = jnp.arange(w) // HEAD_DIM
        return jnp.where(hid[:, None] == hid[None, :], 1.0 / HEAD_DIM, 0.0).astype(BF16)

    full = lambda i: (0, 0)
    return pl.pallas_call(
        _qk_prep_kernel,
        grid=(T // tm,),
        in_specs=[
            pl.BlockSpec((tm, QKV_W), lambda i: (i, Q0 // QKV_W)),
            pl.BlockSpec((tm, 2 * HEAD_DIM), lambda i: (i % tpb, 0)),
            pl.BlockSpec((tm, 2 * HEAD_DIM), lambda i: (i % tpb, 0)),
            pl.BlockSpec((1, ATT_W), full),
            pl.BlockSpec((1, KV_W), full),
            pl.BlockSpec((ATT_W, ATT_W), full),
            pl.BlockSpec((KV_W, KV_W), full),
        ],
        out_specs=[pl.BlockSpec((tm, ATT_W), lambda i: (i, 0)), pl.BlockSpec((tm, KV_W), lambda i: (i, 0))],
        out_shape=[jax.ShapeDtypeStruct((T, ATT_W), BF16), jax.ShapeDtypeStruct((T, KV_W), BF16)],
        compiler_params=_cparams(("parallel",)),
        name="qk_prep",
    )(p, cos, sin, qg, kg, block_avg(ATT_W), block_avg(KV_W))


def _shift_rows(cur, prev8, next8, first, last):
    tm = cur.shape[0]
    rid = lax.broadcasted_iota(jnp.int32, cur.shape, 0)
    pr = jnp.where(first, 0.0, prev8[HALO - 1:HALO, :])
    nx = jnp.where(last, 0.0, next8[0:1, :])
    up = jnp.where(rid == 0, pr, pltpu.roll(cur, 1, 0))
    dn = jnp.where(rid == tm - 1, nx, pltpu.roll(cur, tm - 1, 0))
    return up, dn


def _halo_specs(tm, width, col_block, n_rows):
    r = tm // HALO
    last = n_rows // HALO - 1
    prev = pl.BlockSpec((1, HALO, width), lambda b, i: (b, jnp.maximum(i * r - 1, 0), col_block))
    nxt = pl.BlockSpec((1, HALO, width), lambda b, i: (b, jnp.minimum((i + 1) * r, last), col_block))
    return prev, nxt


def _short_conv_kernel(bg_ref, cg_ref, xs_ref, cgp_ref, xsp_ref, cgn_ref, xsn_ref, w_ref, o_ref):
    i = pl.program_id(1)
    f = lambda r: r[0].astype(F32)
    cur = f(cg_ref) * f(xs_ref)
    up, dn = _shift_rows(cur, f(cgp_ref) * f(xsp_ref), f(cgn_ref) * f(xsn_ref), i == 0, i == pl.num_programs(1) - 1)
    o_ref[0] = (f(bg_ref) * (up * w_ref[0:1] + cur * w_ref[1:2] + dn * w_ref[2:3])).astype(o_ref.dtype)


def _short_conv(p3, w, tm):
    B, L, _ = p3.shape
    c0 = SC0 // SC_W
    blk = lambda j: pl.BlockSpec((1, tm, SC_W), lambda b, i: (b, i, c0 + j))
    cgp, cgn = _halo_specs(tm, SC_W, c0 + 1, L)
    xsp, xsn = _halo_specs(tm, SC_W, c0 + 2, L)
    return pl.pallas_call(
        _short_conv_kernel,
        grid=(B, L // tm),
        in_specs=[blk(0), blk(1), blk(2), cgp, xsp, cgn, xsn, pl.BlockSpec((3, SC_W), lambda b, i: (0, 0))],
        out_specs=pl.BlockSpec((1, tm, SC_W), lambda b, i: (b, i, 0)),
        out_shape=jax.ShapeDtypeStruct((B, L, SC_W), BF16),
        compiler_params=_cparams(("parallel", "parallel")),
        name="short_conv",
    )(p3, p3, p3, p3, p3, p3, p3, w)


def _hy_pre_kernel(p_ref, pp_ref, pn_ref, w_ref, b_ref, v_ref, x1_ref, x2_ref):
    i = pl.program_id(1)
    cur = p_ref[0].astype(F32)
    up, dn = _shift_rows(cur, pp_ref[0].astype(F32), pn_ref[0].astype(F32), i == 0, i == pl.num_programs(1) - 1)
    u = up * w_ref[0:1] + cur * w_ref[1:2] + dn * w_ref[2:3] + b_ref[...]
    v_ref[0] = u[:, :HY_W].astype(BF16)
    x1_ref[0] = u[:, HY_W:2 * HY_W].astype(BF16)
    x2_ref[0] = u[:, 2 * HY_W:].astype(BF16)


def _hy_pre(p3, w, b, tm):
    B, L, _ = p3.shape
    W = (HY_ORDER + 1) * HY_W
    c0 = HY0 // W
    prev, nxt = _halo_specs(tm, W, c0, L)
    out = jax.ShapeDtypeStruct((B, L, HY_W), BF16)
    ospec = pl.BlockSpec((1, tm, HY_W), lambda b_, i: (b_, i, 0))
    return pl.pallas_call(
        _hy_pre_kernel,
        grid=(B, L // tm),
        in_specs=[pl.BlockSpec((1, tm, W), lambda b_, i: (b_, i, c0)), prev, nxt,
                  pl.BlockSpec((3, W), lambda b_, i: (0, 0)), pl.BlockSpec((1, W), lambda b_, i: (0, 0))],
        out_specs=[ospec, ospec, ospec],
        out_shape=[out, out, out],
        compiler_params=_cparams(("parallel", "parallel")),
        name="hyena_pre",
    )(p3, p3, p3, w, b[None, :])


def _hy_features(L):
    n = jnp.arange(2 * L, dtype=jnp.int32)
    j = jnp.where(n < L, n, jnp.where(n == L, 0, 2 * L - n)).astype(F32)[:, None]
    t = j / (L - 1)
    w = 2.0 * math.pi * j / L
    bands = jnp.linspace(1e-4, HY_BANDS - 1, HY_BANDS, dtype=F32)[None, :]
    z = jnp.concatenate([t, jnp.cos(bands * w), -jnp.sin(bands * w)], axis=-1)
    return jnp.pad(z, ((0, 0), (0, HY_FEAT - z.shape[1])))


def _hy_filter_kernel(z_ref, w1_ref, b1_ref, f1_ref, w2_ref, b2_ref, f2_ref, w3_ref, dec_ref,
                      k0_ref, k1_ref, sum_ref):
    i = pl.program_id(0)
    half = pl.num_programs(0) // 2

    @pl.when(i == 0)
    def _():
        sum_ref[...] = jnp.zeros(sum_ref.shape, F32)

    hp = lax.Precision.HIGHEST
    z = z_ref[...]
    h = jnp.sin(f1_ref[...] * (jnp.dot(z, w1_ref[...], preferred_element_type=F32, precision=hp) + b1_ref[...]))
    h = jnp.sin(f2_ref[...] * (jnp.dot(h, w2_ref[...], preferred_element_type=F32, precision=hp) + b2_ref[...]))
    k = jnp.dot(h, w3_ref[0], preferred_element_type=F32, precision=hp)
    k = k * jnp.exp(-z[:, 0:1] * dec_ref[...])
    sum_ref[...] += jnp.sum(jnp.abs(k), axis=0, keepdims=True)
    rid = lax.broadcasted_iota(jnp.int32, k.shape, 0)
    k = jnp.where(i == half, jnp.where(rid == 0, 0.0, k), k)
    k0_ref[...] = k[:, :HY_W].astype(BF16)
    k1_ref[...] = k[:, HY_W:].astype(BF16)


def _hy_filter(L, w1, b1, f1, w2, b2, f2, w3, decay, nb):
    N = 2 * L
    fh = w1.shape[1]
    z = _hy_features(L)
    w1p = jnp.pad(w1, ((0, HY_FEAT - w1.shape[0]), (0, 0)))
    w3d = w3.reshape(fh, HY_ORDER, 2, HY_W).transpose(2, 0, 1, 3).reshape(2, fh, HY_ORDER * HY_W)
    dec = jnp.abs(decay).reshape(1, HY_ORDER * HY_W)
    row = lambda a: a[None, :]
    full = lambda i: (0, 0)
    half = N // nb // 2
    k0, k1, tot = pl.pallas_call(
        _hy_filter_kernel,
        grid=(N // nb,),
        in_specs=[
            pl.BlockSpec((nb, HY_FEAT), lambda i: (i, 0)),
            pl.BlockSpec((HY_FEAT, fh), full), pl.BlockSpec((1, fh), full), pl.BlockSpec((1, fh), full),
            pl.BlockSpec((fh, fh), full), pl.BlockSpec((1, fh), full), pl.BlockSpec((1, fh), full),
            pl.BlockSpec((1, fh, HY_ORDER * HY_W), lambda i: (i // half, 0, 0)),
            pl.BlockSpec((1, HY_ORDER * HY_W), full),
        ],
        out_specs=[pl.BlockSpec((nb, HY_W), lambda i: (i, 0)), pl.BlockSpec((nb, HY_W), lambda i: (i, 0)),
                   pl.BlockSpec((1, HY_ORDER * HY_W), full)],
        out_shape=[jax.ShapeDtypeStruct((N, HY_W), BF16), jax.ShapeDtypeStruct((N, HY_W), BF16),
                   jax.ShapeDtypeStruct((1, HY_ORDER * HY_W), F32)],
        compiler_params=_cparams(("arbitrary",)),
        name="hyena_filter",
    )(z, w1p, row(b1), row(f1), w2, row(b2), row(f2), w3d, dec)
    return k0, k1, 1.0 / tot


def _dft_tables(N1, N2):
    N = N1 * N2
    ar = lambda n: jnp.arange(n, dtype=jnp.int32)

    def cs(m, period):
        ang = (-2.0 * math.pi / period) * (m % period).astype(F32)
        return jnp.cos(ang), jnp.sin(ang)

    fr, fi = cs(ar(N1)[:, None] * ar(N1)[None, :], N1)
    hr, hi = fr[:, :max(N1 // 2, 1)], fi[:, :max(N1 // 2, 1)]
    w_fwd = jnp.block([[hr, -hi], [hi, hr]])
    w_real = jnp.concatenate([fr, fi], axis=0)
    w_inv = jnp.block([[hr.T, hi.T], [-hi.T, hr.T]]) / N1
    f2r, f2i = cs(ar(N2)[:, None] * ar(N2)[None, :], N2)
    tr, ti = cs(ar(N1)[:, None] * ar(N2)[None, :], N)
    gr = f2r[None] * tr[:, None, :] - f2i[None] * ti[:, None, :]
    gi = f2r[None] * ti[:, None, :] + f2i[None] * tr[:, None, :]
    g = jnp.concatenate([jnp.concatenate([gr, -gi], axis=2), jnp.concatenate([gi, gr], axis=2)], axis=1)
    grt, git = gr.transpose(0, 2, 1), gi.transpose(0, 2, 1)
    gh = jnp.concatenate([jnp.concatenate([grt, git], axis=2), jnp.concatenate([-git, grt], axis=2)], axis=1) / N2
    return tuple(a.astype(BF16) for a in (w_fwd, w_real, w_inv, g, gh))


def _colmm_kernel(w_ref, x_ref, o_ref):
    o_ref[...] = jnp.dot(w_ref[...], x_ref[...], preferred_element_type=F32).astype(o_ref.dtype)


def _colmm(w, x, cb):
    M, K = w.shape
    C = x.shape[1]
    return pl.pallas_call(
        _colmm_kernel,
        grid=(C // cb,),
        in_specs=[pl.BlockSpec((M, K), lambda j: (0, 0)), pl.BlockSpec((K, cb), lambda j: (0, j))],
        out_specs=pl.BlockSpec((M, cb), lambda j: (0, j)),
        out_shape=jax.ShapeDtypeStruct((M, C), BF16),
        compiler_params=_cparams(("parallel",)),
        name="dft_outer",
    )(w, x)


def _colmm_gate_kernel(w_ref, d_ref, u_ref, g_ref, b_ref, o_ref):
    y = jnp.dot(w_ref[...], d_ref[...], preferred_element_type=F32)
    o_ref[...] = (g_ref[...].astype(F32) * (y + u_ref[...].astype(F32) * b_ref[...])).astype(o_ref.dtype)


def _colmm_gate(w, d, u, gate, bias_row, cb):
    M, K = w.shape
    C = d.shape[1]
    col = lambda j: (0, j)
    return pl.pallas_call(
        _colmm_gate_kernel,
        grid=(C // cb,),
        in_specs=[pl.BlockSpec((M, K), lambda j: (0, 0)), pl.BlockSpec((K, cb), col), pl.BlockSpec((M, cb), col),
                  pl.BlockSpec((M, cb), col), pl.BlockSpec((1, cb), lambda j: (0, 0))],
        out_specs=pl.BlockSpec((M, cb), col),
        out_shape=jax.ShapeDtypeStruct((M, C), BF16),
        compiler_params=_cparams(("parallel",)),
        name="dft_outer_gate",
    )(w, d, u, gate, bias_row)


def _mid_fwd_kernel(a_ref, g_ref, s_ref, k_ref):
    n2 = a_ref.shape[2]
    a = a_ref[:, 0].reshape(2 * n2, a_ref.shape[3])
    x = jnp.dot(g_ref[0], a, preferred_element_type=F32) * s_ref[...]
    k_ref[:, 0] = x.reshape(2, n2, x.shape[1])


def _mid_fwd(a, g, scale):
    _, N1, N2, C = a.shape
    blk = pl.BlockSpec((2, 1, N2, C), lambda i: (0, i, 0, 0))
    return pl.pallas_call(
        _mid_fwd_kernel,
        grid=(N1,),
        in_specs=[blk, pl.BlockSpec((1, 2 * N2, 2 * N2), lambda i: (i, 0, 0)), pl.BlockSpec((1, C), lambda i: (0, 0))],
        out_specs=blk,
        out_shape=jax.ShapeDtypeStruct((2, N1, N2, C), F32),
        compiler_params=_cparams(("parallel",)),
        name="dft_inner_filter",
    )(a, g, scale)


def _mid_kernel(a_ref, g_ref, gh_ref, k_ref, d_ref):
    n2 = a_ref.shape[2]
    c = a_ref.shape[3]
    a = a_ref[:, 0].reshape(2 * n2, c)
    x = jnp.dot(g_ref[0], a, preferred_element_type=F32)
    xr, xi = x[:n2], x[n2:]
    kr, ki = k_ref[0, 0], k_ref[1, 0]
    y = jnp.concatenate([xr * kr - xi * ki, xr * ki + xi * kr], axis=0).astype(BF16)
    d = jnp.dot(gh_ref[0], y, preferred_element_type=F32)
    d_ref[:, 0] = d.reshape(2, n2, c).astype(d_ref.dtype)


def _mid(a, g, gh, kf):
    _, N1, N2, C = a.shape
    blk = pl.BlockSpec((2, 1, N2, C), lambda i: (0, i, 0, 0))
    tab = pl.BlockSpec((1, 2 * N2, 2 * N2), lambda i: (i, 0, 0))
    return pl.pallas_call(
        _mid_kernel,
        grid=(N1,),
        in_specs=[blk, tab, tab, blk],
        out_specs=blk,
        out_shape=jax.ShapeDtypeStruct((2, N1, N2, C), BF16),
        compiler_params=_cparams(("parallel",)),
        name="dft_inner",
    )(a, g, gh, kf)


def _hyena_long(v, x1, x2, filt, bias, tabs, cb):
    B, L, C = v.shape
    assert B == 2, "the two batches are packed as real / imaginary parts of one complex signal"
    w_fwd, w_real, w_inv, g, gh = tabs
    N1 = g.shape[0]
    N2 = g.shape[1] // 2
    k0, k1, inv_norm = filt
    flat = lambda a: a.reshape(-1, N2 * C)
    z = flat(v)
    for o, (ker, gate) in enumerate(((k0, x1), (k1, x2))):
        if N1 > 1:
            ka = _colmm(w_real, flat(ker), cb).reshape(2, N1, N2, C)
            za = _colmm(w_fwd, z, cb).reshape(2, N1, N2, C)
        else:
            ka = jnp.stack([ker, jnp.zeros_like(ker)]).reshape(2, 1, N2, C)
            za = jnp.pad(z.reshape(2, L, C), ((0, 0), (0, L), (0, 0))).reshape(2, 1, N2, C)
        kf = _mid_fwd(ka, g, inv_norm[:, o * C:(o + 1) * C])
        d = _mid(za, g, gh, kf)
        if N1 > 1:
            z = _colmm_gate(w_inv, flat(d), z, flat(gate), jnp.tile(bias[o], cb // C)[None, :], cb)
        else:
            y = d.reshape(2, N2, C)[:, :L].astype(F32)
            zf = z.reshape(2, L, C).astype(F32)
            z = (gate.astype(F32) * (y + zf * bias[o])).astype(BF16).reshape(-1, N2 * C // 2)
    return z.reshape(B, L, C)


def _resid_kernel(x_ref, fT_ref, gt_ref, g_ref, o_ref, *, final):
    xn = x_ref[...] + gt_ref[0] * fT_ref[...].T
    if final:
        xn = xn * lax.rsqrt(jnp.mean(xn * xn, axis=-1, keepdims=True) + EPS) * g_ref[...]
    o_ref[...] = xn


def _resid(x2d, fT, col0, gt, g, rows_per_batch, tm, final):
    T, D = x2d.shape
    tpb = rows_per_batch // tm
    c0 = col0 // tm
    row = lambda i: (i, 0)
    return pl.pallas_call(
        functools.partial(_resid_kernel, final=final),
        grid=(T // tm,),
        in_specs=[pl.BlockSpec((tm, D), row), pl.BlockSpec((D, tm), lambda i: (0, c0 + i)),
                  pl.BlockSpec((1, 1, D), lambda i: (i // tpb, 0, 0)), pl.BlockSpec((1, D), lambda i: (0, 0))],
        out_specs=pl.BlockSpec((tm, D), row),
        out_shape=jax.ShapeDtypeStruct((T, D), F32),
        compiler_params=_cparams(("parallel",)),
        name="residual",
    )(x2d, fT, gt, g)


def _peer(h2, wq, subkeys, u_bf, vT_bf, tc, nb):
    T, D = h2.shape
    hT = h2.T
    wqT = wq.T
    wqT_hi = wqT.astype(BF16)
    wqT_lo = (wqT - wqT_hi.astype(F32)).astype(BF16)
    cnt, e1, r2, e2 = _peer_route(hT, wqT_hi, wqT_lo, subkeys, tc)
    cnt4 = cnt.reshape(PEER_HEADS, PEER_NKEYS // nb, nb, T)
    e14 = e1.reshape(PEER_HEADS, PEER_NKEYS // nb, nb, T)
    return _peer_dense(hT, r2, e2, cnt4, e14, u_bf, vT_bf, tc, nb)


def kernel(x, c, ctx, c_ctx, w_mod, b_mod, g_norm1, g_norm2, w_in, q_gain, k_gain, sc_conv_w, hy_conv_w,
           hy_conv_b, hy_w1, hy_b1, hy_f1, hy_w2, hy_b2, hy_f2, hy_w3, hy_decay, hy_bias, w_br_att, w_br_sc,
           w_br_hy, w_out, peer_wq, peer_subkeys, peer_u, peer_v, g_final):
    B, S, D = x.shape
    Lc = ctx.shape[1]
    hp = lax.Precision.HIGHEST
    cond = jnp.concatenate([jax.nn.silu(c), jnp.broadcast_to(jax.nn.silu(c_ctx), (B, D))], axis=0)
    cos, sin = _rope_tables(S, True)
    cos_c, sin_c = _rope_tables(Lc, False)
    tm = min(1024, S)
    tq = min(128, S)
    tr = min(512, S)
    cb = 16 * HY_W
    tabs = _dft_tables(2 * S // HY_N2, HY_N2)
    tabs_c = _dft_tables(1, 2 * Lc)
    x2 = x.reshape(B * S, D)
    ctx2 = ctx.reshape(B * Lc, D)

    for l in range(DEPTH):
        need_ctx = l < DEPTH - 1
        mod = (jnp.dot(cond, w_mod[l], precision=hp) + b_mod[l]).reshape(2, B, 1, 6, D)
        sh1, sc1, gt1, sh2, sc2, gt2 = (mod[0, :, :, i] for i in range(6))
        csh1, csc1, cgt1, csh2, csc2, cgt2 = (mod[1, :, :, i] for i in range(6))
        w_in_bf = jnp.concatenate([w_in[l][:, REF_GT0:], w_in[l][:, REF_SC0:REF_GT0], w_in[l][:, :REF_SC0]],
                                  axis=1).astype(BF16)
        wa, ws, wh, wo = (w.astype(BF16) for w in (w_br_att[l], w_br_sc[l], w_br_hy[l], w_out[l]))
        g1 = g_norm1[l][None, :]
        g2 = g_norm2[l][None, :]
        hy_params = (hy_w1[l], hy_b1[l], hy_f1[l], hy_w2[l], hy_b2[l], hy_f2[l], hy_w3[l], hy_decay[l])

        p = _in_proj(x2, g1, sc1, sh1, w_in_bf, S, tm, 768)
        p3 = p.reshape(B, S, -1)
        pc = _in_proj(ctx2, g1, csc1, csh1, w_in_bf, Lc, Lc, 768)
        pc3 = pc.reshape(B, Lc, -1)

        qs, k = _qk_prep(p, cos, sin, q_gain[l], k_gain[l], S, tr)
        qcs, kc = _qk_prep(pc, cos_c, sin_c, q_gain[l], k_gain[l], Lc, Lc)
        k_all = jnp.concatenate([kc.reshape(B, Lc, KV_W), k.reshape(B, S, KV_W)], axis=1)
        v_all = jnp.concatenate([pc3[..., V0:IN_W], p3[..., V0:IN_W]], axis=1)
        tk = (S + Lc) // 13 if (S + Lc) % (13 * 128) == 0 else 128
        y_att = _attention(qs.reshape(B, S, ATT_W), k_all, v_all, tq, tk).reshape(B * S, ATT_W)

        y_sc = _short_conv(p3, sc_conv_w[l], tr).reshape(B * S, SC_W)
        y_hy = _hyena_long(*_hy_pre(p3, hy_conv_w[l], hy_conv_b[l], tr), _hy_filter(S, *hy_params, tr),
                           hy_bias[l], tabs, cb).reshape(B * S, HY_W)
        x2, h2 = _merge(x2, y_att, y_sc, y_hy, p, gt1, g2, sc2, sh2, wa, ws, wh, wo, S, tr)

        if need_ctx:
            yc_att = _attention(qcs.reshape(B, Lc, ATT_W), kc.reshape(B, Lc, KV_W), pc3[..., V0:IN_W], Lc, Lc)
            yc_sc = _short_conv(pc3, sc_conv_w[l], Lc).reshape(B * Lc, SC_W)
            yc_hy = _hyena_long(*_hy_pre(pc3, hy_conv_w[l], hy_conv_b[l], Lc), _hy_filter(Lc, *hy_params, Lc),
                                hy_bias[l], tabs_c, cb).reshape(B * Lc, HY_W)
            ctx2, h2c = _merge(ctx2, yc_att.reshape(B * Lc, ATT_W), yc_sc, yc_hy, pc, cgt1, g2, csc2, csh2,
                               wa, ws, wh, wo, Lc, Lc)
            tok = jnp.concatenate([h2, h2c], axis=0)
        else:
            tok = h2

        u_bf = peer_u[l].astype(BF16)
        vT_bf = peer_v[l].astype(BF16).reshape(-1, PEER_BLOCK * PEER_NKEYS, D).transpose(0, 2, 1)
        fT = _peer(tok, peer_wq[l], peer_subkeys[l], u_bf, vT_bf, min(512, tok.shape[0]), PEER_BLOCK)
        last = l == DEPTH - 1
        x2 = _resid(x2, fT, 0, gt2, g_final[None, :], S, tr, last)
        if need_ctx:
            ctx2 = _resid(ctx2, fT, B * S, cgt2, g_final[None, :], Lc, Lc, False)

    return x2.reshape(B, S, D)
```

```python
import functools
import math

import jax
import jax.numpy as jnp
from jax import lax
from jax.experimental import pallas as pl
from jax.experimental.pallas import tpu as pltpu

F32 = jnp.float32
BF16 = jnp.bfloat16

DEPTH = 2
GRID_W = 64
EPS = 1e-6
N_HEADS = 8
N_KV_HEADS = 2
GQA_GROUP = N_HEADS // N_KV_HEADS
HEAD_DIM = 64
AXIS_DIM = HEAD_DIM // 2
ATT_W = N_HEADS * HEAD_DIM
KV_W = N_KV_HEADS * HEAD_DIM
ATT_SCALE = HEAD_DIM ** -0.5
ROPE_THETA = 10000.0
SC_W = 512
HY_W = 512
HY_ORDER = 2
HY_BANDS = 16
PEER_HEADS = 8
PEER_NKEYS = 128
PEER_DK = 128
PEER_TOPK = 16
D_MODEL = 1024
QKV_W = ATT_W + 2 * KV_W
REF_SC0 = QKV_W
REF_GT0 = QKV_W + 3 * SC_W + (HY_ORDER + 1) * HY_W
GT0 = 0
SC0 = GT0 + 3 * D_MODEL
HY0 = SC0 + 3 * SC_W
Q0 = HY0 + (HY_ORDER + 1) * HY_W
K0 = Q0 + ATT_W
V0 = K0 + KV_W
IN_W = V0 + KV_W
HALO = 8
HY_N2 = 256
HY_FEAT = 128

VMEM_LIMIT = 56 * 1024 * 1024


def _cparams(sem):
    return pltpu.CompilerParams(dimension_semantics=sem, vmem_limit_bytes=VMEM_LIMIT)


def _in_proj_kernel(x_ref, g_ref, sc_ref, sh_ref, w_ref, o_ref, h_scr):
    @pl.when(pl.program_id(1) == 0)
    def _():
        x = x_ref[...]
        ms = jnp.mean(x * x, axis=-1, keepdims=True)
        y = x * lax.rsqrt(ms + EPS) * g_ref[...]
        h_scr[...] = (y * (1.0 + sc_ref[0]) + sh_ref[0]).astype(BF16)

    o_ref[...] = jnp.dot(h_scr[...], w_ref[...], preferred_element_type=F32).astype(o_ref.dtype)


def _in_proj(x2d, g, sc, sh, w, rows_per_batch, tm, tn):
    T, D = x2d.shape
    N = w.shape[1]
    tpb = rows_per_batch // tm
    return pl.pallas_call(
        _in_proj_kernel,
        grid=(T // tm, N // tn),
        in_specs=[
            pl.BlockSpec((tm, D), lambda i, j: (i, 0)),
            pl.BlockSpec((1, D), lambda i, j: (0, 0)),
            pl.BlockSpec((1, 1, D), lambda i, j: (i // tpb, 0, 0)),
            pl.BlockSpec((1, 1, D), lambda i, j: (i // tpb, 0, 0)),
            pl.BlockSpec((D, tn), lambda i, j: (0, j)),
        ],
        out_specs=pl.BlockSpec((tm, tn), lambda i, j: (i, j)),
        out_shape=jax.ShapeDtypeStruct((T, N), BF16),
        scratch_shapes=[pltpu.VMEM((tm, D), BF16)],
        compiler_params=_cparams(("parallel", "arbitrary")),
        name="in_proj",
    )(x2d, g, sc, sh, w)


def _attn_kernel(q_ref, kT_ref, v_ref, o_ref, q_scr, m_scr, acc_scr, sa_scr, sb_scr, *, nkb):
    tq = q_ref.shape[1]
    m_scr[...] = jnp.full(m_scr.shape, -jnp.inf, F32)
    acc_scr[...] = jnp.zeros(acc_scr.shape, F32)
    for g in range(GQA_GROUP):
        q_scr[g * tq:(g + 1) * tq, :] = q_ref[0, :, g * HEAD_DIM:(g + 1) * HEAD_DIM]

    def scores(j, dst_scr):
        dst_scr[...] = jnp.dot(q_scr[...], kT_ref[0, 0, j], preferred_element_type=F32)

    def update(j, src_scr):
        s = src_scr[...]
        m_prev = m_scr[...]
        m_new = jnp.maximum(m_prev, jnp.max(s, axis=-1, keepdims=True))
        p = jnp.exp(s - m_new).astype(BF16)
        alpha = jnp.exp(m_prev - m_new)
        acc_scr[...] = acc_scr[...] * alpha + jnp.dot(p, v_ref[0, 0, j], preferred_element_type=F32)
        m_scr[...] = m_new

    scores(0, sa_scr)

    def pair(i, carry):
        j = 2 * i
        scores(j + 1, sb_scr)
        update(j, sa_scr)
        scores(j + 2, sa_scr)
        update(j + 1, sb_scr)
        return carry

    lax.fori_loop(0, (nkb - 1) // 2, pair, 0)
    if nkb % 2 == 1:
        update(nkb - 1, sa_scr)
    else:
        scores(nkb - 1, sb_scr)
        update(nkb - 2, sa_scr)
        update(nkb - 1, sb_scr)
    acc = acc_scr[...]
    o = (acc[:, :HEAD_DIM] / acc[:, HEAD_DIM:HEAD_DIM + 1]).astype(o_ref.dtype)
    o_ref[0] = jnp.concatenate([o[g * tq:(g + 1) * tq] for g in range(GQA_GROUP)], axis=1)


def _attention(q, k, v, tq, tk):
    B, Lq, _ = q.shape
    Lk = k.shape[1]
    nqb, nkb = Lq // tq, Lk // tk
    R = GQA_GROUP * tq
    GW = GQA_GROUP * HEAD_DIM
    kT = k.reshape(B, nkb, tk, N_KV_HEADS, HEAD_DIM).transpose(0, 3, 1, 4, 2)
    vb = v.reshape(B, nkb, tk, N_KV_HEADS, HEAD_DIM).transpose(0, 3, 1, 2, 4)
    ones = jnp.ones(vb.shape[:-1] + (1,), BF16)
    zeros = jnp.zeros(vb.shape[:-1] + (HEAD_DIM - 1,), BF16)
    vb = jnp.concatenate([vb, ones, zeros], axis=-1)
    return pl.pallas_call(
        functools.partial(_attn_kernel, nkb=nkb),
        grid=(B, N_KV_HEADS, nqb),
        in_specs=[
            pl.BlockSpec((1, tq, GW), lambda b, h, i: (b, i, h)),
            pl.BlockSpec((1, 1, nkb, HEAD_DIM, tk), lambda b, h, i: (b, h, 0, 0, 0)),
            pl.BlockSpec((1, 1, nkb, tk, 2 * HEAD_DIM), lambda b, h, i: (b, h, 0, 0, 0)),
        ],
        out_specs=pl.BlockSpec((1, tq, GW), lambda b, h, i: (b, i, h)),
        out_shape=jax.ShapeDtypeStruct((B, Lq, ATT_W), BF16),
        scratch_shapes=[pltpu.VMEM((R, HEAD_DIM), BF16), pltpu.VMEM((R, 1), F32), pltpu.VMEM((R, 2 * HEAD_DIM), F32),
                        pltpu.VMEM((R, tk), F32), pltpu.VMEM((R, tk), F32)],
        compiler_params=_cparams(("parallel", "parallel", "arbitrary")),
        name="attention",
    )(q, kT, vb)


def _merge_kernel(x_ref, ya_ref, ys_ref, yh_ref, ga_ref, gs_ref, gh_ref, gt_ref, g2_ref, sc_ref, sh_ref,
                  wa_ref, ws_ref, wh_ref, wo_ref, xo_ref, h2_ref):
    def br(y_ref, g_ref, w_ref):
        gate = jax.nn.sigmoid(g_ref[...].astype(F32))
        return gate * jnp.dot(y_ref[...], w_ref[...], preferred_element_type=F32)

    m = br(ya_ref, ga_ref, wa_ref) + br(ys_ref, gs_ref, ws_ref) + br(yh_ref, gh_ref, wh_ref)
    o = jnp.dot(m.astype(BF16), wo_ref[...], preferred_element_type=F32)
    xn = x_ref[...] + gt_ref[0] * o
    xo_ref[...] = xn
    ms = jnp.mean(xn * xn, axis=-1, keepdims=True)
    y = xn * lax.rsqrt(ms + EPS) * g2_ref[...]
    h2_ref[...] = (y * (1.0 + sc_ref[0]) + sh_ref[0]).astype(BF16)


def _merge(x2d, ya, ys, yh, p, gt1, g2, sc2, sh2, wa, ws, wh, wo, rows_per_batch, tm):
    T, D = x2d.shape
    tpb = rows_per_batch // tm
    gblk = GT0 // D
    row = lambda i: (i, 0)
    mod = lambda i: (i // tpb, 0, 0)
    full = lambda i: (0, 0)
    return pl.pallas_call(
        _merge_kernel,
        grid=(T // tm,),
        in_specs=[
            pl.BlockSpec((tm, D), row),
            pl.BlockSpec((tm, ATT_W), row),
            pl.BlockSpec((tm, SC_W), row),
            pl.BlockSpec((tm, HY_W), row),
            pl.BlockSpec((tm, D), lambda i: (i, gblk)),
            pl.BlockSpec((tm, D), lambda i: (i, gblk + 1)),
            pl.BlockSpec((tm, D), lambda i: (i, gblk + 2)),
            pl.BlockSpec((1, 1, D), mod),
            pl.BlockSpec((1, D), full),
            pl.BlockSpec((1, 1, D), mod),
            pl.BlockSpec((1, 1, D), mod),
            pl.BlockSpec((ATT_W, D), full),
            pl.BlockSpec((SC_W, D), full),
            pl.BlockSpec((HY_W, D), full),
            pl.BlockSpec((D, D), full),
        ],
        out_specs=[pl.BlockSpec((tm, D), row), pl.BlockSpec((tm, D), row)],
        out_shape=[jax.ShapeDtypeStruct((T, D), F32), jax.ShapeDtypeStruct((T, D), BF16)],
        compiler_params=_cparams(("parallel",)),
        name="merge",
    )(x2d, ya, ys, yh, p, p, p, gt1, g2, sc2, sh2, wa, ws, wh, wo)


LANES = 128
PACKED_ROWS = 16
PEER_BLOCK = 8
PEER_ACC = 4


def _peer_kernel(hT_ref, r2_ref, e2_ref, n_ref, e1_ref, u_ref, vT_ref, o_ref, gwa_scr, gwb_scr, act_scr, *, nb):
    e = pl.program_id(1)
    tc = hT_ref.shape[1]

    @pl.when(e == 0)
    def _():
        o_ref[...] = jnp.zeros(o_ref.shape, F32)
        gwb_scr[...] = jnp.zeros(gwb_scr.shape, BF16)

    def step(cur_scr, prev_scr):
        o_ref[...] += jnp.dot(vT_ref[0], prev_scr[...], preferred_element_type=F32)
        blk = jnp.minimum(e, pl.num_programs(1) - 2)
        a = jnp.dot(u_ref[...], hT_ref[...], preferred_element_type=F32)
        act_scr[...] = (0.5 * a * (1.0 + lax.erf(a * (2.0 ** -0.5)))).astype(BF16)
        shape3 = (PEER_NKEYS // PACKED_ROWS, PACKED_ROWS, LANES)
        for lt in range(tc // LANES):
            lanes = slice(lt * LANES, (lt + 1) * LANES)
            for i0 in range(0, nb, PEER_ACC):
                ws = [jnp.zeros(shape3, BF16) for _ in range(PEER_ACC)]
                for h in range(PEER_HEADS):
                    r2t = pltpu.bitcast(r2_ref[h, :, lanes], BF16).reshape(shape3)
                    e2t = pltpu.bitcast(e2_ref[h, :, lanes], BF16).reshape(shape3)
                    for k in range(PEER_ACC):
                        ii = i0 + k
                        cnt = jnp.broadcast_to(n_ref[h, blk, ii:ii + 1, lanes], (PACKED_ROWS, LANES)).astype(BF16)
                        e1row = jnp.broadcast_to(e1_ref[h, blk, ii:ii + 1, lanes], (PACKED_ROWS, LANES)).astype(BF16)
                        ws[k] = ws[k] + jnp.where(r2t < cnt[None], e2t, jnp.zeros_like(e2t)) * e1row[None]
                for k in range(PEER_ACC):
                    rows = slice((i0 + k) * PEER_NKEYS, (i0 + k + 1) * PEER_NKEYS)
                    cur_scr[rows, lanes] = act_scr[rows, lanes] * ws[k].reshape(PEER_NKEYS, LANES)

    @pl.when(e % 2 == 0)
    def _():
        step(gwa_scr, gwb_scr)

    @pl.when(e % 2 == 1)
    def _():
        step(gwb_scr, gwa_scr)


def _peer_dense(hT, r2, e2, cnt, e1, u, vT, tc, nb):
    D, T = hT.shape
    N = u.shape[0]
    eb = nb * PEER_NKEYS
    ne = N // eb
    assert ne % 2 == 0, "the drain step must find the last block in the buffer the parity rule reads"
    cur = lambda e: jnp.minimum(e, ne - 1)
    return pl.pallas_call(
        functools.partial(_peer_kernel, nb=nb),
        grid=(T // tc, ne + 1),
        in_specs=[
            pl.BlockSpec((D, tc), lambda t, e: (0, t)),
            pl.BlockSpec((PEER_HEADS, PEER_NKEYS // 2, tc), lambda t, e: (0, 0, t)),
            pl.BlockSpec((PEER_HEADS, PEER_NKEYS // 2, tc), lambda t, e: (0, 0, t)),
            pl.BlockSpec((PEER_HEADS, ne, nb, tc), lambda t, e: (0, 0, 0, t)),
            pl.BlockSpec((PEER_HEADS, ne, nb, tc), lambda t, e: (0, 0, 0, t)),
            pl.BlockSpec((eb, D), lambda t, e: (cur(e), 0)),
            pl.BlockSpec((1, D, eb), lambda t, e: (jnp.maximum(e - 1, 0), 0, 0)),
        ],
        out_specs=pl.BlockSpec((D, tc), lambda t, e: (0, t)),
        out_shape=jax.ShapeDtypeStruct((D, T), F32),
        scratch_shapes=[pltpu.VMEM((eb, tc), BF16), pltpu.VMEM((eb, tc), BF16), pltpu.VMEM((eb, tc), BF16)],
        compiler_params=_cparams(("parallel", "arbitrary")),
        name="peer_dense",
    )(hT, r2, e2, cnt, e1, u, vT)


ROUTE_ROWS = 24
NEG_INF = float("-inf")
ROUTE_UNROLL = 4


def _top_rows(s, n):
    tc = s.shape[1]
    rid = lax.broadcasted_iota(jnp.int32, (ROUTE_ROWS, tc), 0)
    packed = jnp.full((ROUTE_ROWS, tc), NEG_INF, F32)
    rank = jnp.full(s.shape, float(n), F32)
    for k in range(n):
        m = jnp.max(s, axis=0, keepdims=True)
        packed = jnp.where(rid == k, m, packed)
        hit = s == m
        rank = jnp.where(hit, float(k), rank)
        s = jnp.where(hit, NEG_INF, s)
    return packed, rank


def _row_penalty(tc, lo, hi):
    rid = lax.broadcasted_iota(jnp.int32, (8, tc), 0)
    return jnp.where((rid >= lo) & (rid < hi), 0.0, NEG_INF).astype(F32)


def _route_kernel(hT_ref, wqh_ref, wql_ref, sk_ref, n_ref, e1_ref, r2_ref, e2_ref, q_scr):
    hT = hT_ref[...]
    q_scr[...] = (jnp.dot(wqh_ref[...], hT, preferred_element_type=F32)
                  + jnp.dot(wql_ref[...], hT, preferred_element_type=F32))
    tc = hT.shape[1]
    n = PEER_TOPK + 1
    half = PEER_DK // 2

    def head(h, carry):
        r0 = pl.multiple_of(h * PEER_DK, PEER_DK)
        hp = lax.Precision.HIGHEST
        s1 = jnp.dot(sk_ref[h, 0], q_scr[pl.ds(r0, half), :], preferred_element_type=F32, precision=hp)
        s2 = jnp.dot(sk_ref[h, 1], q_scr[pl.ds(r0 + half, half), :], preferred_element_type=F32, precision=hp)
        v1, _ = _top_rows(s1, n)
        v2, rank = _top_rows(s2, n)
        tiles = [v1[0:1] + v2[0:8], v1[0:1] + v2[8:16], v1[0:1] + v2[16:24], v1[1:2] + v2[0:8],
                 v1[2:3] + v2[0:8] + _row_penalty(tc, 0, n // 3), v1[3:4] + v2[0:8] + _row_penalty(tc, 0, n // 4),
                 v2[0:1] + v1[0:8] + _row_penalty(tc, 4, 8), v2[0:1] + v1[8:16], v2[0:1] + v1[16:24],
                 v2[1:2] + v1[0:8] + _row_penalty(tc, 4, n // 2), v2[2:3] + v1[0:8] + _row_penalty(tc, 4, n // 3)]
        top, _ = _top_rows(jnp.concatenate(tiles, axis=0), n)
        theta = 0.5 * (top[PEER_TOPK - 1:PEER_TOPK] + top[PEER_TOPK:PEER_TOPK + 1])
        z = jnp.sum(jnp.exp(top[0:16] - top[0:1]), axis=0, keepdims=True)
        c = theta - s1
        cnt = jnp.zeros(c.shape, F32)
        for b in range(n):
            cnt = jnp.where(v2[b:b + 1] >= c, float(b + 1), cnt)
        n_ref[h] = cnt
        e1_ref[h] = jnp.exp(s1 - v1[0:1]) / z
        r2_ref[h] = pltpu.bitcast(rank.astype(BF16), jnp.uint32)
        e2_ref[h] = pltpu.bitcast(jnp.exp(s2 - v2[0:1]).astype(BF16), jnp.uint32)
        return carry

    def heads(i, c):
        for k in range(ROUTE_UNROLL):
            c = head(ROUTE_UNROLL * i + k, c)
        return c

    lax.fori_loop(0, PEER_HEADS // ROUTE_UNROLL, heads, 0)


def _peer_route(hT, wqT_hi, wqT_lo, subkeys, tc):
    D, T = hT.shape
    W = wqT_hi.shape[0]
    out = jax.ShapeDtypeStruct((PEER_HEADS, PEER_NKEYS, T), F32)
    out16 = jax.ShapeDtypeStruct((PEER_HEADS, PEER_NKEYS // 2, T), jnp.uint32)
    ospec = pl.BlockSpec((PEER_HEADS, PEER_NKEYS, tc), lambda t: (0, 0, t))
    ospec16 = pl.BlockSpec((PEER_HEADS, PEER_NKEYS // 2, tc), lambda t: (0, 0, t))
    return pl.pallas_call(
        _route_kernel,
        grid=(T // tc,),
        in_specs=[
            pl.BlockSpec((D, tc), lambda t: (0, t)),
            pl.BlockSpec((W, D), lambda t: (0, 0)),
            pl.BlockSpec((W, D), lambda t: (0, 0)),
            pl.BlockSpec(subkeys.shape, lambda t: (0, 0, 0, 0)),
        ],
        out_specs=[ospec, ospec, ospec16, ospec16],
        out_shape=[out, out, out16, out16],
        scratch_shapes=[pltpu.VMEM((W, tc), F32)],
        compiler_params=_cparams(("parallel",)),
        name="peer_route",
    )(hT, wqT_hi, wqT_lo, subkeys)


def _rope_tables(L, rope):
    if not rope:
        return jnp.ones((L, 2 * HEAD_DIM), F32), jnp.zeros((L, 2 * HEAD_DIM), F32)
    pos = jnp.arange(L, dtype=jnp.int32)
    row = (pos // GRID_W).astype(F32)
    col = (pos % GRID_W).astype(F32)
    inv = jnp.power(ROPE_THETA, -jnp.arange(0, AXIS_DIM, 2, dtype=F32) / AXIS_DIM)
    ar = row[:, None] * inv[None, :]
    ac = col[:, None] * inv[None, :]
    cos = jnp.concatenate([jnp.cos(ar), jnp.cos(ar), jnp.cos(ac), jnp.cos(ac)], axis=-1)
    sin = jnp.concatenate([-jnp.sin(ar), jnp.sin(ar), -jnp.sin(ac), jnp.sin(ac)], axis=-1)
    return jnp.tile(cos, (1, 2)), jnp.tile(sin, (1, 2))


def _qk_prep_kernel(p_ref, cos_ref, sin_ref, qg_ref, kg_ref, bdq_ref, bdk_ref, q_ref, k_ref):
    x = p_ref[...].astype(F32)
    cos = cos_ref[...]
    sin = sin_ref[...]
    quarter = AXIS_DIM // 2

    def prep(xh, gain, bd):
        w = xh.shape[1]
        sq = xh * xh
        hi = sq.astype(BF16)
        lo = (sq - hi.astype(F32)).astype(BF16)
        ms = jnp.dot(hi, bd, preferred_element_type=F32) + jnp.dot(lo, bd, preferred_element_type=F32)
        y = xh * lax.rsqrt(ms + EPS) * gain
        lane = lax.broadcasted_iota(jnp.int32, y.shape, 1)
        first = (lane & (AXIS_DIM - 1)) < quarter
        partner = jnp.where(first, pltpu.roll(y, w - quarter, 1), pltpu.roll(y, quarter, 1))
        reps = w // cos.shape[1]
        c = jnp.concatenate([cos] * reps, axis=1) if reps > 1 else cos
        s = jnp.concatenate([sin] * reps, axis=1) if reps > 1 else sin
        return y * c + partner * s

    q_ref[...] = (prep(x[:, :ATT_W], qg_ref[...], bdq_ref[...]) * ATT_SCALE).astype(BF16)
    k_ref[...] = prep(x[:, ATT_W:ATT_W + KV_W], kg_ref[...], bdk_ref[...]).astype(BF16)


def _qk_prep(p, cos, sin, q_gain, k_gain, rows_per_batch, tm):
    T = p.shape[0]
    tpb = rows_per_batch // tm
    qg = jnp.tile(q_gain, N_HEADS)[None, :]
    kg = jnp.tile(k_gain, N_KV_HEADS)[None, :]

    def block_avg(w):
        hid = jnp.arange(w) // HEAD_DIM
        return jnp.where(hid[:, None] == hid[None, :], 1.0 / HEAD_DIM, 0.0).astype(BF16)

    full = lambda i: (0, 0)
    return pl.pallas_call(
        _qk_prep_kernel,
        grid=(T // tm,),
        in_specs=[
            pl.BlockSpec((tm, QKV_W), lambda i: (i, Q0 // QKV_W)),
            pl.BlockSpec((tm, 2 * HEAD_DIM), lambda i: (i % tpb, 0)),
            pl.BlockSpec((tm, 2 * HEAD_DIM), lambda i: (i % tpb, 0)),
            pl.BlockSpec((1, ATT_W), full),
            pl.BlockSpec((1, KV_W), full),
            pl.BlockSpec((ATT_W, ATT_W), full),
            pl.BlockSpec((KV_W, KV_W), full),
        ],
        out_specs=[pl.BlockSpec((tm, ATT_W), lambda i: (i, 0)), pl.BlockSpec((tm, KV_W), lambda i: (i, 0))],
        out_shape=[jax.ShapeDtypeStruct((T, ATT_W), BF16), jax.ShapeDtypeStruct((T, KV_W), BF16)],
        compiler_params=_cparams(("parallel",)),
        name="qk_prep",
    )(p, cos, sin, qg, kg, block_avg(ATT_W), block_avg(KV_W))


def _shift_rows(cur, prev8, next8, first, last):
    tm = cur.shape[0]
    rid = lax.broadcasted_iota(jnp.int32, cur.shape, 0)
    pr = jnp.where(first, 0.0, prev8[HALO - 1:HALO, :])
    nx = jnp.where(last, 0.0, next8[0:1, :])
    up = jnp.where(rid == 0, pr, pltpu.roll(cur, 1, 0))
    dn = jnp.where(rid == tm - 1, nx, pltpu.roll(cur, tm - 1, 0))
    return up, dn


def _halo_specs(tm, width, col_block, n_rows):
    r = tm // HALO
    last = n_rows // HALO - 1
    prev = pl.BlockSpec((1, HALO, width), lambda b, i: (b, jnp.maximum(i * r - 1, 0), col_block))
    nxt = pl.BlockSpec((1, HALO, width), lambda b, i: (b, jnp.minimum((i + 1) * r, last), col_block))
    return prev, nxt


def _short_conv_kernel(bg_ref, cg_ref, xs_ref, cgp_ref, xsp_ref, cgn_ref, xsn_ref, w_ref, o_ref):
    i = pl.program_id(1)
    f = lambda r: r[0].astype(F32)
    cur = f(cg_ref) * f(xs_ref)
    up, dn = _shift_rows(cur, f(cgp_ref) * f(xsp_ref), f(cgn_ref) * f(xsn_ref), i == 0, i == pl.num_programs(1) - 1)
    o_ref[0] = (f(bg_ref) * (up * w_ref[0:1] + cur * w_ref[1:2] + dn * w_ref[2:3])).astype(o_ref.dtype)


def _short_conv(p3, w, tm):
    B, L, _ = p3.shape
    c0 = SC0 // SC_W
    blk = lambda j: pl.BlockSpec((1, tm, SC_W), lambda b, i: (b, i, c0 + j))
    cgp, cgn = _halo_specs(tm, SC_W, c0 + 1, L)
    xsp, xsn = _halo_specs(tm, SC_W, c0 + 2, L)
    return pl.pallas_call(
        _short_conv_kernel,
        grid=(B, L // tm),
        in_specs=[blk(0), blk(1), blk(2), cgp, xsp, cgn, xsn, pl.BlockSpec((3, SC_W), lambda b, i: (0, 0))],
        out_specs=pl.BlockSpec((1, tm, SC_W), lambda b, i: (b, i, 0)),
        out_shape=jax.ShapeDtypeStruct((B, L, SC_W), BF16),
        compiler_params=_cparams(("parallel", "parallel")),
        name="short_conv",
    )(p3, p3, p3, p3, p3, p3, p3, w)


def _hy_pre_kernel(p_ref, pp_ref, pn_ref, w_ref, b_ref, v_ref, x1_ref, x2_ref):
    i = pl.program_id(1)
    cur = p_ref[0].astype(F32)
    up, dn = _shift_rows(cur, pp_ref[0].astype(F32), pn_ref[0].astype(F32), i == 0, i == pl.num_programs(1) - 1)
    u = up * w_ref[0:1] + cur * w_ref[1:2] + dn * w_ref[2:3] + b_ref[...]
    v_ref[0] = u[:, :HY_W].astype(BF16)
    x1_ref[0] = u[:, HY_W:2 * HY_W].astype(BF16)
    x2_ref[0] = u[:, 2 * HY_W:].astype(BF16)


def _hy_pre(p3, w, b, tm):
    B, L, _ = p3.shape
    W = (HY_ORDER + 1) * HY_W
    c0 = HY0 // W
    prev, nxt = _halo_specs(tm, W, c0, L)
    out = jax.ShapeDtypeStruct((B, L, HY_W), BF16)
    ospec = pl.BlockSpec((1, tm, HY_W), lambda b_, i: (b_, i, 0))
    return pl.pallas_call(
        _hy_pre_kernel,
        grid=(B, L // tm),
        in_specs=[pl.BlockSpec((1, tm, W), lambda b_, i: (b_, i, c0)), prev, nxt,
                  pl.BlockSpec((3, W), lambda b_, i: (0, 0)), pl.BlockSpec((1, W), lambda b_, i: (0, 0))],
        out_specs=[ospec, ospec, ospec],
        out_shape=[out, out, out],
        compiler_params=_cparams(("parallel", "parallel")),
        name="hyena_pre",
    )(p3, p3, p3, w, b[None, :])


def _hy_features(L):
    n = jnp.arange(2 * L, dtype=jnp.int32)
    j = jnp.where(n < L, n, jnp.where(n == L, 0, 2 * L - n)).astype(F32)[:, None]
    t = j / (L - 1)
    w = 2.0 * math.pi * j / L
    bands = jnp.linspace(1e-4, HY_BANDS - 1, HY_BANDS, dtype=F32)[None, :]
    z = jnp.concatenate([t, jnp.cos(bands * w), -jnp.sin(bands * w)], axis=-1)
    return jnp.pad(z, ((0, 0), (0, HY_FEAT - z.shape[1])))


def _hy_filter_kernel(z_ref, w1_ref, b1_ref, f1_ref, w2_ref, b2_ref, f2_ref, w3_ref, dec_ref,
                      k0_ref, k1_ref, sum_ref):
    i = pl.program_id(0)
    half = pl.num_programs(0) // 2

    @pl.when(i == 0)
    def _():
        sum_ref[...] = jnp.zeros(sum_ref.shape, F32)

    hp = lax.Precision.HIGHEST
    z = z_ref[...]
    h = jnp.sin(f1_ref[...] * (jnp.dot(z, w1_ref[...], preferred_element_type=F32, precision=hp) + b1_ref[...]))
    h = jnp.sin(f2_ref[...] * (jnp.dot(h, w2_ref[...], preferred_element_type=F32, precision=hp) + b2_ref[...]))
    k = jnp.dot(h, w3_ref[0], preferred_element_type=F32, precision=hp)
    k = k * jnp.exp(-z[:, 0:1] * dec_ref[...])
    sum_ref[...] += jnp.sum(jnp.abs(k), axis=0, keepdims=True)
    rid = lax.broadcasted_iota(jnp.int32, k.shape, 0)
    k = jnp.where(i == half, jnp.where(rid == 0, 0.0, k), k)
    k0_ref[...] = k[:, :HY_W].astype(BF16)
    k1_ref[...] = k[:, HY_W:].astype(BF16)


def _hy_filter(L, w1, b1, f1, w2, b2, f2, w3, decay, nb):
    N = 2 * L
    fh = w1.shape[1]
    z = _hy_features(L)
    w1p = jnp.pad(w1, ((0, HY_FEAT - w1.shape[0]), (0, 0)))
    w3d = w3.reshape(fh, HY_ORDER, 2, HY_W).transpose(2, 0, 1, 3).reshape(2, fh, HY_ORDER * HY_W)
    dec = jnp.abs(decay).reshape(1, HY_ORDER * HY_W)
    row = lambda a: a[None, :]
    full = lambda i: (0, 0)
    half = N // nb // 2
    k0, k1, tot = pl.pallas_call(
        _hy_filter_kernel,
        grid=(N // nb,),
        in_specs=[
            pl.BlockSpec((nb, HY_FEAT), lambda i: (i, 0)),
            pl.BlockSpec((HY_FEAT, fh), full), pl.BlockSpec((1, fh), full), pl.BlockSpec((1, fh), full),
            pl.BlockSpec((fh, fh), full), pl.BlockSpec((1, fh), full), pl.BlockSpec((1, fh), full),
            pl.BlockSpec((1, fh, HY_ORDER * HY_W), lambda i: (i // half, 0, 0)),
            pl.BlockSpec((1, HY_ORDER * HY_W), full),
        ],
        out_specs=[pl.BlockSpec((nb, HY_W), lambda i: (i, 0)), pl.BlockSpec((nb, HY_W), lambda i: (i, 0)),
                   pl.BlockSpec((1, HY_ORDER * HY_W), full)],
        out_shape=[jax.ShapeDtypeStruct((N, HY_W), BF16), jax.ShapeDtypeStruct((N, HY_W), BF16),
                   jax.ShapeDtypeStruct((1, HY_ORDER * HY_W), F32)],
        compiler_params=_cparams(("arbitrary",)),
        name="hyena_filter",
    )(z, w1p, row(b1), row(f1), w2, row(b2), row(f2), w3d, dec)
    return k0, k1, 1.0 / tot


def _dft_tables(N1, N2):
    N = N1 * N2
    ar = lambda n: jnp.arange(n, dtype=jnp.int32)

    def cs(m, period):
        ang = (-2.0 * math.pi / period) * (m % period).astype(F32)
        return jnp.cos(ang), jnp.sin(ang)

    fr, fi = cs(ar(N1)[:, None] * ar(N1)[None, :], N1)
    hr, hi = fr[:, :max(N1 // 2, 1)], fi[:, :max(N1 // 2, 1)]
    w_fwd = jnp.block([[hr, -hi], [hi, hr]])
    w_real = jnp.concatenate([fr, fi], axis=0)
    w_inv = jnp.block([[hr.T, hi.T], [-hi.T, hr.T]]) / N1
    f2r, f2i = cs(ar(N2)[:, None] * ar(N2)[None, :], N2)
    tr, ti = cs(ar(N1)[:, None] * ar(N2)[None, :], N)
    gr = f2r[None] * tr[:, None, :] - f2i[None] * ti[:, None, :]
    gi = f2r[None] * ti[:, None, :] + f2i[None] * tr[:, None, :]
    g = jnp.concatenate([jnp.concatenate([gr, -gi], axis=2), jnp.concatenate([gi, gr], axis=2)], axis=1)
    grt, git = gr.transpose(0, 2, 1), gi.transpose(0, 2, 1)
    gh = jnp.concatenate([jnp.concatenate([grt, git], axis=2), jnp.concatenate([-git, grt], axis=2)], axis=1) / N2
    return tuple(a.astype(BF16) for a in (w_fwd, w_real, w_inv, g, gh))


def _colmm_kernel(w_ref, x_ref, o_ref):
    o_ref[...] = jnp.dot(w_ref[...], x_ref[...], preferred_element_type=F32).astype(o_ref.dtype)


def _colmm(w, x, cb):
    M, K = w.shape
    C = x.shape[1]
    return pl.pallas_call(
        _colmm_kernel,
        grid=(C // cb,),
        in_specs=[pl.BlockSpec((M, K), lambda j: (0, 0)), pl.BlockSpec((K, cb), lambda j: (0, j))],
        out_specs=pl.BlockSpec((M, cb), lambda j: (0, j)),
        out_shape=jax.ShapeDtypeStruct((M, C), BF16),
        compiler_params=_cparams(("parallel",)),
        name="dft_outer",
    )(w, x)


def _colmm_gate_kernel(w_ref, d_ref, u_ref, g_ref, b_ref, o_ref):
    y = jnp.dot(w_ref[...], d_ref[...], preferred_element_type=F32)
    o_ref[...] = (g_ref[...].astype(F32) * (y + u_ref[...].astype(F32) * b_ref[...])).astype(o_ref.dtype)


def _colmm_gate(w, d, u, gate, bias_row, cb):
    M, K = w.shape
    C = d.shape[1]
    col = lambda j: (0, j)
    return pl.pallas_call(
        _colmm_gate_kernel,
        grid=(C // cb,),
        in_specs=[pl.BlockSpec((M, K), lambda j: (0, 0)), pl.BlockSpec((K, cb), col), pl.BlockSpec((M, cb), col),
                  pl.BlockSpec((M, cb), col), pl.BlockSpec((1, cb), lambda j: (0, 0))],
        out_specs=pl.BlockSpec((M, cb), col),
        out_shape=jax.ShapeDtypeStruct((M, C), BF16),
        compiler_params=_cparams(("parallel",)),
        name="dft_outer_gate",
    )(w, d, u, gate, bias_row)


def _mid_fwd_kernel(a_ref, g_ref, s_ref, k_ref):
    n2 = a_ref.shape[2]
    a = a_ref[:, 0].reshape(2 * n2, a_ref.shape[3])
    x = jnp.dot(g_ref[0], a, preferred_element_type=F32) * s_ref[...]
    k_ref[:, 0] = x.reshape(2, n2, x.shape[1])


def _mid_fwd(a, g, scale):
    _, N1, N2, C = a.shape
    blk = pl.BlockSpec((2, 1, N2, C), lambda i: (0, i, 0, 0))
    return pl.pallas_call(
        _mid_fwd_kernel,
        grid=(N1,),
        in_specs=[blk, pl.BlockSpec((1, 2 * N2, 2 * N2), lambda i: (i, 0, 0)), pl.BlockSpec((1, C), lambda i: (0, 0))],
        out_specs=blk,
        out_shape=jax.ShapeDtypeStruct((2, N1, N2, C), F32),
        compiler_params=_cparams(("parallel",)),
        name="dft_inner_filter",
    )(a, g, scale)


def _mid_kernel(a_ref, g_ref, gh_ref, k_ref, d_ref):
    n2 = a_ref.shape[2]
    c = a_ref.shape[3]
    a = a_ref[:, 0].reshape(2 * n2, c)
    x = jnp.dot(g_ref[0], a, preferred_element_type=F32)
    xr, xi = x[:n2], x[n2:]
    kr, ki = k_ref[0, 0], k_ref[1, 0]
    y = jnp.concatenate([xr * kr - xi * ki, xr * ki + xi * kr], axis=0).astype(BF16)
    d = jnp.dot(gh_ref[0], y, preferred_element_type=F32)
    d_ref[:, 0] = d.reshape(2, n2, c).astype(d_ref.dtype)


def _mid(a, g, gh, kf):
    _, N1, N2, C = a.shape
    blk = pl.BlockSpec((2, 1, N2, C), lambda i: (0, i, 0, 0))
    tab = pl.BlockSpec((1, 2 * N2, 2 * N2), lambda i: (i, 0, 0))
    return pl.pallas_call(
        _mid_kernel,
        grid=(N1,),
        in_specs=[blk, tab, tab, blk],
        out_specs=blk,
        out_shape=jax.ShapeDtypeStruct((2, N1, N2, C), BF16),
        compiler_params=_cparams(("parallel",)),
        name="dft_inner",
    )(a, g, gh, kf)


def _hyena_long(v, x1, x2, filt, bias, tabs, cb):
    B, L, C = v.shape
    assert B == 2, "the two batches are packed as real / imaginary parts of one complex signal"
    w_fwd, w_real, w_inv, g, gh = tabs
    N1 = g.shape[0]
    N2 = g.shape[1] // 2
    k0, k1, inv_norm = filt
    flat = lambda a: a.reshape(-1, N2 * C)
    z = flat(v)
    for o, (ker, gate) in enumerate(((k0, x1), (k1, x2))):
        if N1 > 1:
            ka = _colmm(w_real, flat(ker), cb).reshape(2, N1, N2, C)
            za = _colmm(w_fwd, z, cb).reshape(2, N1, N2, C)
        else:
            ka = jnp.stack([ker, jnp.zeros_like(ker)]).reshape(2, 1, N2, C)
            za = jnp.pad(z.reshape(2, L, C), ((0, 0), (0, L), (0, 0))).reshape(2, 1, N2, C)
        kf = _mid_fwd(ka, g, inv_norm[:, o * C:(o + 1) * C])
        d = _mid(za, g, gh, kf)
        if N1 > 1:
            z = _colmm_gate(w_inv, flat(d), z, flat(gate), jnp.tile(bias[o], cb // C)[None, :], cb)
        else:
            y = d.reshape(2, N2, C)[:, :L].astype(F32)
            zf = z.reshape(2, L, C).astype(F32)
            z = (gate.astype(F32) * (y + zf * bias[o])).astype(BF16).reshape(-1, N2 * C // 2)
    return z.reshape(B, L, C)


def _resid_kernel(x_ref, fT_ref, gt_ref, g_ref, o_ref, *, final):
    xn = x_ref[...] + gt_ref[0] * fT_ref[...].T
    if final:
        xn = xn * lax.rsqrt(jnp.mean(xn * xn, axis=-1, keepdims=True) + EPS) * g_ref[...]
    o_ref[...] = xn


def _resid(x2d, fT, col0, gt, g, rows_per_batch, tm, final):
    T, D = x2d.shape
    tpb = rows_per_batch // tm
    c0 = col0 // tm
    row = lambda i: (i, 0)
    return pl.pallas_call(
        functools.partial(_resid_kernel, final=final),
        grid=(T // tm,),
        in_specs=[pl.BlockSpec((tm, D), row), pl.BlockSpec((D, tm), lambda i: (0, c0 + i)),
                  pl.BlockSpec((1, 1, D), lambda i: (i // tpb, 0, 0)), pl.BlockSpec((1, D), lambda i: (0, 0))],
        out_specs=pl.BlockSpec((tm, D), row),
        out_shape=jax.ShapeDtypeStruct((T, D), F32),
        compiler_params=_cparams(("parallel",)),
        name="residual",
    )(x2d, fT, gt, g)


def _peer(h2, wq, subkeys, u_bf, vT_bf, tc, nb):
    T, D = h2.shape
    hT = h2.T
    wqT = wq.T
    wqT_hi = wqT.astype(BF16)
    wqT_lo = (wqT - wqT_hi.astype(F32)).astype(BF16)
    cnt, e1, r2, e2 = _peer_route(hT, wqT_hi, wqT_lo, subkeys, tc)
    cnt4 = cnt.reshape(PEER_HEADS, PEER_NKEYS // nb, nb, T)
    e14 = e1.reshape(PEER_HEADS, PEER_NKEYS // nb, nb, T)
    return _peer_dense(hT, r2, e2, cnt4, e14, u_bf, vT_bf, tc, nb)


def kernel(x, c, ctx, c_ctx, w_mod, b_mod, g_norm1, g_norm2, w_in, q_gain, k_gain, sc_conv_w, hy_conv_w,
           hy_conv_b, hy_w1, hy_b1, hy_f1, hy_w2, hy_b2, hy_f2, hy_w3, hy_decay, hy_bias, w_br_att, w_br_sc,
           w_br_hy, w_out, peer_wq, peer_subkeys, peer_u, peer_v, g_final):
    B, S, D = x.shape
    Lc = ctx.shape[1]
    hp = lax.Precision.HIGHEST
    cond = jnp.concatenate([jax.nn.silu(c), jnp.broadcast_to(jax.nn.silu(c_ctx), (B, D))], axis=0)
    cos, sin = _rope_tables(S, True)
    cos_c, sin_c = _rope_tables(Lc, False)
    tm = min(1024, S)
    tq = min(512, S)
    tr = min(512, S)
    cb = 16 * HY_W
    tabs = _dft_tables(2 * S // HY_N2, HY_N2)
    tabs_c = _dft_tables(1, 2 * Lc)
    x2 = x.reshape(B * S, D)
    ctx2 = ctx.reshape(B * Lc, D)

    for l in range(DEPTH):
        need_ctx = l < DEPTH - 1
        mod = (jnp.dot(cond, w_mod[l], precision=hp) + b_mod[l]).reshape(2, B, 1, 6, D)
        sh1, sc1, gt1, sh2, sc2, gt2 = (mod[0, :, :, i] for i in range(6))
        csh1, csc1, cgt1, csh2, csc2, cgt2 = (mod[1, :, :, i] for i in range(6))
        w_in_bf = jnp.concatenate([w_in[l][:, REF_GT0:], w_in[l][:, REF_SC0:REF_GT0], w_in[l][:, :REF_SC0]],
                                  axis=1).astype(BF16)
        wa, ws, wh, wo = (w.astype(BF16) for w in (w_br_att[l], w_br_sc[l], w_br_hy[l], w_out[l]))
        g1 = g_norm1[l][None, :]
        g2 = g_norm2[l][None, :]
        hy_params = (hy_w1[l], hy_b1[l], hy_f1[l], hy_w2[l], hy_b2[l], hy_f2[l], hy_w3[l], hy_decay[l])

        p = _in_proj(x2, g1, sc1, sh1, w_in_bf, S, tm, 768)
        p3 = p.reshape(B, S, -1)
        pc = _in_proj(ctx2, g1, csc1, csh1, w_in_bf, Lc, Lc, 768)
        pc3 = pc.reshape(B, Lc, -1)

        qs, k = _qk_prep(p, cos, sin, q_gain[l], k_gain[l], S, tr)
        qcs, kc = _qk_prep(pc, cos_c, sin_c, q_gain[l], k_gain[l], Lc, Lc)
        k_all = jnp.concatenate([kc.reshape(B, Lc, KV_W), k.reshape(B, S, KV_W)], axis=1)
        v_all = jnp.concatenate([pc3[..., V0:IN_W], p3[..., V0:IN_W]], axis=1)
        tk = (S + Lc) // 13 if (S + Lc) % (13 * 128) == 0 else 128
        y_att = _attention(qs.reshape(B, S, ATT_W), k_all, v_all, tq, tk).reshape(B * S, ATT_W)

        y_sc = _short_conv(p3, sc_conv_w[l], tr).reshape(B * S, SC_W)
        y_hy = _hyena_long(*_hy_pre(p3, hy_conv_w[l], hy_conv_b[l], tr), _hy_filter(S, *hy_params, tr),
                           hy_bias[l], tabs, cb).reshape(B * S, HY_W)
        x2, h2 = _merge(x2, y_att, y_sc, y_hy, p, gt1, g2, sc2, sh2, wa, ws, wh, wo, S, tr)

        if need_ctx:
            yc_att = _attention(qcs.reshape(B, Lc, ATT_W), kc.reshape(B, Lc, KV_W), pc3[..., V0:IN_W], Lc, Lc)
            yc_sc = _short_conv(pc3, sc_conv_w[l], Lc).reshape(B * Lc, SC_W)
            yc_hy = _hyena_long(*_hy_pre(pc3, hy_conv_w[l], hy_conv_b[l], Lc), _hy_filter(Lc, *hy_params, Lc),
                                hy_bias[l], tabs_c, cb).reshape(B * Lc, HY_W)
            ctx2, h2c = _merge(ctx2, yc_att.reshape(B * Lc, ATT_W), yc_sc, yc_hy, pc, cgt1, g2, csc2, csh2,
                               wa, ws, wh, wo, Lc, Lc)
            tok = jnp.concatenate([h2, h2c], axis=0)
        else:
            tok = h2

        u_bf = peer_u[l].astype(BF16)
        vT_bf = peer_v[l].astype(BF16).reshape(-1, PEER_BLOCK * PEER_NKEYS, D).transpose(0, 2, 1)
        fT = _peer(tok, peer_wq[l], peer_subkeys[l], u_bf, vT_bf, min(512, tok.shape[0]), PEER_BLOCK)
        last = l == DEPTH - 1
        x2 = _resid(x2, fT, 0, gt2, g_final[None, :], S, tr, last)
        if need_ctx:
            ctx2 = _resid(ctx2, fT, B * S, cgt2, g_final[None, :], Lc, Lc, False)

    return x2.reshape(B, S, D)
```

```python
import functools
import math

import jax
import jax.numpy as jnp
from jax import lax
from jax.experimental import pallas as pl
from jax.experimental.pallas import tpu as pltpu

F32 = jnp.float32
BF16 = jnp.bfloat16

DEPTH = 2
GRID_W = 64
EPS = 1e-6
N_HEADS = 8
N_KV_HEADS = 2
GQA_GROUP = N_HEADS // N_KV_HEADS
HEAD_DIM = 64
AXIS_DIM = HEAD_DIM // 2
ATT_W = N_HEADS * HEAD_DIM
KV_W = N_KV_HEADS * HEAD_DIM
ATT_SCALE = HEAD_DIM ** -0.5
ROPE_THETA = 10000.0
SC_W = 512
HY_W = 512
HY_ORDER = 2
HY_BANDS = 16
PEER_HEADS = 8
PEER_NKEYS = 128
PEER_DK = 128
PEER_TOPK = 16
D_MODEL = 1024
QKV_W = ATT_W + 2 * KV_W
REF_SC0 = QKV_W
REF_GT0 = QKV_W + 3 * SC_W + (HY_ORDER + 1) * HY_W
GT0 = 0
SC0 = GT0 + 3 * D_MODEL
HY0 = SC0 + 3 * SC_W
Q0 = HY0 + (HY_ORDER + 1) * HY_W
K0 = Q0 + ATT_W
V0 = K0 + KV_W
IN_W = V0 + KV_W
HALO = 8
HY_N2 = 256
HY_FEAT = 128

VMEM_LIMIT = 56 * 1024 * 1024


def _cparams(sem):
    return pltpu.CompilerParams(dimension_semantics=sem, vmem_limit_bytes=VMEM_LIMIT)


def _in_proj_kernel(x_ref, g_ref, sc_ref, sh_ref, w_ref, o_ref, h_scr):
    @pl.when(pl.program_id(1) == 0)
    def _():
        x = x_ref[...]
        ms = jnp.mean(x * x, axis=-1, keepdims=True)
        y = x * lax.rsqrt(ms + EPS) * g_ref[...]
        h_scr[...] = (y * (1.0 + sc_ref[0]) + sh_ref[0]).astype(BF16)

    o_ref[...] = jnp.dot(h_scr[...], w_ref[...], preferred_element_type=F32).astype(o_ref.dtype)


def _in_proj(x2d, g, sc, sh, w, rows_per_batch, tm, tn):
    T, D = x2d.shape
    N = w.shape[1]
    tpb = rows_per_batch // tm
    return pl.pallas_call(
        _in_proj_kernel,
        grid=(T // tm, N // tn),
        in_specs=[
            pl.BlockSpec((tm, D), lambda i, j: (i, 0)),
            pl.BlockSpec((1, D), lambda i, j: (0, 0)),
            pl.BlockSpec((1, 1, D), lambda i, j: (i // tpb, 0, 0)),
            pl.BlockSpec((1, 1, D), lambda i, j: (i // tpb, 0, 0)),
            pl.BlockSpec((D, tn), lambda i, j: (0, j)),
        ],
        out_specs=pl.BlockSpec((tm, tn), lambda i, j: (i, j)),
        out_shape=jax.ShapeDtypeStruct((T, N), BF16),
        scratch_shapes=[pltpu.VMEM((tm, D), BF16)],
        compiler_params=_cparams(("parallel", "arbitrary")),
        name="in_proj",
    )(x2d, g, sc, sh, w)


def _attn_kernel(q_ref, kT_ref, v_ref, o_ref, q_scr, m_scr, acc_scr, sa_scr, sb_scr, *, nkb):
    tq = q_ref.shape[1]
    m_scr[...] = jnp.full(m_scr.shape, -jnp.inf, F32)
    acc_scr[...] = jnp.zeros(acc_scr.shape, F32)
    for g in range(GQA_GROUP):
        q_scr[g * tq:(g + 1) * tq, :] = q_ref[0, :, g * HEAD_DIM:(g + 1) * HEAD_DIM]

    def scores(j, dst_scr):
        dst_scr[...] = jnp.dot(q_scr[...], kT_ref[0, 0, j], preferred_element_type=F32)

    def update(j, src_scr):
        s = src_scr[...]
        m_prev = m_scr[...]
        m_new = jnp.maximum(m_prev, jnp.max(s, axis=-1, keepdims=True))
        p = jnp.exp(s - m_new).astype(BF16)
        alpha = jnp.exp(m_prev - m_new)
        acc_scr[...] = acc_scr[...] * alpha + jnp.dot(p, v_ref[0, 0, j], preferred_element_type=F32)
        m_scr[...] = m_new

    scores(0, sa_scr)

    def pair(i, carry):
        j = 2 * i
        scores(j + 1, sb_scr)
        update(j, sa_scr)
        scores(j + 2, sa_scr)
        update(j + 1, sb_scr)
        return carry

    lax.fori_loop(0, (nkb - 1) // 2, pair, 0)
    if nkb % 2 == 1:
        update(nkb - 1, sa_scr)
    else:
        scores(nkb - 1, sb_scr)
        update(nkb - 2, sa_scr)
        update(nkb - 1, sb_scr)
    acc = acc_scr[...]
    o = (acc[:, :HEAD_DIM] / acc[:, HEAD_DIM:HEAD_DIM + 1]).astype(o_ref.dtype)
    o_ref[0] = jnp.concatenate([o[g * tq:(g + 1) * tq] for g in range(GQA_GROUP)], axis=1)


def _attention(q, k, v, tq, tk):
    B, Lq, _ = q.shape
    Lk = k.shape[1]
    nqb, nkb = Lq // tq, Lk // tk
    R = GQA_GROUP * tq
    GW = GQA_GROUP * HEAD_DIM
    kT = k.reshape(B, nkb, tk, N_KV_HEADS, HEAD_DIM).transpose(0, 3, 1, 4, 2)
    vb = v.reshape(B, nkb, tk, N_KV_HEADS, HEAD_DIM).transpose(0, 3, 1, 2, 4)
    ones = jnp.ones(vb.shape[:-1] + (1,), BF16)
    zeros = jnp.zeros(vb.shape[:-1] + (HEAD_DIM - 1,), BF16)
    vb = jnp.concatenate([vb, ones, zeros], axis=-1)
    return pl.pallas_call(
        functools.partial(_attn_kernel, nkb=nkb),
        grid=(B, N_KV_HEADS, nqb),
        in_specs=[
            pl.BlockSpec((1, tq, GW), lambda b, h, i: (b, i, h)),
            pl.BlockSpec((1, 1, nkb, HEAD_DIM, tk), lambda b, h, i: (b, h, 0, 0, 0)),
            pl.BlockSpec((1, 1, nkb, tk, 2 * HEAD_DIM), lambda b, h, i: (b, h, 0, 0, 0)),
        ],
        out_specs=pl.BlockSpec((1, tq, GW), lambda b, h, i: (b, i, h)),
        out_shape=jax.ShapeDtypeStruct((B, Lq, ATT_W), BF16),
        scratch_shapes=[pltpu.VMEM((R, HEAD_DIM), BF16), pltpu.VMEM((R, 1), F32), pltpu.VMEM((R, 2 * HEAD_DIM), F32),
                        pltpu.VMEM((R, tk), F32), pltpu.VMEM((R, tk), F32)],
        compiler_params=_cparams(("parallel", "parallel", "arbitrary")),
        name="attention",
    )(q, kT, vb)


def _merge_kernel(x_ref, ya_ref, ys_ref, yh_ref, ga_ref, gs_ref, gh_ref, gt_ref, g2_ref, sc_ref, sh_ref,
                  wa_ref, ws_ref, wh_ref, wo_ref, xo_ref, h2_ref):
    def br(y_ref, g_ref, w_ref):
        gate = jax.nn.sigmoid(g_ref[...].astype(F32))
        return gate * jnp.dot(y_ref[...], w_ref[...], preferred_element_type=F32)

    m = br(ya_ref, ga_ref, wa_ref) + br(ys_ref, gs_ref, ws_ref) + br(yh_ref, gh_ref, wh_ref)
    o = jnp.dot(m.astype(BF16), wo_ref[...], preferred_element_type=F32)
    xn = x_ref[...] + gt_ref[0] * o
    xo_ref[...] = xn
    ms = jnp.mean(xn * xn, axis=-1, keepdims=True)
    y = xn * lax.rsqrt(ms + EPS) * g2_ref[...]
    h2_ref[...] = (y * (1.0 + sc_ref[0]) + sh_ref[0]).astype(BF16)


def _merge(x2d, ya, ys, yh, p, gt1, g2, sc2, sh2, wa, ws, wh, wo, rows_per_batch, tm):
    T, D = x2d.shape
    tpb = rows_per_batch // tm
    gblk = GT0 // D
    row = lambda i: (i, 0)
    mod = lambda i: (i // tpb, 0, 0)
    full = lambda i: (0, 0)
    return pl.pallas_call(
        _merge_kernel,
        grid=(T // tm,),
        in_specs=[
            pl.BlockSpec((tm, D), row),
            pl.BlockSpec((tm, ATT_W), row),
            pl.BlockSpec((tm, SC_W), row),
            pl.BlockSpec((tm, HY_W), row),
            pl.BlockSpec((tm, D), lambda i: (i, gblk)),
            pl.BlockSpec((tm, D), lambda i: (i, gblk + 1)),
            pl.BlockSpec((tm, D), lambda i: (i, gblk + 2)),
            pl.BlockSpec((1, 1, D), mod),
            pl.BlockSpec((1, D), full),
            pl.BlockSpec((1, 1, D), mod),
            pl.BlockSpec((1, 1, D), mod),
            pl.BlockSpec((ATT_W, D), full),
            pl.BlockSpec((SC_W, D), full),
            pl.BlockSpec((HY_W, D), full),
            pl.BlockSpec((D, D), full),
        ],
        out_specs=[pl.BlockSpec((tm, D), row), pl.BlockSpec((tm, D), row)],
        out_shape=[jax.ShapeDtypeStruct((T, D), F32), jax.ShapeDtypeStruct((T, D), BF16)],
        compiler_params=_cparams(("parallel",)),
        name="merge",
    )(x2d, ya, ys, yh, p, p, p, gt1, g2, sc2, sh2, wa, ws, wh, wo)


LANES = 128
PACKED_ROWS = 16
PEER_BLOCK = 8
PEER_MM = 4
PEER_ACC = 2


def _peer_kernel(hT_ref, r2_ref, e2_ref, n_ref, e1_ref, u_ref, vT_ref, o_ref, gwa_scr, gwb_scr, act_scr, *, nb):
    e = pl.program_id(1)
    tc = hT_ref.shape[1]

    @pl.when(e == 0)
    def _():
        o_ref[...] = jnp.zeros(o_ref.shape, F32)
        gwb_scr[...] = jnp.zeros(gwb_scr.shape, BF16)

    def step(cur_scr, prev_scr):
        blk = jnp.minimum(e, pl.num_programs(1) - 2)
        shape3 = (PEER_NKEYS // PACKED_ROWS, PACKED_ROWS, LANES)
        for g0 in range(0, nb, PEER_MM):
            grows = slice(g0 * PEER_NKEYS, (g0 + PEER_MM) * PEER_NKEYS)
            o_ref[...] += jnp.dot(vT_ref[0, :, grows], prev_scr[grows, :], preferred_element_type=F32)
            for lt in range(tc // LANES):
                lanes = slice(lt * LANES, (lt + 1) * LANES)
                for i0 in range(g0, g0 + PEER_MM, PEER_ACC):
                    ws = [jnp.zeros(shape3, BF16) for _ in range(PEER_ACC)]
                    for h in range(PEER_HEADS):
                        r2t = pltpu.bitcast(r2_ref[h, :, lanes], BF16).reshape(shape3)
                        e2t = pltpu.bitcast(e2_ref[h, :, lanes], BF16).reshape(shape3)
                        for k in range(PEER_ACC):
                            ii = i0 + k
                            cnt = jnp.broadcast_to(n_ref[h, blk, ii:ii + 1, lanes], (PACKED_ROWS, LANES)).astype(BF16)
                            e1row = jnp.broadcast_to(e1_ref[h, blk, ii:ii + 1, lanes],
                                                     (PACKED_ROWS, LANES)).astype(BF16)
                            ws[k] = ws[k] + jnp.where(r2t < cnt[None], e2t, jnp.zeros_like(e2t)) * e1row[None]
                    for k in range(PEER_ACC):
                        rows = slice((i0 + k) * PEER_NKEYS, (i0 + k + 1) * PEER_NKEYS)
                        cur_scr[rows, lanes] = ws[k].reshape(PEER_NKEYS, LANES)
            a = jnp.dot(u_ref[grows, :], hT_ref[...], preferred_element_type=F32)
            act = (0.5 * a * (1.0 + lax.erf(a * (2.0 ** -0.5)))).astype(BF16)
            cur_scr[grows, :] = cur_scr[grows, :] * act

    @pl.when(e % 2 == 0)
    def _():
        step(gwa_scr, gwb_scr)

    @pl.when(e % 2 == 1)
    def _():
        step(gwb_scr, gwa_scr)


def _peer_dense(hT, r2, e2, cnt, e1, u, vT, tc, nb):
    D, T = hT.shape
    N = u.shape[0]
    eb = nb * PEER_NKEYS
    ne = N // eb
    assert ne % 2 == 0, "the drain step must find the last block in the buffer the parity rule reads"
    cur = lambda e: jnp.minimum(e, ne - 1)
    return pl.pallas_call(
        functools.partial(_peer_kernel, nb=nb),
        grid=(T // tc, ne + 1),
        in_specs=[
            pl.BlockSpec((D, tc), lambda t, e: (0, t)),
            pl.BlockSpec((PEER_HEADS, PEER_NKEYS // 2, tc), lambda t, e: (0, 0, t)),
            pl.BlockSpec((PEER_HEADS, PEER_NKEYS // 2, tc), lambda t, e: (0, 0, t)),
            pl.BlockSpec((PEER_HEADS, ne, nb, tc), lambda t, e: (0, 0, 0, t)),
            pl.BlockSpec((PEER_HEADS, ne, nb, tc), lambda t, e: (0, 0, 0, t)),
            pl.BlockSpec((eb, D), lambda t, e: (cur(e), 0)),
            pl.BlockSpec((1, D, eb), lambda t, e: (jnp.maximum(e - 1, 0), 0, 0)),
        ],
        out_specs=pl.BlockSpec((D, tc), lambda t, e: (0, t)),
        out_shape=jax.ShapeDtypeStruct((D, T), F32),
        scratch_shapes=[pltpu.VMEM((eb, tc), BF16), pltpu.VMEM((eb, tc), BF16), pltpu.VMEM((eb, tc), BF16)],
        compiler_params=_cparams(("parallel", "arbitrary")),
        name="peer_dense",
    )(hT, r2, e2, cnt, e1, u, vT)


ROUTE_ROWS = 24
NEG_INF = float("-inf")
ROUTE_UNROLL = 4


def _top_rows(s, n):
    tc = s.shape[1]
    rid = lax.broadcasted_iota(jnp.int32, (ROUTE_ROWS, tc), 0)
    packed = jnp.full((ROUTE_ROWS, tc), NEG_INF, F32)
    rank = jnp.full(s.shape, float(n), F32)
    for k in range(n):
        m = jnp.max(s, axis=0, keepdims=True)
        packed = jnp.where(rid == k, m, packed)
        hit = s == m
        rank = jnp.where(hit, float(k), rank)
        s = jnp.where(hit, NEG_INF, s)
    return packed, rank


def _row_penalty(tc, lo, hi):
    rid = lax.broadcasted_iota(jnp.int32, (8, tc), 0)
    return jnp.where((rid >= lo) & (rid < hi), 0.0, NEG_INF).astype(F32)


def _route_kernel(hT_ref, wqh_ref, wql_ref, sk_ref, n_ref, e1_ref, r2_ref, e2_ref, q_scr):
    hT = hT_ref[...]
    q_scr[...] = (jnp.dot(wqh_ref[...], hT, preferred_element_type=F32)
                  + jnp.dot(wql_ref[...], hT, preferred_element_type=F32))
    tc = hT.shape[1]
    n = PEER_TOPK + 1
    half = PEER_DK // 2

    def head(h, carry):
        r0 = pl.multiple_of(h * PEER_DK, PEER_DK)
        hp = lax.Precision.HIGHEST
        s1 = jnp.dot(sk_ref[h, 0], q_scr[pl.ds(r0, half), :], preferred_element_type=F32, precision=hp)
        s2 = jnp.dot(sk_ref[h, 1], q_scr[pl.ds(r0 + half, half), :], preferred_element_type=F32, precision=hp)
        v1, _ = _top_rows(s1, n)
        v2, rank = _top_rows(s2, n)
        tiles = [v1[0:1] + v2[0:8], v1[0:1] + v2[8:16], v1[0:1] + v2[16:24], v1[1:2] + v2[0:8],
                 v1[2:3] + v2[0:8] + _row_penalty(tc, 0, n // 3), v1[3:4] + v2[0:8] + _row_penalty(tc, 0, n // 4),
                 v2[0:1] + v1[0:8] + _row_penalty(tc, 4, 8), v2[0:1] + v1[8:16], v2[0:1] + v1[16:24],
                 v2[1:2] + v1[0:8] + _row_penalty(tc, 4, n // 2), v2[2:3] + v1[0:8] + _row_penalty(tc, 4, n // 3)]
        top, _ = _top_rows(jnp.concatenate(tiles, axis=0), n)
        theta = 0.5 * (top[PEER_TOPK - 1:PEER_TOPK] + top[PEER_TOPK:PEER_TOPK + 1])
        z = jnp.sum(jnp.exp(top[0:16] - top[0:1]), axis=0, keepdims=True)
        c = theta - s1
        cnt = jnp.zeros(c.shape, F32)
        for b in range(n):
            cnt = jnp.where(v2[b:b + 1] >= c, float(b + 1), cnt)
        n_ref[h] = cnt
        e1_ref[h] = jnp.exp(s1 - v1[0:1]) / z
        r2_ref[h] = pltpu.bitcast(rank.astype(BF16), jnp.uint32)
        e2_ref[h] = pltpu.bitcast(jnp.exp(s2 - v2[0:1]).astype(BF16), jnp.uint32)
        return carry

    def heads(i, c):
        for k in range(ROUTE_UNROLL):
            c = head(ROUTE_UNROLL * i + k, c)
        return c

    lax.fori_loop(0, PEER_HEADS // ROUTE_UNROLL, heads, 0)


def _peer_route(hT, wqT_hi, wqT_lo, subkeys, tc):
    D, T = hT.shape
    W = wqT_hi.shape[0]
    out = jax.ShapeDtypeStruct((PEER_HEADS, PEER_NKEYS, T), F32)
    out16 = jax.ShapeDtypeStruct((PEER_HEADS, PEER_NKEYS // 2, T), jnp.uint32)
    ospec = pl.BlockSpec((PEER_HEADS, PEER_NKEYS, tc), lambda t: (0, 0, t))
    ospec16 = pl.BlockSpec((PEER_HEADS, PEER_NKEYS // 2, tc), lambda t: (0, 0, t))
    return pl.pallas_call(
        _route_kernel,
        grid=(T // tc,),
        in_specs=[
            pl.BlockSpec((D, tc), lambda t: (0, t)),
            pl.BlockSpec((W, D), lambda t: (0, 0)),
            pl.BlockSpec((W, D), lambda t: (0, 0)),
            pl.BlockSpec(subkeys.shape, lambda t: (0, 0, 0, 0)),
        ],
        out_specs=[ospec, ospec, ospec16, ospec16],
        out_shape=[out, out, out16, out16],
        scratch_shapes=[pltpu.VMEM((W, tc), F32)],
        compiler_params=_cparams(("parallel",)),
        name="peer_route",
    )(hT, wqT_hi, wqT_lo, subkeys)


def _rope_tables(L, rope):
    if not rope:
        return jnp.ones((L, 2 * HEAD_DIM), F32), jnp.zeros((L, 2 * HEAD_DIM), F32)
    pos = jnp.arange(L, dtype=jnp.int32)
    row = (pos // GRID_W).astype(F32)
    col = (pos % GRID_W).astype(F32)
    inv = jnp.power(ROPE_THETA, -jnp.arange(0, AXIS_DIM, 2, dtype=F32) / AXIS_DIM)
    ar = row[:, None] * inv[None, :]
    ac = col[:, None] * inv[None, :]
    cos = jnp.concatenate([jnp.cos(ar), jnp.cos(ar), jnp.cos(ac), jnp.cos(ac)], axis=-1)
    sin = jnp.concatenate([-jnp.sin(ar), jnp.sin(ar), -jnp.sin(ac), jnp.sin(ac)], axis=-1)
    return jnp.tile(cos, (1, 2)), jnp.tile(sin, (1, 2))


def _qk_prep_kernel(p_ref, cos_ref, sin_ref, qg_ref, kg_ref, bdq_ref, bdk_ref, q_ref, k_ref):
    x = p_ref[...].astype(F32)
    cos = cos_ref[...]
    sin = sin_ref[...]
    quarter = AXIS_DIM // 2

    def prep(xh, gain, bd):
        w = xh.shape[1]
        sq = xh * xh
        hi = sq.astype(BF16)
        lo = (sq - hi.astype(F32)).astype(BF16)
        ms = jnp.dot(hi, bd, preferred_element_type=F32) + jnp.dot(lo, bd, preferred_element_type=F32)
        y = xh * lax.rsqrt(ms + EPS) * gain
        lane = lax.broadcasted_iota(jnp.int32, y.shape, 1)
        first = (lane & (AXIS_DIM - 1)) < quarter
        partner = jnp.where(first, pltpu.roll(y, w - quarter, 1), pltpu.roll(y, quarter, 1))
        reps = w // cos.shape[1]
        c = jnp.concatenate([cos] * reps, axis=1) if reps > 1 else cos
        s = jnp.concatenate([sin] * reps, axis=1) if reps > 1 else sin
        return y * c + partner * s

    q_ref[...] = (prep(x[:, :ATT_W], qg_ref[...], bdq_ref[...]) * ATT_SCALE).astype(BF16)
    k_ref[...] = prep(x[:, ATT_W:ATT_W + KV_W], kg_ref[...], bdk_ref[...]).astype(BF16)


def _qk_prep(p, cos, sin, q_gain, k_gain, rows_per_batch, tm):
    T = p.shape[0]
    tpb = rows_per_batch // tm
    qg = jnp.tile(q_gain, N_HEADS)[None, :]
    kg = jnp.tile(k_gain, N_KV_HEADS)[None, :]

    def block_avg(w):
        hid = jnp.arange(w) // HEAD_DIM
        return jnp.where(hid[:, None] == hid[None, :], 1.0 / HEAD_DIM, 0.0).astype(BF16)

    full = lambda i: (0, 0)
    return pl.pallas_call(
        _qk_prep_kernel,
        grid=(T // tm,),
        in_specs=[
            pl.BlockSpec((tm, QKV_W), lambda i: (i, Q0 // QKV_W)),
            pl.BlockSpec((tm, 2 * HEAD_DIM), lambda i: (i % tpb, 0)),
            pl.BlockSpec((tm, 2 * HEAD_DIM), lambda i: (i % tpb, 0)),
            pl.BlockSpec((1, ATT_W), full),
            pl.BlockSpec((1, KV_W), full),
            pl.BlockSpec((ATT_W, ATT_W), full),
            pl.BlockSpec((KV_W, KV_W), full),
        ],
        out_specs=[pl.BlockSpec((tm, ATT_W), lambda i: (i, 0)), pl.BlockSpec((tm, KV_W), lambda i: (i, 0))],
        out_shape=[jax.ShapeDtypeStruct((T, ATT_W), BF16), jax.ShapeDtypeStruct((T, KV_W), BF16)],
        compiler_params=_cparams(("parallel",)),
        name="qk_prep",
    )(p, cos, sin, qg, kg, block_avg(ATT_W), block_avg(KV_W))


def _shift_rows(cur, prev8, next8, first, last):
    tm = cur.shape[0]
    rid = lax.broadcasted_iota(jnp.int32, cur.shape, 0)
    pr = jnp.where(first, 0.0, prev8[HALO - 1:HALO, :])
    nx = jnp.where(last, 0.0, next8[0:1, :])
    up = jnp.where(rid == 0, pr, pltpu.roll(cur, 1, 0))
    dn = jnp.where(rid == tm - 1, nx, pltpu.roll(cur, tm - 1, 0))
    return up, dn


def _halo_specs(tm, width, col_block, n_rows):
    r = tm // HALO
    last = n_rows // HALO - 1
    prev = pl.BlockSpec((1, HALO, width), lambda b, i: (b, jnp.maximum(i * r - 1, 0), col_block))
    nxt = pl.BlockSpec((1, HALO, width), lambda b, i: (b, jnp.minimum((i + 1) * r, last), col_block))
    return prev, nxt


def _short_conv_kernel(bg_ref, cg_ref, xs_ref, cgp_ref, xsp_ref, cgn_ref, xsn_ref, w_ref, o_ref):
    i = pl.program_id(1)
    f = lambda r: r[0].astype(F32)
    cur = f(cg_ref) * f(xs_ref)
    up, dn = _shift_rows(cur, f(cgp_ref) * f(xsp_ref), f(cgn_ref) * f(xsn_ref), i == 0, i == pl.num_programs(1) - 1)
    o_ref[0] = (f(bg_ref) * (up * w_ref[0:1] + cur * w_ref[1:2] + dn * w_ref[2:3])).astype(o_ref.dtype)


def _short_conv(p3, w, tm):
    B, L, _ = p3.shape
    c0 = SC0 // SC_W
    blk = lambda j: pl.BlockSpec((1, tm, SC_W), lambda b, i: (b, i, c0 + j))
    cgp, cgn = _halo_specs(tm, SC_W, c0 + 1, L)
    xsp, xsn = _halo_specs(tm, SC_W, c0 + 2, L)
    return pl.pallas_call(
        _short_conv_kernel,
        grid=(B, L // tm),
        in_specs=[blk(0), blk(1), blk(2), cgp, xsp, cgn, xsn, pl.BlockSpec((3, SC_W), lambda b, i: (0, 0))],
        out_specs=pl.BlockSpec((1, tm, SC_W), lambda b, i: (b, i, 0)),
        out_shape=jax.ShapeDtypeStruct((B, L, SC_W), BF16),
        compiler_params=_cparams(("parallel", "parallel")),
        name="short_conv",
    )(p3, p3, p3, p3, p3, p3, p3, w)


def _hy_pre_kernel(p_ref, pp_ref, pn_ref, w_ref, b_ref, v_ref, x1_ref, x2_ref):
    i = pl.program_id(1)
    cur = p_ref[0].astype(F32)
    up, dn = _shift_rows(cur, pp_ref[0].astype(F32), pn_ref[0].astype(F32), i == 0, i == pl.num_programs(1) - 1)
    u = up * w_ref[0:1] + cur * w_ref[1:2] + dn * w_ref[2:3] + b_ref[...]
    v_ref[0] = u[:, :HY_W].astype(BF16)
    x1_ref[0] = u[:, HY_W:2 * HY_W].astype(BF16)
    x2_ref[0] = u[:, 2 * HY_W:].astype(BF16)


def _hy_pre(p3, w, b, tm):
    B, L, _ = p3.shape
    W = (HY_ORDER + 1) * HY_W
    c0 = HY0 // W
    prev, nxt = _halo_specs(tm, W, c0, L)
    out = jax.ShapeDtypeStruct((B, L, HY_W), BF16)
    ospec = pl.BlockSpec((1, tm, HY_W), lambda b_, i: (b_, i, 0))
    return pl.pallas_call(
        _hy_pre_kernel,
        grid=(B, L // tm),
        in_specs=[pl.BlockSpec((1, tm, W), lambda b_, i: (b_, i, c0)), prev, nxt,
                  pl.BlockSpec((3, W), lambda b_, i: (0, 0)), pl.BlockSpec((1, W), lambda b_, i: (0, 0))],
        out_specs=[ospec, ospec, ospec],
        out_shape=[out, out, out],
        compiler_params=_cparams(("parallel", "parallel")),
        name="hyena_pre",
    )(p3, p3, p3, w, b[None, :])


def _hy_features(L):
    n = jnp.arange(2 * L, dtype=jnp.int32)
    j = jnp.where(n < L, n, jnp.where(n == L, 0, 2 * L - n)).astype(F32)[:, None]
    t = j / (L - 1)
    w = 2.0 * math.pi * j / L
    bands = jnp.linspace(1e-4, HY_BANDS - 1, HY_BANDS, dtype=F32)[None, :]
    z = jnp.concatenate([t, jnp.cos(bands * w), -jnp.sin(bands * w)], axis=-1)
    return jnp.pad(z, ((0, 0), (0, HY_FEAT - z.shape[1])))


def _hy_filter_kernel(z_ref, w1_ref, b1_ref, f1_ref, w2_ref, b2_ref, f2_ref, w3_ref, dec_ref,
                      k0_ref, k1_ref, sum_ref):
    i = pl.program_id(0)
    half = pl.num_programs(0) // 2

    @pl.when(i == 0)
    def _():
        sum_ref[...] = jnp.zeros(sum_ref.shape, F32)

    hp = lax.Precision.HIGHEST
    z = z_ref[...]
    h = jnp.sin(f1_ref[...] * (jnp.dot(z, w1_ref[...], preferred_element_type=F32, precision=hp) + b1_ref[...]))
    h = jnp.sin(f2_ref[...] * (jnp.dot(h, w2_ref[...], preferred_element_type=F32, precision=hp) + b2_ref[...]))
    k = jnp.dot(h, w3_ref[0], preferred_element_type=F32, precision=hp)
    k = k * jnp.exp(-z[:, 0:1] * dec_ref[...])
    sum_ref[...] += jnp.sum(jnp.abs(k), axis=0, keepdims=True)
    rid = lax.broadcasted_iota(jnp.int32, k.shape, 0)
    k = jnp.where(i == half, jnp.where(rid == 0, 0.0, k), k)
    k0_ref[...] = k[:, :HY_W].astype(BF16)
    k1_ref[...] = k[:, HY_W:].astype(BF16)


def _hy_filter(L, w1, b1, f1, w2, b2, f2, w3, decay, nb):
    N = 2 * L
    fh = w1.shape[1]
    z = _hy_features(L)
    w1p = jnp.pad(w1, ((0, HY_FEAT - w1.shape[0]), (0, 0)))
    w3d = w3.reshape(fh, HY_ORDER, 2, HY_W).transpose(2, 0, 1, 3).reshape(2, fh, HY_ORDER * HY_W)
    dec = jnp.abs(decay).reshape(1, HY_ORDER * HY_W)
    row = lambda a: a[None, :]
    full = lambda i: (0, 0)
    half = N // nb // 2
    k0, k1, tot = pl.pallas_call(
        _hy_filter_kernel,
        grid=(N // nb,),
        in_specs=[
            pl.BlockSpec((nb, HY_FEAT), lambda i: (i, 0)),
            pl.BlockSpec((HY_FEAT, fh), full), pl.BlockSpec((1, fh), full), pl.BlockSpec((1, fh), full),
            pl.BlockSpec((fh, fh), full), pl.BlockSpec((1, fh), full), pl.BlockSpec((1, fh), full),
            pl.BlockSpec((1, fh, HY_ORDER * HY_W), lambda i: (i // half, 0, 0)),
            pl.BlockSpec((1, HY_ORDER * HY_W), full),
        ],
        out_specs=[pl.BlockSpec((nb, HY_W), lambda i: (i, 0)), pl.BlockSpec((nb, HY_W), lambda i: (i, 0)),
                   pl.BlockSpec((1, HY_ORDER * HY_W), full)],
        out_shape=[jax.ShapeDtypeStruct((N, HY_W), BF16), jax.ShapeDtypeStruct((N, HY_W), BF16),
                   jax.ShapeDtypeStruct((1, HY_ORDER * HY_W), F32)],
        compiler_params=_cparams(("arbitrary",)),
        name="hyena_filter",
    )(z, w1p, row(b1), row(f1), w2, row(b2), row(f2), w3d, dec)
    return k0, k1, 1.0 / tot


def _dft_tables(N1, N2):
    N = N1 * N2
    ar = lambda n: jnp.arange(n, dtype=jnp.int32)

    def cs(m, period):
        ang = (-2.0 * math.pi / period) * (m % period).astype(F32)
        return jnp.cos(ang), jnp.sin(ang)

    fr, fi = cs(ar(N1)[:, None] * ar(N1)[None, :], N1)
    hr, hi = fr[:, :max(N1 // 2, 1)], fi[:, :max(N1 // 2, 1)]
    w_fwd = jnp.block([[hr, -hi], [hi, hr]])
    w_real = jnp.concatenate([fr, fi], axis=0)
    w_inv = jnp.block([[hr.T, hi.T], [-hi.T, hr.T]]) / N1
    f2r, f2i = cs(ar(N2)[:, None] * ar(N2)[None, :], N2)
    tr, ti = cs(ar(N1)[:, None] * ar(N2)[None, :], N)
    gr = f2r[None] * tr[:, None, :] - f2i[None] * ti[:, None, :]
    gi = f2r[None] * ti[:, None, :] + f2i[None] * tr[:, None, :]
    g = jnp.concatenate([jnp.concatenate([gr, -gi], axis=2), jnp.concatenate([gi, gr], axis=2)], axis=1)
    grt, git = gr.transpose(0, 2, 1), gi.transpose(0, 2, 1)
    gh = jnp.concatenate([jnp.concatenate([grt, git], axis=2), jnp.concatenate([-git, grt], axis=2)], axis=1) / N2
    return tuple(a.astype(BF16) for a in (w_fwd, w_real, w_inv, g, gh))


def _colmm_kernel(w_ref, x_ref, o_ref):
    o_ref[...] = jnp.dot(w_ref[...], x_ref[...], preferred_element_type=F32).astype(o_ref.dtype)


def _colmm(w, x, cb):
    M, K = w.shape
    C = x.shape[1]
    return pl.pallas_call(
        _colmm_kernel,
        grid=(C // cb,),
        in_specs=[pl.BlockSpec((M, K), lambda j: (0, 0)), pl.BlockSpec((K, cb), lambda j: (0, j))],
        out_specs=pl.BlockSpec((M, cb), lambda j: (0, j)),
        out_shape=jax.ShapeDtypeStruct((M, C), BF16),
        compiler_params=_cparams(("parallel",)),
        name="dft_outer",
    )(w, x)


def _colmm_gate_kernel(w_ref, d_ref, u_ref, g_ref, b_ref, o_ref):
    y = jnp.dot(w_ref[...], d_ref[...], preferred_element_type=F32)
    o_ref[...] = (g_ref[...].astype(F32) * (y + u_ref[...].astype(F32) * b_ref[...])).astype(o_ref.dtype)


def _colmm_gate(w, d, u, gate, bias_row, cb):
    M, K = w.shape
    C = d.shape[1]
    col = lambda j: (0, j)
    return pl.pallas_call(
        _colmm_gate_kernel,
        grid=(C // cb,),
        in_specs=[pl.BlockSpec((M, K), lambda j: (0, 0)), pl.BlockSpec((K, cb), col), pl.BlockSpec((M, cb), col),
                  pl.BlockSpec((M, cb), col), pl.BlockSpec((1, cb), lambda j: (0, 0))],
        out_specs=pl.BlockSpec((M, cb), col),
        out_shape=jax.ShapeDtypeStruct((M, C), BF16),
        compiler_params=_cparams(("parallel",)),
        name="dft_outer_gate",
    )(w, d, u, gate, bias_row)


def _mid_fwd_kernel(a_ref, g_ref, s_ref, k_ref):
    n2 = a_ref.shape[2]
    a = a_ref[:, 0].reshape(2 * n2, a_ref.shape[3])
    x = jnp.dot(g_ref[0], a, preferred_element_type=F32) * s_ref[...]
    k_ref[:, 0] = x.reshape(2, n2, x.shape[1])


def _mid_fwd(a, g, scale):
    _, N1, N2, C = a.shape
    blk = pl.BlockSpec((2, 1, N2, C), lambda i: (0, i, 0, 0))
    return pl.pallas_call(
        _mid_fwd_kernel,
        grid=(N1,),
        in_specs=[blk, pl.BlockSpec((1, 2 * N2, 2 * N2), lambda i: (i, 0, 0)), pl.BlockSpec((1, C), lambda i: (0, 0))],
        out_specs=blk,
        out_shape=jax.ShapeDtypeStruct((2, N1, N2, C), F32),
        compiler_params=_cparams(("parallel",)),
        name="dft_inner_filter",
    )(a, g, scale)


def _mid_kernel(a_ref, g_ref, gh_ref, k_ref, d_ref):
    n2 = a_ref.shape[2]
    c = a_ref.shape[3]
    a = a_ref[:, 0].reshape(2 * n2, c)
    x = jnp.dot(g_ref[0], a, preferred_element_type=F32)
    xr, xi = x[:n2], x[n2:]
    kr, ki = k_ref[0, 0], k_ref[1, 0]
    y = jnp.concatenate([xr * kr - xi * ki, xr * ki + xi * kr], axis=0).astype(BF16)
    d = jnp.dot(gh_ref[0], y, preferred_element_type=F32)
    d_ref[:, 0] = d.reshape(2, n2, c).astype(d_ref.dtype)


def _mid(a, g, gh, kf):
    _, N1, N2, C = a.shape
    blk = pl.BlockSpec((2, 1, N2, C), lambda i: (0, i, 0, 0))
    tab = pl.BlockSpec((1, 2 * N2, 2 * N2), lambda i: (i, 0, 0))
    return pl.pallas_call(
        _mid_kernel,
        grid=(N1,),
        in_specs=[blk, tab, tab, blk],
        out_specs=blk,
        out_shape=jax.ShapeDtypeStruct((2, N1, N2, C), BF16),
        compiler_params=_cparams(("parallel",)),
        name="dft_inner",
    )(a, g, gh, kf)


def _hyena_long(v, x1, x2, filt, bias, tabs, cb):
    B, L, C = v.shape
    assert B == 2, "the two batches are packed as real / imaginary parts of one complex signal"
    w_fwd, w_real, w_inv, g, gh = tabs
    N1 = g.shape[0]
    N2 = g.shape[1] // 2
    k0, k1, inv_norm = filt
    flat = lambda a: a.reshape(-1, N2 * C)
    z = flat(v)
    for o, (ker, gate) in enumerate(((k0, x1), (k1, x2))):
        if N1 > 1:
            ka = _colmm(w_real, flat(ker), cb).reshape(2, N1, N2, C)
            za = _colmm(w_fwd, z, cb).reshape(2, N1, N2, C)
        else:
            ka = jnp.stack([ker, jnp.zeros_like(ker)]).reshape(2, 1, N2, C)
            za = jnp.pad(z.reshape(2, L, C), ((0, 0), (0, L), (0, 0))).reshape(2, 1, N2, C)
        kf = _mid_fwd(ka, g, inv_norm[:, o * C:(o + 1) * C])
        d = _mid(za, g, gh, kf)
        if N1 > 1:
            z = _colmm_gate(w_inv, flat(d), z, flat(gate), jnp.tile(bias[o], cb // C)[None, :], cb)
        else:
            y = d.reshape(2, N2, C)[:, :L].astype(F32)
            zf = z.reshape(2, L, C).astype(F32)
            z = (gate.astype(F32) * (y + zf * bias[o])).astype(BF16).reshape(-1, N2 * C // 2)
    return z.reshape(B, L, C)


def _resid_kernel(x_ref, fT_ref, gt_ref, g_ref, o_ref, *, final):
    xn = x_ref[...] + gt_ref[0] * fT_ref[...].T
    if final:
        xn = xn * lax.rsqrt(jnp.mean(xn * xn, axis=-1, keepdims=True) + EPS) * g_ref[...]
    o_ref[...] = xn


def _resid(x2d, fT, col0, gt, g, rows_per_batch, tm, final):
    T, D = x2d.shape
    tpb = rows_per_batch // tm
    c0 = col0 // tm
    row = lambda i: (i, 0)
    return pl.pallas_call(
        functools.partial(_resid_kernel, final=final),
        grid=(T // tm,),
        in_specs=[pl.BlockSpec((tm, D), row), pl.BlockSpec((D, tm), lambda i: (0, c0 + i)),
                  pl.BlockSpec((1, 1, D), lambda i: (i // tpb, 0, 0)), pl.BlockSpec((1, D), lambda i: (0, 0))],
        out_specs=pl.BlockSpec((tm, D), row),
        out_shape=jax.ShapeDtypeStruct((T, D), F32),
        compiler_params=_cparams(("parallel",)),
        name="residual",
    )(x2d, fT, gt, g)


def _peer(h2, wq, subkeys, u_bf, vT_bf, tc, nb):
    T, D = h2.shape
    hT = h2.T
    wqT = wq.T
    wqT_hi = wqT.astype(BF16)
    wqT_lo = (wqT - wqT_hi.astype(F32)).astype(BF16)
    cnt, e1, r2, e2 = _peer_route(hT, wqT_hi, wqT_lo, subkeys, tc)
    cnt4 = cnt.reshape(PEER_HEADS, PEER_NKEYS // nb, nb, T)
    e14 = e1.reshape(PEER_HEADS, PEER_NKEYS // nb, nb, T)
    return _peer_dense(hT, r2, e2, cnt4, e14, u_bf, vT_bf, tc, nb)


def kernel(x, c, ctx, c_ctx, w_mod, b_mod, g_norm1, g_norm2, w_in, q_gain, k_gain, sc_conv_w, hy_conv_w,
           hy_conv_b, hy_w1, hy_b1, hy_f1, hy_w2, hy_b2, hy_f2, hy_w3, hy_decay, hy_bias, w_br_att, w_br_sc,
           w_br_hy, w_out, peer_wq, peer_subkeys, peer_u, peer_v, g_final):
    B, S, D = x.shape
    Lc = ctx.shape[1]
    hp = lax.Precision.HIGHEST
    cond = jnp.concatenate([jax.nn.silu(c), jnp.broadcast_to(jax.nn.silu(c_ctx), (B, D))], axis=0)
    cos, sin = _rope_tables(S, True)
    cos_c, sin_c = _rope_tables(Lc, False)
    tm = min(1024, S)
    tq = min(512, S)
    tr = min(512, S)
    cb = 16 * HY_W
    tabs = _dft_tables(2 * S // HY_N2, HY_N2)
    tabs_c = _dft_tables(1, 2 * Lc)
    x2 = x.reshape(B * S, D)
    ctx2 = ctx.reshape(B * Lc, D)

    for l in range(DEPTH):
        need_ctx = l < DEPTH - 1
        mod = (jnp.dot(cond, w_mod[l], precision=hp) + b_mod[l]).reshape(2, B, 1, 6, D)
        sh1, sc1, gt1, sh2, sc2, gt2 = (mod[0, :, :, i] for i in range(6))
        csh1, csc1, cgt1, csh2, csc2, cgt2 = (mod[1, :, :, i] for i in range(6))
        w_in_bf = jnp.concatenate([w_in[l][:, REF_GT0:], w_in[l][:, REF_SC0:REF_GT0], w_in[l][:, :REF_SC0]],
                                  axis=1).astype(BF16)
        wa, ws, wh, wo = (w.astype(BF16) for w in (w_br_att[l], w_br_sc[l], w_br_hy[l], w_out[l]))
        g1 = g_norm1[l][None, :]
        g2 = g_norm2[l][None, :]
        hy_params = (hy_w1[l], hy_b1[l], hy_f1[l], hy_w2[l], hy_b2[l], hy_f2[l], hy_w3[l], hy_decay[l])

        p = _in_proj(x2, g1, sc1, sh1, w_in_bf, S, min(2048, S), 2304)
        p3 = p.reshape(B, S, -1)
        pc = _in_proj(ctx2, g1, csc1, csh1, w_in_bf, Lc, Lc, 768)
        pc3 = pc.reshape(B, Lc, -1)

        qs, k = _qk_prep(p, cos, sin, q_gain[l], k_gain[l], S, tr)
        qcs, kc = _qk_prep(pc, cos_c, sin_c, q_gain[l], k_gain[l], Lc, Lc)
        k_all = jnp.concatenate([kc.reshape(B, Lc, KV_W), k.reshape(B, S, KV_W)], axis=1)
        v_all = jnp.concatenate([pc3[..., V0:IN_W], p3[..., V0:IN_W]], axis=1)
        tk = (S + Lc) // 13 if (S + Lc) % (13 * 128) == 0 else 128
        y_att = _attention(qs.reshape(B, S, ATT_W), k_all, v_all, tq, tk).reshape(B * S, ATT_W)

        y_sc = _short_conv(p3, sc_conv_w[l], tr).reshape(B * S, SC_W)
        y_hy = _hyena_long(*_hy_pre(p3, hy_conv_w[l], hy_conv_b[l], tr), _hy_filter(S, *hy_params, tr),
                           hy_bias[l], tabs, cb).reshape(B * S, HY_W)
        x2, h2 = _merge(x2, y_att, y_sc, y_hy, p, gt1, g2, sc2, sh2, wa, ws, wh, wo, S, tm)

        if need_ctx:
            yc_att = _attention(qcs.reshape(B, Lc, ATT_W), kc.reshape(B, Lc, KV_W), pc3[..., V0:IN_W], Lc, Lc)
            yc_sc = _short_conv(pc3, sc_conv_w[l], Lc).reshape(B * Lc, SC_W)
            yc_hy = _hyena_long(*_hy_pre(pc3, hy_conv_w[l], hy_conv_b[l], Lc), _hy_filter(Lc, *hy_params, Lc),
                                hy_bias[l], tabs_c, cb).reshape(B * Lc, HY_W)
            ctx2, h2c = _merge(ctx2, yc_att.reshape(B * Lc, ATT_W), yc_sc, yc_hy, pc, cgt1, g2, csc2, csh2,
                               wa, ws, wh, wo, Lc, Lc)
            tok = jnp.concatenate([h2, h2c], axis=0)
        else:
            tok = h2

        u_bf = peer_u[l].astype(BF16)
        vT_bf = peer_v[l].astype(BF16).reshape(-1, PEER_BLOCK * PEER_NKEYS, D).transpose(0, 2, 1)
        fT = _peer(tok, peer_wq[l], peer_subkeys[l], u_bf, vT_bf, min(512, tok.shape[0]), PEER_BLOCK)
        last = l == DEPTH - 1
        x2 = _resid(x2, fT, 0, gt2, g_final[None, :], S, tr, last)
        if need_ctx:
            ctx2 = _resid(ctx2, fT, B * S, cgt2, g_final[None, :], Lc, Lc, False)

    return x2.reshape(B, S, D)
```

```python
import functools
import math

import jax
import jax.numpy as jnp
from jax import lax
from jax.experimental import pallas as pl
from jax.experimental.pallas import tpu as pltpu

F32 = jnp.float32
BF16 = jnp.bfloat16

DEPTH = 2
GRID_W = 64
EPS = 1e-6
N_HEADS = 8
N_KV_HEADS = 2
GQA_GROUP = N_HEADS // N_KV_HEADS
HEAD_DIM = 64
AXIS_DIM = HEAD_DIM // 2
ATT_W = N_HEADS * HEAD_DIM
KV_W = N_KV_HEADS * HEAD_DIM
ATT_SCALE = HEAD_DIM ** -0.5
ROPE_THETA = 10000.0
SC_W = 512
HY_W = 512
HY_ORDER = 2
HY_BANDS = 16
PEER_HEADS = 8
PEER_NKEYS = 128
PEER_DK = 128
PEER_TOPK = 16
D_MODEL = 1024
QKV_W = ATT_W + 2 * KV_W
REF_SC0 = QKV_W
REF_GT0 = QKV_W + 3 * SC_W + (HY_ORDER + 1) * HY_W
GT0 = 0
SC0 = GT0 + 3 * D_MODEL
HY0 = SC0 + 3 * SC_W
Q0 = HY0 + (HY_ORDER + 1) * HY_W
K0 = Q0 + ATT_W
V0 = K0 + KV_W
IN_W = V0 + KV_W
HALO = 8
HY_N2 = 256
HY_FEAT = 128

VMEM_BYTES_V7X = 64 * 1024 * 1024
VMEM_LIMIT = VMEM_BYTES_V7X - 8 * 1024 * 1024
MXU_WIDTH_V7X = 256
LANES = 128
SUBLANES = 8
PACKED_ROWS = 16

TM_IN_PROJ = 2048
TM_MATMUL = 1024
TM_ROWS = 512
TQ_ATT = 512
TK_ATT_MAX = 2048
TC_PEER = 512
DFT_COLS = 32


def _cparams(sem):
    return pltpu.CompilerParams(dimension_semantics=sem, vmem_limit_bytes=VMEM_LIMIT)


def _key_block(lk):
    for unit in (MXU_WIDTH_V7X, LANES):
        cands = [t for t in range(unit, min(lk, TK_ATT_MAX) + 1, unit) if lk % t == 0]
        if cands:
            return max(cands)
    raise ValueError(f"no lane-aligned key block divides {lk}")


def _in_proj_kernel(x_ref, g_ref, sc_ref, sh_ref, w_ref, o_ref, h_scr):
    @pl.when(pl.program_id(1) == 0)
    def _():
        x = x_ref[...]
        ms = jnp.mean(x * x, axis=-1, keepdims=True)
        y = x * lax.rsqrt(ms + EPS) * g_ref[...]
        h_scr[...] = (y * (1.0 + sc_ref[0]) + sh_ref[0]).astype(BF16)

    o_ref[...] = jnp.dot(h_scr[...], w_ref[...], preferred_element_type=F32).astype(o_ref.dtype)


def _in_proj(x2d, g, sc, sh, w, rows_per_batch, tm, tn):
    T, D = x2d.shape
    N = w.shape[1]
    tpb = rows_per_batch // tm
    return pl.pallas_call(
        _in_proj_kernel,
        grid=(T // tm, N // tn),
        in_specs=[
            pl.BlockSpec((tm, D), lambda i, j: (i, 0)),
            pl.BlockSpec((1, D), lambda i, j: (0, 0)),
            pl.BlockSpec((1, 1, D), lambda i, j: (i // tpb, 0, 0)),
            pl.BlockSpec((1, 1, D), lambda i, j: (i // tpb, 0, 0)),
            pl.BlockSpec((D, tn), lambda i, j: (0, j)),
        ],
        out_specs=pl.BlockSpec((tm, tn), lambda i, j: (i, j)),
        out_shape=jax.ShapeDtypeStruct((T, N), BF16),
        scratch_shapes=[pltpu.VMEM((tm, D), BF16)],
        compiler_params=_cparams(("parallel", "arbitrary")),
        name="in_proj",
    )(x2d, g, sc, sh, w)


def _attn_kernel(q_ref, kT_ref, v_ref, o_ref, q_scr, m_scr, acc_scr, sa_scr, sb_scr, *, nkb):
    tq = q_ref.shape[1]
    m_scr[...] = jnp.full(m_scr.shape, -jnp.inf, F32)
    acc_scr[...] = jnp.zeros(acc_scr.shape, F32)
    for g in range(GQA_GROUP):
        q_scr[g * tq:(g + 1) * tq, :] = q_ref[0, :, g * HEAD_DIM:(g + 1) * HEAD_DIM]

    def scores(j, dst_scr):
        dst_scr[...] = jnp.dot(q_scr[...], kT_ref[0, 0, j], preferred_element_type=F32)

    def update(j, src_scr):
        s = src_scr[...]
        m_prev = m_scr[...]
        m_new = jnp.maximum(m_prev, jnp.max(s, axis=-1, keepdims=True))
        p = jnp.exp(s - m_new).astype(BF16)
        alpha = jnp.exp(m_prev - m_new)
        acc_scr[...] = acc_scr[...] * alpha + jnp.dot(p, v_ref[0, 0, j], preferred_element_type=F32)
        m_scr[...] = m_new

    scores(0, sa_scr)

    def pair(i, carry):
        j = 2 * i
        scores(j + 1, sb_scr)
        update(j, sa_scr)
        scores(j + 2, sa_scr)
        update(j + 1, sb_scr)
        return carry

    lax.fori_loop(0, (nkb - 1) // 2, pair, 0)
    if nkb % 2 == 1:
        update(nkb - 1, sa_scr)
    else:
        scores(nkb - 1, sb_scr)
        update(nkb - 2, sa_scr)
        update(nkb - 1, sb_scr)
    acc = acc_scr[...]
    o = (acc[:, :HEAD_DIM] / acc[:, HEAD_DIM:HEAD_DIM + 1]).astype(o_ref.dtype)
    o_ref[0] = jnp.concatenate([o[g * tq:(g + 1) * tq] for g in range(GQA_GROUP)], axis=1)


def _attention(q, k, v, tq, tk):
    B, Lq, _ = q.shape
    Lk = k.shape[1]
    nqb, nkb = Lq // tq, Lk // tk
    R = GQA_GROUP * tq
    GW = GQA_GROUP * HEAD_DIM
    kT = k.reshape(B, nkb, tk, N_KV_HEADS, HEAD_DIM).transpose(0, 3, 1, 4, 2)
    vb = v.reshape(B, nkb, tk, N_KV_HEADS, HEAD_DIM).transpose(0, 3, 1, 2, 4)
    ones = jnp.ones(vb.shape[:-1] + (1,), BF16)
    zeros = jnp.zeros(vb.shape[:-1] + (HEAD_DIM - 1,), BF16)
    vb = jnp.concatenate([vb, ones, zeros], axis=-1)
    return pl.pallas_call(
        functools.partial(_attn_kernel, nkb=nkb),
        grid=(B, N_KV_HEADS, nqb),
        in_specs=[
            pl.BlockSpec((1, tq, GW), lambda b, h, i: (b, i, h)),
            pl.BlockSpec((1, 1, nkb, HEAD_DIM, tk), lambda b, h, i: (b, h, 0, 0, 0)),
            pl.BlockSpec((1, 1, nkb, tk, 2 * HEAD_DIM), lambda b, h, i: (b, h, 0, 0, 0)),
        ],
        out_specs=pl.BlockSpec((1, tq, GW), lambda b, h, i: (b, i, h)),
        out_shape=jax.ShapeDtypeStruct((B, Lq, ATT_W), BF16),
        scratch_shapes=[pltpu.VMEM((R, HEAD_DIM), BF16), pltpu.VMEM((R, 1), F32), pltpu.VMEM((R, 2 * HEAD_DIM), F32),
                        pltpu.VMEM((R, tk), F32), pltpu.VMEM((R, tk), F32)],
        compiler_params=_cparams(("parallel", "parallel", "arbitrary")),
        name="attention",
    )(q, kT, vb)


def _merge_kernel(x_ref, ya_ref, ys_ref, yh_ref, ga_ref, gs_ref, gh_ref, gt_ref, g2_ref, sc_ref, sh_ref,
                  wa_ref, ws_ref, wh_ref, wo_ref, xo_ref, h2_ref):
    def br(y_ref, g_ref, w_ref):
        gate = jax.nn.sigmoid(g_ref[...].astype(F32))
        return gate * jnp.dot(y_ref[...], w_ref[...], preferred_element_type=F32)

    m = br(ya_ref, ga_ref, wa_ref) + br(ys_ref, gs_ref, ws_ref) + br(yh_ref, gh_ref, wh_ref)
    o = jnp.dot(m.astype(BF16), wo_ref[...], preferred_element_type=F32)
    xn = x_ref[...] + gt_ref[0] * o
    xo_ref[...] = xn
    ms = jnp.mean(xn * xn, axis=-1, keepdims=True)
    y = xn * lax.rsqrt(ms + EPS) * g2_ref[...]
    h2_ref[...] = (y * (1.0 + sc_ref[0]) + sh_ref[0]).astype(BF16)


def _merge(x2d, ya, ys, yh, p, gt1, g2, sc2, sh2, wa, ws, wh, wo, rows_per_batch, tm):
    T, D = x2d.shape
    tpb = rows_per_batch // tm
    gblk = GT0 // D
    row = lambda i: (i, 0)
    mod = lambda i: (i // tpb, 0, 0)
    full = lambda i: (0, 0)
    return pl.pallas_call(
        _merge_kernel,
        grid=(T // tm,),
        in_specs=[
            pl.BlockSpec((tm, D), row),
            pl.BlockSpec((tm, ATT_W), row),
            pl.BlockSpec((tm, SC_W), row),
            pl.BlockSpec((tm, HY_W), row),
            pl.BlockSpec((tm, D), lambda i: (i, gblk)),
            pl.BlockSpec((tm, D), lambda i: (i, gblk + 1)),
            pl.BlockSpec((tm, D), lambda i: (i, gblk + 2)),
            pl.BlockSpec((1, 1, D), mod),
            pl.BlockSpec((1, D), full),
            pl.BlockSpec((1, 1, D), mod),
            pl.BlockSpec((1, 1, D), mod),
            pl.BlockSpec((ATT_W, D), full),
            pl.BlockSpec((SC_W, D), full),
            pl.BlockSpec((HY_W, D), full),
            pl.BlockSpec((D, D), full),
        ],
        out_specs=[pl.BlockSpec((tm, D), row), pl.BlockSpec((tm, D), row)],
        out_shape=[jax.ShapeDtypeStruct((T, D), F32), jax.ShapeDtypeStruct((T, D), BF16)],
        compiler_params=_cparams(("parallel",)),
        name="merge",
    )(x2d, ya, ys, yh, p, p, p, gt1, g2, sc2, sh2, wa, ws, wh, wo)


PEER_BLOCK = 8
PEER_MM = 8
PEER_ACC = 2


def _peer_kernel(hT_ref, r2_ref, e2_ref, n_ref, e1_ref, u_ref, vT_ref, o_ref, gwa_scr, gwb_scr, act_scr, *, nb):
    e = pl.program_id(1)
    tc = hT_ref.shape[1]

    @pl.when(e == 0)
    def _():
        o_ref[...] = jnp.zeros(o_ref.shape, F32)
        gwb_scr[...] = jnp.zeros(gwb_scr.shape, BF16)

    def step(cur_scr, prev_scr):
        blk = jnp.minimum(e, pl.num_programs(1) - 2)
        shape3 = (PEER_NKEYS // PACKED_ROWS, PACKED_ROWS, LANES)
        for g0 in range(0, nb, PEER_MM):
            grows = slice(g0 * PEER_NKEYS, (g0 + PEER_MM) * PEER_NKEYS)
            o_ref[...] += jnp.dot(vT_ref[0, :, grows], prev_scr[grows, :], preferred_element_type=F32)
            for lt in range(tc // LANES):
                lanes = slice(lt * LANES, (lt + 1) * LANES)
                for i0 in range(g0, g0 + PEER_MM, PEER_ACC):
                    ws = [jnp.zeros(shape3, BF16) for _ in range(PEER_ACC)]
                    for h in range(PEER_HEADS):
                        r2t = pltpu.bitcast(r2_ref[h, :, lanes], BF16).reshape(shape3)
                        e2t = pltpu.bitcast(e2_ref[h, :, lanes], BF16).reshape(shape3)
                        for k in range(PEER_ACC):
                            ii = i0 + k
                            cnt = jnp.broadcast_to(n_ref[h, blk, ii:ii + 1, lanes], (PACKED_ROWS, LANES)).astype(BF16)
                            e1row = jnp.broadcast_to(e1_ref[h, blk, ii:ii + 1, lanes],
                                                     (PACKED_ROWS, LANES)).astype(BF16)
                            ws[k] = ws[k] + jnp.where(r2t < cnt[None], e2t, jnp.zeros_like(e2t)) * e1row[None]
                    for k in range(PEER_ACC):
                        rows = slice((i0 + k) * PEER_NKEYS, (i0 + k + 1) * PEER_NKEYS)
                        cur_scr[rows, lanes] = ws[k].reshape(PEER_NKEYS, LANES)
            a = jnp.dot(u_ref[grows, :], hT_ref[...], preferred_element_type=F32)
            act = (0.5 * a * (1.0 + lax.erf(a * (2.0 ** -0.5)))).astype(BF16)
            cur_scr[grows, :] = cur_scr[grows, :] * act

    @pl.when(e % 2 == 0)
    def _():
        step(gwa_scr, gwb_scr)

    @pl.when(e % 2 == 1)
    def _():
        step(gwb_scr, gwa_scr)


def _peer_dense(hT, r2, e2, cnt, e1, u, vT, tc, nb):
    D, T = hT.shape
    N = u.shape[0]
    eb = nb * PEER_NKEYS
    ne = N // eb
    assert ne % 2 == 0, "the drain step must find the last block in the buffer the parity rule reads"
    cur = lambda e: jnp.minimum(e, ne - 1)
    return pl.pallas_call(
        functools.partial(_peer_kernel, nb=nb),
        grid=(T // tc, ne + 1),
        in_specs=[
            pl.BlockSpec((D, tc), lambda t, e: (0, t)),
            pl.BlockSpec((PEER_HEADS, PEER_NKEYS // 2, tc), lambda t, e: (0, 0, t)),
            pl.BlockSpec((PEER_HEADS, PEER_NKEYS // 2, tc), lambda t, e: (0, 0, t)),
            pl.BlockSpec((PEER_HEADS, ne, nb, tc), lambda t, e: (0, 0, 0, t)),
            pl.BlockSpec((PEER_HEADS, ne, nb, tc), lambda t, e: (0, 0, 0, t)),
            pl.BlockSpec((eb, D), lambda t, e: (cur(e), 0)),
            pl.BlockSpec((1, D, eb), lambda t, e: (jnp.maximum(e - 1, 0), 0, 0)),
        ],
        out_specs=pl.BlockSpec((D, tc), lambda t, e: (0, t)),
        out_shape=jax.ShapeDtypeStruct((D, T), F32),
        scratch_shapes=[pltpu.VMEM((eb, tc), BF16), pltpu.VMEM((eb, tc), BF16), pltpu.VMEM((eb, tc), BF16)],
        compiler_params=_cparams(("parallel", "arbitrary")),
        name="peer_dense",
    )(hT, r2, e2, cnt, e1, u, vT)


ROUTE_ROWS = 24
NEG_INF = float("-inf")
ROUTE_UNROLL = 4


def _sort_network(n):
    pairs = []
    p = 1
    while p < n:
        k = p
        while k >= 1:
            for j in range(k % p, n - k, 2 * k):
                for i in range(min(k, n - j - k)):
                    if (i + j) // (2 * p) == (i + j + k) // (2 * p):
                        pairs.append((i + j, i + j + k))
            k //= 2
        p *= 2
    return pairs


def _top_rows_sorted(s, n):
    tc = s.shape[1]
    levels = s.shape[0] // SUBLANES
    col = [s[r * SUBLANES:(r + 1) * SUBLANES] for r in range(levels)]
    for a, b in _sort_network(1 << (levels - 1).bit_length()):
        if b < levels:
            col[a], col[b] = jnp.maximum(col[a], col[b]), jnp.minimum(col[a], col[b])
    rid = lax.broadcasted_iota(jnp.int32, (ROUTE_ROWS, tc), 0)
    packed = jnp.full((ROUTE_ROWS, tc), NEG_INF, F32)
    for k in range(n):
        m = jnp.max(col[0], axis=0, keepdims=True)
        packed = jnp.where(rid == k, m, packed)
        hit = col[0] == m
        live = min(levels, n - k)
        for r in range(live):
            below = col[r + 1] if r + 1 < levels else NEG_INF
            col[r] = jnp.where(hit, below, col[r])
    return packed


def _row_penalty(tc, lo, hi):
    rid = lax.broadcasted_iota(jnp.int32, (8, tc), 0)
    return jnp.where((rid >= lo) & (rid < hi), 0.0, NEG_INF).astype(F32)


def _route_kernel(hT_ref, wqh_ref, wql_ref, sk_ref, n_ref, e1_ref, r2_ref, e2_ref, q_scr):
    hT = hT_ref[...]
    q_scr[...] = (jnp.dot(wqh_ref[...], hT, preferred_element_type=F32)
                  + jnp.dot(wql_ref[...], hT, preferred_element_type=F32))
    tc = hT.shape[1]
    n = PEER_TOPK + 1
    half = PEER_DK // 2

    def head(h, carry):
        r0 = pl.multiple_of(h * PEER_DK, PEER_DK)
        hp = lax.Precision.HIGHEST
        s1 = jnp.dot(sk_ref[h, 0], q_scr[pl.ds(r0, half), :], preferred_element_type=F32, precision=hp)
        s2 = jnp.dot(sk_ref[h, 1], q_scr[pl.ds(r0 + half, half), :], preferred_element_type=F32, precision=hp)
        v1 = _top_rows_sorted(s1, n)
        v2 = _top_rows_sorted(s2, n)
        tiles = [v1[0:1] + v2[0:8], v1[0:1] + v2[8:16], v1[0:1] + v2[16:24], v1[1:2] + v2[0:8],
                 v1[2:3] + v2[0:8] + _row_penalty(tc, 0, n // 3), v1[3:4] + v2[0:8] + _row_penalty(tc, 0, n // 4),
                 v2[0:1] + v1[0:8] + _row_penalty(tc, 4, 8), v2[0:1] + v1[8:16], v2[0:1] + v1[16:24],
                 v2[1:2] + v1[0:8] + _row_penalty(tc, 4, n // 2), v2[2:3] + v1[0:8] + _row_penalty(tc, 4, n // 3)]
        top = _top_rows_sorted(jnp.concatenate(tiles, axis=0), n)
        theta = 0.5 * (top[PEER_TOPK - 1:PEER_TOPK] + top[PEER_TOPK:PEER_TOPK + 1])
        z = jnp.sum(jnp.exp(top[0:16] - top[0:1]), axis=0, keepdims=True)
        c = theta - s1
        cnt = jnp.zeros(c.shape, F32)
        rank = jnp.zeros(c.shape, F32)
        for b in range(n):
            cnt = jnp.where(v2[b:b + 1] >= c, float(b + 1), cnt)
            rank = jnp.where(v2[b:b + 1] > s2, float(b + 1), rank)
        n_ref[h] = cnt
        e1_ref[h] = jnp.exp(s1 - v1[0:1]) / z
        r2_ref[h] = pltpu.bitcast(rank.astype(BF16), jnp.uint32)
        e2_ref[h] = pltpu.bitcast(jnp.exp(s2 - v2[0:1]).astype(BF16), jnp.uint32)
        return carry

    def heads(i, c):
        for k in range(ROUTE_UNROLL):
            c = head(ROUTE_UNROLL * i + k, c)
        return c

    lax.fori_loop(0, PEER_HEADS // ROUTE_UNROLL, heads, 0)


def _peer_route(hT, wqT_hi, wqT_lo, subkeys, tc):
    D, T = hT.shape
    W = wqT_hi.shape[0]
    out = jax.ShapeDtypeStruct((PEER_HEADS, PEER_NKEYS, T), F32)
    out16 = jax.ShapeDtypeStruct((PEER_HEADS, PEER_NKEYS // 2, T), jnp.uint32)
    ospec = pl.BlockSpec((PEER_HEADS, PEER_NKEYS, tc), lambda t: (0, 0, t))
    ospec16 = pl.BlockSpec((PEER_HEADS, PEER_NKEYS // 2, tc), lambda t: (0, 0, t))
    return pl.pallas_call(
        _route_kernel,
        grid=(T // tc,),
        in_specs=[
            pl.BlockSpec((D, tc), lambda t: (0, t)),
            pl.BlockSpec((W, D), lambda t: (0, 0)),
            pl.BlockSpec((W, D), lambda t: (0, 0)),
            pl.BlockSpec(subkeys.shape, lambda t: (0, 0, 0, 0)),
        ],
        out_specs=[ospec, ospec, ospec16, ospec16],
        out_shape=[out, out, out16, out16],
        scratch_shapes=[pltpu.VMEM((W, tc), F32)],
        compiler_params=_cparams(("parallel",)),
        name="peer_route",
    )(hT, wqT_hi, wqT_lo, subkeys)


def _rope_tables(L, rope):
    if not rope:
        return jnp.ones((L, 2 * HEAD_DIM), F32), jnp.zeros((L, 2 * HEAD_DIM), F32)
    pos = jnp.arange(L, dtype=jnp.int32)
    row = (pos // GRID_W).astype(F32)
    col = (pos % GRID_W).astype(F32)
    inv = jnp.power(ROPE_THETA, -jnp.arange(0, AXIS_DIM, 2, dtype=F32) / AXIS_DIM)
    ar = row[:, None] * inv[None, :]
    ac = col[:, None] * inv[None, :]
    cos = jnp.concatenate([jnp.cos(ar), jnp.cos(ar), jnp.cos(ac), jnp.cos(ac)], axis=-1)
    sin = jnp.concatenate([-jnp.sin(ar), jnp.sin(ar), -jnp.sin(ac), jnp.sin(ac)], axis=-1)
    return jnp.tile(cos, (1, 2)), jnp.tile(sin, (1, 2))


def _qk_prep_kernel(p_ref, cos_ref, sin_ref, qg_ref, kg_ref, bdq_ref, bdk_ref, q_ref, k_ref):
    x = p_ref[...].astype(F32)
    cos = cos_ref[...]
    sin = sin_ref[...]
    quarter = AXIS_DIM // 2

    def prep(xh, gain, bd):
        w = xh.shape[1]
        sq = xh * xh
        hi = sq.astype(BF16)
        lo = (sq - hi.astype(F32)).astype(BF16)
        ms = jnp.dot(hi, bd, preferred_element_type=F32) + jnp.dot(lo, bd, preferred_element_type=F32)
        y = xh * lax.rsqrt(ms + EPS) * gain
        lane = lax.broadcasted_iota(jnp.int32, y.shape, 1)
        first = (lane & (AXIS_DIM - 1)) < quarter
        partner = jnp.where(first, pltpu.roll(y, w - quarter, 1), pltpu.roll(y, quarter, 1))
        reps = w // cos.shape[1]
        c = jnp.concatenate([cos] * reps, axis=1) if reps > 1 else cos
        s = jnp.concatenate([sin] * reps, axis=1) if reps > 1 else sin
        return y * c + partner * s

    q_ref[...] = (prep(x[:, :ATT_W], qg_ref[...], bdq_ref[...]) * ATT_SCALE).astype(BF16)
    k_ref[...] = prep(x[:, ATT_W:ATT_W + KV_W], kg_ref[...], bdk_ref[...]).astype(BF16)


def _qk_prep(p, cos, sin, q_gain, k_gain, rows_per_batch, tm):
    T = p.shape[0]
    tpb = rows_per_batch // tm
    qg = jnp.tile(q_gain, N_HEADS)[None, :]
    kg = jnp.tile(k_gain, N_KV_HEADS)[None, :]

    def block_avg(w):
        hid = jnp.arange(w) // HEAD_DIM
        return jnp.where(hid[:, None] == hid[None, :], 1.0 / HEAD_DIM, 0.0).astype(BF16)

    full = lambda i: (0, 0)
    return pl.pallas_call(
        _qk_prep_kernel,
        grid=(T // tm,),
        in_specs=[
            pl.BlockSpec((tm, QKV_W), lambda i: (i, Q0 // QKV_W)),
            pl.BlockSpec((tm, 2 * HEAD_DIM), lambda i: (i % tpb, 0)),
            pl.BlockSpec((tm, 2 * HEAD_DIM), lambda i: (i % tpb, 0)),
            pl.BlockSpec((1, ATT_W), full),
            pl.BlockSpec((1, KV_W), full),
            pl.BlockSpec((ATT_W, ATT_W), full),
            pl.BlockSpec((KV_W, KV_W), full),
        ],
        out_specs=[pl.BlockSpec((tm, ATT_W), lambda i: (i, 0)), pl.BlockSpec((tm, KV_W), lambda i: (i, 0))],
        out_shape=[jax.ShapeDtypeStruct((T, ATT_W), BF16), jax.ShapeDtypeStruct((T, KV_W), BF16)],
        compiler_params=_cparams(("parallel",)),
        name="qk_prep",
    )(p, cos, sin, qg, kg, block_avg(ATT_W), block_avg(KV_W))


def _shift_rows(cur, prev8, next8, first, last):
    tm = cur.shape[0]
    rid = lax.broadcasted_iota(jnp.int32, cur.shape, 0)
    pr = jnp.where(first, 0.0, prev8[HALO - 1:HALO, :])
    nx = jnp.where(last, 0.0, next8[0:1, :])
    up = jnp.where(rid == 0, pr, pltpu.roll(cur, 1, 0))
    dn = jnp.where(rid == tm - 1, nx, pltpu.roll(cur, tm - 1, 0))
    return up, dn


def _halo_specs(tm, width, col_block, n_rows):
    r = tm // HALO
    last = n_rows // HALO - 1
    prev = pl.BlockSpec((1, HALO, width), lambda b, i: (b, jnp.maximum(i * r - 1, 0), col_block))
    nxt = pl.BlockSpec((1, HALO, width), lambda b, i: (b, jnp.minimum((i + 1) * r, last), col_block))
    return prev, nxt


def _short_conv_kernel(bg_ref, cg_ref, xs_ref, cgp_ref, xsp_ref, cgn_ref, xsn_ref, w_ref, o_ref):
    i = pl.program_id(1)
    f = lambda r: r[0].astype(F32)
    cur = f(cg_ref) * f(xs_ref)
    up, dn = _shift_rows(cur, f(cgp_ref) * f(xsp_ref), f(cgn_ref) * f(xsn_ref), i == 0, i == pl.num_programs(1) - 1)
    o_ref[0] = (f(bg_ref) * (up * w_ref[0:1] + cur * w_ref[1:2] + dn * w_ref[2:3])).astype(o_ref.dtype)


def _short_conv(p3, w, tm):
    B, L, _ = p3.shape
    c0 = SC0 // SC_W
    blk = lambda j: pl.BlockSpec((1, tm, SC_W), lambda b, i: (b, i, c0 + j))
    cgp, cgn = _halo_specs(tm, SC_W, c0 + 1, L)
    xsp, xsn = _halo_specs(tm, SC_W, c0 + 2, L)
    return pl.pallas_call(
        _short_conv_kernel,
        grid=(B, L // tm),
        in_specs=[blk(0), blk(1), blk(2), cgp, xsp, cgn, xsn, pl.BlockSpec((3, SC_W), lambda b, i: (0, 0))],
        out_specs=pl.BlockSpec((1, tm, SC_W), lambda b, i: (b, i, 0)),
        out_shape=jax.ShapeDtypeStruct((B, L, SC_W), BF16),
        compiler_params=_cparams(("parallel", "parallel")),
        name="short_conv",
    )(p3, p3, p3, p3, p3, p3, p3, w)


def _hy_pre_kernel(p_ref, pp_ref, pn_ref, w_ref, b_ref, v_ref, x1_ref, x2_ref):
    i = pl.program_id(1)
    cur = p_ref[0].astype(F32)
    up, dn = _shift_rows(cur, pp_ref[0].astype(F32), pn_ref[0].astype(F32), i == 0, i == pl.num_programs(1) - 1)
    u = up * w_ref[0:1] + cur * w_ref[1:2] + dn * w_ref[2:3] + b_ref[...]
    v_ref[0] = u[:, :HY_W].astype(BF16)
    x1_ref[0] = u[:, HY_W:2 * HY_W].astype(BF16)
    x2_ref[0] = u[:, 2 * HY_W:].astype(BF16)


def _hy_pre(p3, w, b, tm):
    B, L, _ = p3.shape
    W = (HY_ORDER + 1) * HY_W
    c0 = HY0 // W
    prev, nxt = _halo_specs(tm, W, c0, L)
    out = jax.ShapeDtypeStruct((B, L, HY_W), BF16)
    ospec = pl.BlockSpec((1, tm, HY_W), lambda b_, i: (b_, i, 0))
    return pl.pallas_call(
        _hy_pre_kernel,
        grid=(B, L // tm),
        in_specs=[pl.BlockSpec((1, tm, W), lambda b_, i: (b_, i, c0)), prev, nxt,
                  pl.BlockSpec((3, W), lambda b_, i: (0, 0)), pl.BlockSpec((1, W), lambda b_, i: (0, 0))],
        out_specs=[ospec, ospec, ospec],
        out_shape=[out, out, out],
        compiler_params=_cparams(("parallel", "parallel")),
        name="hyena_pre",
    )(p3, p3, p3, w, b[None, :])


def _hy_features(L):
    n = jnp.arange(2 * L, dtype=jnp.int32)
    j = jnp.where(n < L, n, jnp.where(n == L, 0, 2 * L - n)).astype(F32)[:, None]
    t = j / (L - 1)
    w = 2.0 * math.pi * j / L
    bands = jnp.linspace(1e-4, HY_BANDS - 1, HY_BANDS, dtype=F32)[None, :]
    z = jnp.concatenate([t, jnp.cos(bands * w), -jnp.sin(bands * w)], axis=-1)
    return jnp.pad(z, ((0, 0), (0, HY_FEAT - z.shape[1])))


def _hy_filter_kernel(z_ref, w1_ref, b1_ref, f1_ref, w2_ref, b2_ref, f2_ref, w3_ref, dec_ref,
                      k0_ref, k1_ref, sum_ref):
    i = pl.program_id(0)
    half = pl.num_programs(0) // 2

    @pl.when(i == 0)
    def _():
        sum_ref[...] = jnp.zeros(sum_ref.shape, F32)

    hp = lax.Precision.HIGHEST
    z = z_ref[...]
    h = jnp.sin(f1_ref[...] * (jnp.dot(z, w1_ref[...], preferred_element_type=F32, precision=hp) + b1_ref[...]))
    h = jnp.sin(f2_ref[...] * (jnp.dot(h, w2_ref[...], preferred_element_type=F32, precision=hp) + b2_ref[...]))
    k = jnp.dot(h, w3_ref[0], preferred_element_type=F32, precision=hp)
    k = k * jnp.exp(-z[:, 0:1] * dec_ref[...])
    sum_ref[...] += jnp.sum(jnp.abs(k), axis=0, keepdims=True)
    rid = lax.broadcasted_iota(jnp.int32, k.shape, 0)
    k = jnp.where(i == half, jnp.where(rid == 0, 0.0, k), k)
    k0_ref[...] = k[:, :HY_W].astype(BF16)
    k1_ref[...] = k[:, HY_W:].astype(BF16)


def _hy_filter(L, w1, b1, f1, w2, b2, f2, w3, decay, nb):
    N = 2 * L
    fh = w1.shape[1]
    z = _hy_features(L)
    w1p = jnp.pad(w1, ((0, HY_FEAT - w1.shape[0]), (0, 0)))
    w3d = w3.reshape(fh, HY_ORDER, 2, HY_W).transpose(2, 0, 1, 3).reshape(2, fh, HY_ORDER * HY_W)
    dec = jnp.abs(decay).reshape(1, HY_ORDER * HY_W)
    row = lambda a: a[None, :]
    full = lambda i: (0, 0)
    half = N // nb // 2
    k0, k1, tot = pl.pallas_call(
        _hy_filter_kernel,
        grid=(N // nb,),
        in_specs=[
            pl.BlockSpec((nb, HY_FEAT), lambda i: (i, 0)),
            pl.BlockSpec((HY_FEAT, fh), full), pl.BlockSpec((1, fh), full), pl.BlockSpec((1, fh), full),
            pl.BlockSpec((fh, fh), full), pl.BlockSpec((1, fh), full), pl.BlockSpec((1, fh), full),
            pl.BlockSpec((1, fh, HY_ORDER * HY_W), lambda i: (i // half, 0, 0)),
            pl.BlockSpec((1, HY_ORDER * HY_W), full),
        ],
        out_specs=[pl.BlockSpec((nb, HY_W), lambda i: (i, 0)), pl.BlockSpec((nb, HY_W), lambda i: (i, 0)),
                   pl.BlockSpec((1, HY_ORDER * HY_W), full)],
        out_shape=[jax.ShapeDtypeStruct((N, HY_W), BF16), jax.ShapeDtypeStruct((N, HY_W), BF16),
                   jax.ShapeDtypeStruct((1, HY_ORDER * HY_W), F32)],
        compiler_params=_cparams(("arbitrary",)),
        name="hyena_filter",
    )(z, w1p, row(b1), row(f1), w2, row(b2), row(f2), w3d, dec)
    return k0, k1, 1.0 / tot


def _dft_tables(N1, N2):
    N = N1 * N2
    ar = lambda n: jnp.arange(n, dtype=jnp.int32)

    def cs(m, period):
        ang = (-2.0 * math.pi / period) * (m % period).astype(F32)
        return jnp.cos(ang), jnp.sin(ang)

    fr, fi = cs(ar(N1)[:, None] * ar(N1)[None, :], N1)
    hr, hi = fr[:, :max(N1 // 2, 1)], fi[:, :max(N1 // 2, 1)]
    w_fwd = jnp.block([[hr, -hi], [hi, hr]])
    w_real = jnp.concatenate([fr, fi], axis=0)
    w_inv = jnp.block([[hr.T, hi.T], [-hi.T, hr.T]]) / N1
    f2r, f2i = cs(ar(N2)[:, None] * ar(N2)[None, :], N2)
    tr, ti = cs(ar(N1)[:, None] * ar(N2)[None, :], N)
    gr = f2r[None] * tr[:, None, :] - f2i[None] * ti[:, None, :]
    gi = f2r[None] * ti[:, None, :] + f2i[None] * tr[:, None, :]
    g = jnp.concatenate([jnp.concatenate([gr, -gi], axis=2), jnp.concatenate([gi, gr], axis=2)], axis=1)
    grt, git = gr.transpose(0, 2, 1), gi.transpose(0, 2, 1)
    gh = jnp.concatenate([jnp.concatenate([grt, git], axis=2), jnp.concatenate([-git, grt], axis=2)], axis=1) / N2
    return tuple(a.astype(BF16) for a in (w_fwd, w_real, w_inv, g, gh))


def _colmm_kernel(w_ref, x_ref, o_ref):
    o_ref[...] = jnp.dot(w_ref[...], x_ref[...], preferred_element_type=F32).astype(o_ref.dtype)


def _colmm(w, x, cb):
    M, K = w.shape
    C = x.shape[1]
    return pl.pallas_call(
        _colmm_kernel,
        grid=(C // cb,),
        in_specs=[pl.BlockSpec((M, K), lambda j: (0, 0)), pl.BlockSpec((K, cb), lambda j: (0, j))],
        out_specs=pl.BlockSpec((M, cb), lambda j: (0, j)),
        out_shape=jax.ShapeDtypeStruct((M, C), BF16),
        compiler_params=_cparams(("parallel",)),
        name="dft_outer",
    )(w, x)


def _colmm_gate_kernel(w_ref, d_ref, u_ref, g_ref, b_ref, o_ref):
    y = jnp.dot(w_ref[...], d_ref[...], preferred_element_type=F32)
    o_ref[...] = (g_ref[...].astype(F32) * (y + u_ref[...].astype(F32) * b_ref[...])).astype(o_ref.dtype)


def _colmm_gate(w, d, u, gate, bias_row, cb):
    M, K = w.shape
    C = d.shape[1]
    col = lambda j: (0, j)
    return pl.pallas_call(
        _colmm_gate_kernel,
        grid=(C // cb,),
        in_specs=[pl.BlockSpec((M, K), lambda j: (0, 0)), pl.BlockSpec((K, cb), col), pl.BlockSpec((M, cb), col),
                  pl.BlockSpec((M, cb), col), pl.BlockSpec((1, cb), lambda j: (0, 0))],
        out_specs=pl.BlockSpec((M, cb), col),
        out_shape=jax.ShapeDtypeStruct((M, C), BF16),
        compiler_params=_cparams(("parallel",)),
        name="dft_outer_gate",
    )(w, d, u, gate, bias_row)


def _mid_fwd_kernel(a_ref, g_ref, s_ref, k_ref):
    n2 = a_ref.shape[2]
    a = a_ref[:, 0].reshape(2 * n2, a_ref.shape[3])
    x = jnp.dot(g_ref[0], a, preferred_element_type=F32) * s_ref[...]
    k_ref[:, 0] = x.reshape(2, n2, x.shape[1]).astype(k_ref.dtype)


def _mid_fwd(a, g, scale):
    _, N1, N2, C = a.shape
    blk = pl.BlockSpec((2, 1, N2, C), lambda i: (0, i, 0, 0))
    return pl.pallas_call(
        _mid_fwd_kernel,
        grid=(N1,),
        in_specs=[blk, pl.BlockSpec((1, 2 * N2, 2 * N2), lambda i: (i, 0, 0)), pl.BlockSpec((1, C), lambda i: (0, 0))],
        out_specs=blk,
        out_shape=jax.ShapeDtypeStruct((2, N1, N2, C), BF16),
        compiler_params=_cparams(("parallel",)),
        name="dft_inner_filter",
    )(a, g, scale)


def _mid_kernel(a_ref, g_ref, gh_ref, k_ref, d_ref):
    n2 = a_ref.shape[2]
    c = a_ref.shape[3]
    a = a_ref[:, 0].reshape(2 * n2, c)
    x = jnp.dot(g_ref[0], a, preferred_element_type=F32)
    xr, xi = x[:n2], x[n2:]
    kr, ki = k_ref[0, 0].astype(F32), k_ref[1, 0].astype(F32)
    y = jnp.concatenate([xr * kr - xi * ki, xr * ki + xi * kr], axis=0).astype(BF16)
    d = jnp.dot(gh_ref[0], y, preferred_element_type=F32)
    d_ref[:, 0] = d.reshape(2, n2, c).astype(d_ref.dtype)


def _mid(a, g, gh, kf):
    _, N1, N2, C = a.shape
    blk = pl.BlockSpec((2, 1, N2, C), lambda i: (0, i, 0, 0))
    tab = pl.BlockSpec((1, 2 * N2, 2 * N2), lambda i: (i, 0, 0))
    return pl.pallas_call(
        _mid_kernel,
        grid=(N1,),
        in_specs=[blk, tab, tab, blk],
        out_specs=blk,
        out_shape=jax.ShapeDtypeStruct((2, N1, N2, C), BF16),
        compiler_params=_cparams(("parallel",)),
        name="dft_inner",
    )(a, g, gh, kf)


def _hyena_long(v, x1, x2, filt, bias, tabs, cb):
    B, L, C = v.shape
    assert B == 2, "the two batches are packed as real / imaginary parts of one complex signal"
    w_fwd, w_real, w_inv, g, gh = tabs
    N1 = g.shape[0]
    N2 = g.shape[1] // 2
    k0, k1, inv_norm = filt
    flat = lambda a: a.reshape(-1, N2 * C)
    z = flat(v)
    for o, (ker, gate) in enumerate(((k0, x1), (k1, x2))):
        if N1 > 1:
            ka = _colmm(w_real, flat(ker), cb).reshape(2, N1, N2, C)
            za = _colmm(w_fwd, z, cb).reshape(2, N1, N2, C)
        else:
            ka = jnp.stack([ker, jnp.zeros_like(ker)]).reshape(2, 1, N2, C)
            za = jnp.pad(z.reshape(2, L, C), ((0, 0), (0, L), (0, 0))).reshape(2, 1, N2, C)
        kf = _mid_fwd(ka, g, inv_norm[:, o * C:(o + 1) * C])
        d = _mid(za, g, gh, kf)
        if N1 > 1:
            z = _colmm_gate(w_inv, flat(d), z, flat(gate), jnp.tile(bias[o], cb // C)[None, :], cb)
        else:
            y = d.reshape(2, N2, C)[:, :L].astype(F32)
            zf = z.reshape(2, L, C).astype(F32)
            z = (gate.astype(F32) * (y + zf * bias[o])).astype(BF16).reshape(-1, N2 * C // 2)
    return z.reshape(B, L, C)


def _resid_kernel(x_ref, fT_ref, gt_ref, g_ref, o_ref, *, final):
    xn = x_ref[...] + gt_ref[0] * fT_ref[...].T
    if final:
        xn = xn * lax.rsqrt(jnp.mean(xn * xn, axis=-1, keepdims=True) + EPS) * g_ref[...]
    o_ref[...] = xn


def _resid(x2d, fT, col0, gt, g, rows_per_batch, tm, final):
    T, D = x2d.shape
    tpb = rows_per_batch // tm
    c0 = col0 // tm
    row = lambda i: (i, 0)
    return pl.pallas_call(
        functools.partial(_resid_kernel, final=final),
        grid=(T // tm,),
        in_specs=[pl.BlockSpec((tm, D), row), pl.BlockSpec((D, tm), lambda i: (0, c0 + i)),
                  pl.BlockSpec((1, 1, D), lambda i: (i // tpb, 0, 0)), pl.BlockSpec((1, D), lambda i: (0, 0))],
        out_specs=pl.BlockSpec((tm, D), row),
        out_shape=jax.ShapeDtypeStruct((T, D), F32),
        compiler_params=_cparams(("parallel",)),
        name="residual",
    )(x2d, fT, gt, g)


def _peer(h2, wq, subkeys, u_bf, vT_bf, tc, nb):
    T, D = h2.shape
    hT = h2.T
    wqT = wq.T
    wqT_hi = wqT.astype(BF16)
    wqT_lo = (wqT - wqT_hi.astype(F32)).astype(BF16)
    cnt, e1, r2, e2 = _peer_route(hT, wqT_hi, wqT_lo, subkeys, tc)
    cnt4 = cnt.reshape(PEER_HEADS, PEER_NKEYS // nb, nb, T)
    e14 = e1.reshape(PEER_HEADS, PEER_NKEYS // nb, nb, T)
    return _peer_dense(hT, r2, e2, cnt4, e14, u_bf, vT_bf, tc, nb)


def kernel(x, c, ctx, c_ctx, w_mod, b_mod, g_norm1, g_norm2, w_in, q_gain, k_gain, sc_conv_w, hy_conv_w,
           hy_conv_b, hy_w1, hy_b1, hy_f1, hy_w2, hy_b2, hy_f2, hy_w3, hy_decay, hy_bias, w_br_att, w_br_sc,
           w_br_hy, w_out, peer_wq, peer_subkeys, peer_u, peer_v, g_final):
    B, S, D = x.shape
    Lc = ctx.shape[1]
    hp = lax.Precision.HIGHEST
    cond = jnp.concatenate([jax.nn.silu(c), jnp.broadcast_to(jax.nn.silu(c_ctx), (B, D))], axis=0)
    cos, sin = _rope_tables(S, True)
    cos_c, sin_c = _rope_tables(Lc, False)
    tm, tq, tr = min(TM_MATMUL, S), min(TQ_ATT, S), min(TM_ROWS, S)
    cb = DFT_COLS * HY_W
    tabs = _dft_tables(2 * S // HY_N2, HY_N2)
    tabs_c = _dft_tables(1, 2 * Lc)
    x2 = x.reshape(B * S, D)
    ctx2 = ctx.reshape(B * Lc, D)

    for l in range(DEPTH):
        need_ctx = l < DEPTH - 1
        mod = (jnp.dot(cond, w_mod[l], precision=hp) + b_mod[l]).reshape(2, B, 1, 6, D)
        sh1, sc1, gt1, sh2, sc2, gt2 = (mod[0, :, :, i] for i in range(6))
        csh1, csc1, cgt1, csh2, csc2, cgt2 = (mod[1, :, :, i] for i in range(6))
        w_in_bf = jnp.concatenate([w_in[l][:, REF_GT0:], w_in[l][:, REF_SC0:REF_GT0], w_in[l][:, :REF_SC0]],
                                  axis=1).astype(BF16)
        wa, ws, wh, wo = (w.astype(BF16) for w in (w_br_att[l], w_br_sc[l], w_br_hy[l], w_out[l]))
        g1 = g_norm1[l][None, :]
        g2 = g_norm2[l][None, :]
        hy_params = (hy_w1[l], hy_b1[l], hy_f1[l], hy_w2[l], hy_b2[l], hy_f2[l], hy_w3[l], hy_decay[l])

        p = _in_proj(x2, g1, sc1, sh1, w_in_bf, S, min(TM_IN_PROJ, S), IN_W // 3)
        p3 = p.reshape(B, S, -1)
        pc = _in_proj(ctx2, g1, csc1, csh1, w_in_bf, Lc, Lc, IN_W // 3)
        pc3 = pc.reshape(B, Lc, -1)

        qs, k = _qk_prep(p, cos, sin, q_gain[l], k_gain[l], S, tr)
        qcs, kc = _qk_prep(pc, cos_c, sin_c, q_gain[l], k_gain[l], Lc, Lc)
        k_all = jnp.concatenate([kc.reshape(B, Lc, KV_W), k.reshape(B, S, KV_W)], axis=1)
        v_all = jnp.concatenate([pc3[..., V0:IN_W], p3[..., V0:IN_W]], axis=1)
        y_att = _attention(qs.reshape(B, S, ATT_W), k_all, v_all, tq, _key_block(S + Lc)).reshape(B * S, ATT_W)

        y_sc = _short_conv(p3, sc_conv_w[l], tr).reshape(B * S, SC_W)
        y_hy = _hyena_long(*_hy_pre(p3, hy_conv_w[l], hy_conv_b[l], tr), _hy_filter(S, *hy_params, tr),
                           hy_bias[l], tabs, cb).reshape(B * S, HY_W)
        x2, h2 = _merge(x2, y_att, y_sc, y_hy, p, gt1, g2, sc2, sh2, wa, ws, wh, wo, S, tm)

        if need_ctx:
            yc_att = _attention(qcs.reshape(B, Lc, ATT_W), kc.reshape(B, Lc, KV_W), pc3[..., V0:IN_W], Lc, Lc)
            yc_sc = _short_conv(pc3, sc_conv_w[l], Lc).reshape(B * Lc, SC_W)
            yc_hy = _hyena_long(*_hy_pre(pc3, hy_conv_w[l], hy_conv_b[l], Lc), _hy_filter(Lc, *hy_params, Lc),
                                hy_bias[l], tabs_c, cb).reshape(B * Lc, HY_W)
            ctx2, h2c = _merge(ctx2, yc_att.reshape(B * Lc, ATT_W), yc_sc, yc_hy, pc, cgt1, g2, csc2, csh2,
                               wa, ws, wh, wo, Lc, Lc)
            tok = jnp.concatenate([h2, h2c], axis=0)
        else:
            tok = h2

        u_bf = peer_u[l].astype(BF16)
        vT_bf = peer_v[l].astype(BF16).reshape(-1, PEER_BLOCK * PEER_NKEYS, D).transpose(0, 2, 1)
        fT = _peer(tok, peer_wq[l], peer_subkeys[l], u_bf, vT_bf, min(TC_PEER, tok.shape[0]), PEER_BLOCK)
        last = l == DEPTH - 1
        x2 = _resid(x2, fT, 0, gt2, g_final[None, :], S, tr, last)
        if need_ctx:
            ctx2 = _resid(ctx2, fT, B * S, cgt2, g_final[None, :], Lc, Lc, False)

    return x2.reshape(B, S, D)
```

```python
import functools
import math

import jax
import jax.numpy as jnp
from jax import lax
from jax.experimental import pallas as pl
from jax.experimental.pallas import tpu as pltpu

F32 = jnp.float32
BF16 = jnp.bfloat16

DEPTH = 2
GRID_W = 64
EPS = 1e-6
N_HEADS = 8
N_KV_HEADS = 2
GQA_GROUP = N_HEADS // N_KV_HEADS
HEAD_DIM = 64
AXIS_DIM = HEAD_DIM // 2
ATT_W = N_HEADS * HEAD_DIM
KV_W = N_KV_HEADS * HEAD_DIM
ATT_SCALE = HEAD_DIM ** -0.5
ROPE_THETA = 10000.0
SC_W = 512
HY_W = 512
HY_ORDER = 2
HY_BANDS = 16
PEER_HEADS = 8
PEER_NKEYS = 128
PEER_DK = 128
PEER_TOPK = 16
D_MODEL = 1024
QKV_W = ATT_W + 2 * KV_W
REF_SC0 = QKV_W
REF_GT0 = QKV_W + 3 * SC_W + (HY_ORDER + 1) * HY_W
GT0 = 0
SC0 = GT0 + 3 * D_MODEL
HY0 = SC0 + 3 * SC_W
Q0 = HY0 + (HY_ORDER + 1) * HY_W
K0 = Q0 + ATT_W
V0 = K0 + KV_W
IN_W = V0 + KV_W
HALO = 8
HY_N2 = 256
HY_FEAT = 128

VMEM_BYTES_V7X = 64 * 1024 * 1024
VMEM_LIMIT = VMEM_BYTES_V7X - 8 * 1024 * 1024
MXU_WIDTH_V7X = 256
LANES = 128
SUBLANES = 8
PACKED_ROWS = 16

TM_IN_PROJ = 2048
TM_MATMUL = 1024
TM_ROWS = 1024
TQ_ATT = 512
TK_ATT_MAX = 2048
TC_PEER = 512
DFT_COLS = 32


def _cparams(sem):
    return pltpu.CompilerParams(dimension_semantics=sem, vmem_limit_bytes=VMEM_LIMIT)


def _key_block(lk):
    for unit in (MXU_WIDTH_V7X, LANES):
        cands = [t for t in range(unit, min(lk, TK_ATT_MAX) + 1, unit) if lk % t == 0]
        if cands:
            return max(cands)
    raise ValueError(f"no lane-aligned key block divides {lk}")


def _in_proj_kernel(x_ref, g_ref, sc_ref, sh_ref, w_ref, o_ref, h_scr):
    @pl.when(pl.program_id(1) == 0)
    def _():
        x = x_ref[...]
        ms = jnp.mean(x * x, axis=-1, keepdims=True)
        y = x * lax.rsqrt(ms + EPS) * g_ref[...]
        h_scr[...] = (y * (1.0 + sc_ref[0]) + sh_ref[0]).astype(BF16)

    o_ref[...] = jnp.dot(h_scr[...], w_ref[...], preferred_element_type=F32).astype(o_ref.dtype)


def _in_proj(x2d, g, sc, sh, w, rows_per_batch, tm, tn):
    T, D = x2d.shape
    N = w.shape[1]
    tpb = rows_per_batch // tm
    return pl.pallas_call(
        _in_proj_kernel,
        grid=(T // tm, N // tn),
        in_specs=[
            pl.BlockSpec((tm, D), lambda i, j: (i, 0)),
            pl.BlockSpec((1, D), lambda i, j: (0, 0)),
            pl.BlockSpec((1, 1, D), lambda i, j: (i // tpb, 0, 0)),
            pl.BlockSpec((1, 1, D), lambda i, j: (i // tpb, 0, 0)),
            pl.BlockSpec((D, tn), lambda i, j: (0, j)),
        ],
        out_specs=pl.BlockSpec((tm, tn), lambda i, j: (i, j)),
        out_shape=jax.ShapeDtypeStruct((T, N), BF16),
        scratch_shapes=[pltpu.VMEM((tm, D), BF16)],
        compiler_params=_cparams(("parallel", "arbitrary")),
        name="in_proj",
    )(x2d, g, sc, sh, w)


def _attn_kernel(q_ref, kT_ref, v_ref, o_ref, q_scr, m_scr, acc_scr, sa_scr, sb_scr, *, nkb):
    tq = q_ref.shape[1]
    m_scr[...] = jnp.full(m_scr.shape, -jnp.inf, F32)
    acc_scr[...] = jnp.zeros(acc_scr.shape, F32)
    for g in range(GQA_GROUP):
        q_scr[g * tq:(g + 1) * tq, :] = q_ref[0, :, g * HEAD_DIM:(g + 1) * HEAD_DIM]

    def scores(j, dst_scr):
        dst_scr[...] = jnp.dot(q_scr[...], kT_ref[0, 0, j], preferred_element_type=F32)

    def update(j, src_scr):
        s = src_scr[...]
        m_prev = m_scr[...]
        m_new = jnp.maximum(m_prev, jnp.max(s, axis=-1, keepdims=True))
        p = jnp.exp(s - m_new).astype(BF16)
        alpha = jnp.exp(m_prev - m_new)
        acc_scr[...] = acc_scr[...] * alpha + jnp.dot(p, v_ref[0, 0, j], preferred_element_type=F32)
        m_scr[...] = m_new

    scores(0, sa_scr)

    def pair(i, carry):
        j = 2 * i
        scores(j + 1, sb_scr)
        update(j, sa_scr)
        scores(j + 2, sa_scr)
        update(j + 1, sb_scr)
        return carry

    lax.fori_loop(0, (nkb - 1) // 2, pair, 0)
    if nkb % 2 == 1:
        update(nkb - 1, sa_scr)
    else:
        scores(nkb - 1, sb_scr)
        update(nkb - 2, sa_scr)
        update(nkb - 1, sb_scr)
    acc = acc_scr[...]
    o = (acc[:, :HEAD_DIM] / acc[:, HEAD_DIM:HEAD_DIM + 1]).astype(o_ref.dtype)
    o_ref[0] = jnp.concatenate([o[g * tq:(g + 1) * tq] for g in range(GQA_GROUP)], axis=1)


def _attention(q, k, v, tq, tk):
    B, Lq, _ = q.shape
    Lk = k.shape[1]
    nqb, nkb = Lq // tq, Lk // tk
    R = GQA_GROUP * tq
    GW = GQA_GROUP * HEAD_DIM
    kT = k.reshape(B, nkb, tk, N_KV_HEADS, HEAD_DIM).transpose(0, 3, 1, 4, 2)
    vb = v.reshape(B, nkb, tk, N_KV_HEADS, HEAD_DIM).transpose(0, 3, 1, 2, 4)
    ones = jnp.ones(vb.shape[:-1] + (1,), BF16)
    zeros = jnp.zeros(vb.shape[:-1] + (HEAD_DIM - 1,), BF16)
    vb = jnp.concatenate([vb, ones, zeros], axis=-1)
    return pl.pallas_call(
        functools.partial(_attn_kernel, nkb=nkb),
        grid=(B, N_KV_HEADS, nqb),
        in_specs=[
            pl.BlockSpec((1, tq, GW), lambda b, h, i: (b, i, h)),
            pl.BlockSpec((1, 1, nkb, HEAD_DIM, tk), lambda b, h, i: (b, h, 0, 0, 0)),
            pl.BlockSpec((1, 1, nkb, tk, 2 * HEAD_DIM), lambda b, h, i: (b, h, 0, 0, 0)),
        ],
        out_specs=pl.BlockSpec((1, tq, GW), lambda b, h, i: (b, i, h)),
        out_shape=jax.ShapeDtypeStruct((B, Lq, ATT_W), BF16),
        scratch_shapes=[pltpu.VMEM((R, HEAD_DIM), BF16), pltpu.VMEM((R, 1), F32), pltpu.VMEM((R, 2 * HEAD_DIM), F32),
                        pltpu.VMEM((R, tk), F32), pltpu.VMEM((R, tk), F32)],
        compiler_params=_cparams(("parallel", "parallel", "arbitrary")),
        name="attention",
    )(q, kT, vb)


def _merge_kernel(x_ref, ya_ref, ys_ref, yh_ref, ga_ref, gs_ref, gh_ref, gt_ref, g2_ref, sc_ref, sh_ref,
                  wa_ref, ws_ref, wh_ref, wo_ref, xo_ref, h2_ref):
    def br(y_ref, g_ref, w_ref):
        gate = jax.nn.sigmoid(g_ref[...].astype(F32))
        return gate * jnp.dot(y_ref[...], w_ref[...], preferred_element_type=F32)

    m = br(ya_ref, ga_ref, wa_ref) + br(ys_ref, gs_ref, ws_ref) + br(yh_ref, gh_ref, wh_ref)
    o = jnp.dot(m.astype(BF16), wo_ref[...], preferred_element_type=F32)
    xn = x_ref[...] + gt_ref[0] * o
    xo_ref[...] = xn
    ms = jnp.mean(xn * xn, axis=-1, keepdims=True)
    y = xn * lax.rsqrt(ms + EPS) * g2_ref[...]
    h2_ref[...] = (y * (1.0 + sc_ref[0]) + sh_ref[0]).astype(BF16)


def _merge(x2d, ya, ys, yh, p, gt1, g2, sc2, sh2, wa, ws, wh, wo, rows_per_batch, tm):
    T, D = x2d.shape
    tpb = rows_per_batch // tm
    gblk = GT0 // D
    row = lambda i: (i, 0)
    mod = lambda i: (i // tpb, 0, 0)
    full = lambda i: (0, 0)
    return pl.pallas_call(
        _merge_kernel,
        grid=(T // tm,),
        in_specs=[
            pl.BlockSpec((tm, D), row),
            pl.BlockSpec((tm, ATT_W), row),
            pl.BlockSpec((tm, SC_W), row),
            pl.BlockSpec((tm, HY_W), row),
            pl.BlockSpec((tm, D), lambda i: (i, gblk)),
            pl.BlockSpec((tm, D), lambda i: (i, gblk + 1)),
            pl.BlockSpec((tm, D), lambda i: (i, gblk + 2)),
            pl.BlockSpec((1, 1, D), mod),
            pl.BlockSpec((1, D), full),
            pl.BlockSpec((1, 1, D), mod),
            pl.BlockSpec((1, 1, D), mod),
            pl.BlockSpec((ATT_W, D), full),
            pl.BlockSpec((SC_W, D), full),
            pl.BlockSpec((HY_W, D), full),
            pl.BlockSpec((D, D), full),
        ],
        out_specs=[pl.BlockSpec((tm, D), row), pl.BlockSpec((tm, D), row)],
        out_shape=[jax.ShapeDtypeStruct((T, D), F32), jax.ShapeDtypeStruct((T, D), BF16)],
        compiler_params=_cparams(("parallel",)),
        name="merge",
    )(x2d, ya, ys, yh, p, p, p, gt1, g2, sc2, sh2, wa, ws, wh, wo)


PEER_BLOCK = 8
PEER_MM = 8
PEER_ACC = 2


def _peer_kernel(hT_ref, r2_ref, e2_ref, n_ref, e1_ref, u_ref, vT_ref, o_ref, gwa_scr, gwb_scr, act_scr, *, nb):
    e = pl.program_id(1)
    tc = hT_ref.shape[1]

    @pl.when(e == 0)
    def _():
        o_ref[...] = jnp.zeros(o_ref.shape, F32)
        gwb_scr[...] = jnp.zeros(gwb_scr.shape, BF16)

    def step(cur_scr, prev_scr):
        blk = jnp.minimum(e, pl.num_programs(1) - 2)
        shape3 = (PEER_NKEYS // PACKED_ROWS, PACKED_ROWS, LANES)
        for g0 in range(0, nb, PEER_MM):
            grows = slice(g0 * PEER_NKEYS, (g0 + PEER_MM) * PEER_NKEYS)
            o_ref[...] += jnp.dot(vT_ref[0, :, grows], prev_scr[grows, :], preferred_element_type=F32)
            for lt in range(tc // LANES):
                lanes = slice(lt * LANES, (lt + 1) * LANES)
                for i0 in range(g0, g0 + PEER_MM, PEER_ACC):
                    ws = [jnp.zeros(shape3, BF16) for _ in range(PEER_ACC)]
                    for h in range(PEER_HEADS):
                        r2t = pltpu.bitcast(r2_ref[h, :, lanes], BF16).reshape(shape3)
                        e2t = pltpu.bitcast(e2_ref[h, :, lanes], BF16).reshape(shape3)
                        for k in range(PEER_ACC):
                            ii = i0 + k
                            cnt = jnp.broadcast_to(n_ref[h, blk, ii:ii + 1, lanes], (PACKED_ROWS, LANES)).astype(BF16)
                            e1row = jnp.broadcast_to(e1_ref[h, blk, ii:ii + 1, lanes],
                                                     (PACKED_ROWS, LANES)).astype(BF16)
                            ws[k] = ws[k] + jnp.where(r2t < cnt[None], e2t, jnp.zeros_like(e2t)) * e1row[None]
                    for k in range(PEER_ACC):
                        rows = slice((i0 + k) * PEER_NKEYS, (i0 + k + 1) * PEER_NKEYS)
                        cur_scr[rows, lanes] = ws[k].reshape(PEER_NKEYS, LANES)
            a = jnp.dot(u_ref[grows, :], hT_ref[...], preferred_element_type=F32)
            act = (0.5 * a * (1.0 + lax.erf(a * (2.0 ** -0.5)))).astype(BF16)
            cur_scr[grows, :] = cur_scr[grows, :] * act

    @pl.when(e % 2 == 0)
    def _():
        step(gwa_scr, gwb_scr)

    @pl.when(e % 2 == 1)
    def _():
        step(gwb_scr, gwa_scr)


def _peer_dense(hT, r2, e2, cnt, e1, u, vT, tc, nb):
    D, T = hT.shape
    N = u.shape[0]
    eb = nb * PEER_NKEYS
    ne = N // eb
    assert ne % 2 == 0, "the drain step must find the last block in the buffer the parity rule reads"
    cur = lambda e: jnp.minimum(e, ne - 1)
    return pl.pallas_call(
        functools.partial(_peer_kernel, nb=nb),
        grid=(T // tc, ne + 1),
        in_specs=[
            pl.BlockSpec((D, tc), lambda t, e: (0, t)),
            pl.BlockSpec((PEER_HEADS, PEER_NKEYS // 2, tc), lambda t, e: (0, 0, t)),
            pl.BlockSpec((PEER_HEADS, PEER_NKEYS // 2, tc), lambda t, e: (0, 0, t)),
            pl.BlockSpec((PEER_HEADS, ne, nb, tc), lambda t, e: (0, 0, 0, t)),
            pl.BlockSpec((PEER_HEADS, ne, nb, tc), lambda t, e: (0, 0, 0, t)),
            pl.BlockSpec((eb, D), lambda t, e: (cur(e), 0)),
            pl.BlockSpec((1, D, eb), lambda t, e: (jnp.maximum(e - 1, 0), 0, 0)),
        ],
        out_specs=pl.BlockSpec((D, tc), lambda t, e: (0, t)),
        out_shape=jax.ShapeDtypeStruct((D, T), F32),
        scratch_shapes=[pltpu.VMEM((eb, tc), BF16), pltpu.VMEM((eb, tc), BF16), pltpu.VMEM((eb, tc), BF16)],
        compiler_params=_cparams(("parallel", "arbitrary")),
        name="peer_dense",
    )(hT, r2, e2, cnt, e1, u, vT)


ROUTE_ROWS = 24
NEG_INF = float("-inf")
ROUTE_UNROLL = 4


def _sort_network(n):
    pairs = []
    p = 1
    while p < n:
        k = p
        while k >= 1:
            for j in range(k % p, n - k, 2 * k):
                for i in range(min(k, n - j - k)):
                    if (i + j) // (2 * p) == (i + j + k) // (2 * p):
                        pairs.append((i + j, i + j + k))
            k //= 2
        p *= 2
    return pairs


def _top_rows_sorted(s, n):
    tc = s.shape[1]
    levels = s.shape[0] // SUBLANES
    col = [s[r * SUBLANES:(r + 1) * SUBLANES] for r in range(levels)]
    for a, b in _sort_network(1 << (levels - 1).bit_length()):
        if b < levels:
            col[a], col[b] = jnp.maximum(col[a], col[b]), jnp.minimum(col[a], col[b])
    rid = lax.broadcasted_iota(jnp.int32, (ROUTE_ROWS, tc), 0)
    packed = jnp.full((ROUTE_ROWS, tc), NEG_INF, F32)
    for k in range(n):
        m = jnp.max(col[0], axis=0, keepdims=True)
        packed = jnp.where(rid == k, m, packed)
        hit = col[0] == m
        live = min(levels, n - k)
        for r in range(live):
            below = col[r + 1] if r + 1 < levels else NEG_INF
            col[r] = jnp.where(hit, below, col[r])
    return packed


def _row_penalty(tc, lo, hi):
    rid = lax.broadcasted_iota(jnp.int32, (8, tc), 0)
    return jnp.where((rid >= lo) & (rid < hi), 0.0, NEG_INF).astype(F32)


def _route_kernel(hT_ref, wqh_ref, wql_ref, sk_ref, n_ref, e1_ref, r2_ref, e2_ref, q_scr):
    hT = hT_ref[...]
    q_scr[...] = (jnp.dot(wqh_ref[...], hT, preferred_element_type=F32)
                  + jnp.dot(wql_ref[...], hT, preferred_element_type=F32))
    tc = hT.shape[1]
    n = PEER_TOPK + 1
    half = PEER_DK // 2

    def head(h, carry):
        r0 = pl.multiple_of(h * PEER_DK, PEER_DK)
        hp = lax.Precision.HIGHEST
        s1 = jnp.dot(sk_ref[h, 0], q_scr[pl.ds(r0, half), :], preferred_element_type=F32, precision=hp)
        s2 = jnp.dot(sk_ref[h, 1], q_scr[pl.ds(r0 + half, half), :], preferred_element_type=F32, precision=hp)
        v1 = _top_rows_sorted(s1, n)
        v2 = _top_rows_sorted(s2, n)
        tiles = [v1[0:1] + v2[0:8], v1[0:1] + v2[8:16], v1[0:1] + v2[16:24], v1[1:2] + v2[0:8],
                 v1[2:3] + v2[0:8] + _row_penalty(tc, 0, n // 3), v1[3:4] + v2[0:8] + _row_penalty(tc, 0, n // 4),
                 v2[0:1] + v1[0:8] + _row_penalty(tc, 4, 8), v2[0:1] + v1[8:16], v2[0:1] + v1[16:24],
                 v2[1:2] + v1[0:8] + _row_penalty(tc, 4, n // 2), v2[2:3] + v1[0:8] + _row_penalty(tc, 4, n // 3)]
        top = _top_rows_sorted(jnp.concatenate(tiles, axis=0), n)
        theta = 0.5 * (top[PEER_TOPK - 1:PEER_TOPK] + top[PEER_TOPK:PEER_TOPK + 1])
        z = jnp.sum(jnp.exp(top[0:16] - top[0:1]), axis=0, keepdims=True)
        c = theta - s1
        cnt = jnp.zeros(c.shape, F32)
        rank = jnp.zeros(c.shape, F32)
        for b in range(n):
            cnt = jnp.where(v2[b:b + 1] >= c, float(b + 1), cnt)
            rank = jnp.where(v2[b:b + 1] > s2, float(b + 1), rank)
        n_ref[h] = cnt
        e1_ref[h] = jnp.exp(s1 - v1[0:1]) / z
        r2_ref[h] = pltpu.bitcast(rank.astype(BF16), jnp.uint32)
        e2_ref[h] = pltpu.bitcast(jnp.exp(s2 - v2[0:1]).astype(BF16), jnp.uint32)
        return carry

    def heads(i, c):
        for k in range(ROUTE_UNROLL):
            c = head(ROUTE_UNROLL * i + k, c)
        return c

    lax.fori_loop(0, PEER_HEADS // ROUTE_UNROLL, heads, 0)


def _peer_route(hT, wqT_hi, wqT_lo, subkeys, tc):
    D, T = hT.shape
    W = wqT_hi.shape[0]
    out = jax.ShapeDtypeStruct((PEER_HEADS, PEER_NKEYS, T), F32)
    out16 = jax.ShapeDtypeStruct((PEER_HEADS, PEER_NKEYS // 2, T), jnp.uint32)
    ospec = pl.BlockSpec((PEER_HEADS, PEER_NKEYS, tc), lambda t: (0, 0, t))
    ospec16 = pl.BlockSpec((PEER_HEADS, PEER_NKEYS // 2, tc), lambda t: (0, 0, t))
    return pl.pallas_call(
        _route_kernel,
        grid=(T // tc,),
        in_specs=[
            pl.BlockSpec((D, tc), lambda t: (0, t)),
            pl.BlockSpec((W, D), lambda t: (0, 0)),
            pl.BlockSpec((W, D), lambda t: (0, 0)),
            pl.BlockSpec(subkeys.shape, lambda t: (0, 0, 0, 0)),
        ],
        out_specs=[ospec, ospec, ospec16, ospec16],
        out_shape=[out, out, out16, out16],
        scratch_shapes=[pltpu.VMEM((W, tc), F32)],
        compiler_params=_cparams(("parallel",)),
        name="peer_route",
    )(hT, wqT_hi, wqT_lo, subkeys)


def _rope_tables(L, rope):
    if not rope:
        return jnp.ones((L, 2 * HEAD_DIM), F32), jnp.zeros((L, 2 * HEAD_DIM), F32)
    pos = jnp.arange(L, dtype=jnp.int32)
    row = (pos // GRID_W).astype(F32)
    col = (pos % GRID_W).astype(F32)
    inv = jnp.power(ROPE_THETA, -jnp.arange(0, AXIS_DIM, 2, dtype=F32) / AXIS_DIM)
    ar = row[:, None] * inv[None, :]
    ac = col[:, None] * inv[None, :]
    cos = jnp.concatenate([jnp.cos(ar), jnp.cos(ar), jnp.cos(ac), jnp.cos(ac)], axis=-1)
    sin = jnp.concatenate([-jnp.sin(ar), jnp.sin(ar), -jnp.sin(ac), jnp.sin(ac)], axis=-1)
    return jnp.tile(cos, (1, 2)), jnp.tile(sin, (1, 2))


def _qk_prep_kernel(p_ref, cos_ref, sin_ref, qg_ref, kg_ref, bdq_ref, bdk_ref, q_ref, k_ref):
    x = p_ref[...].astype(F32)
    cos = cos_ref[...]
    sin = sin_ref[...]
    quarter = AXIS_DIM // 2

    def prep(xh, gain, bd):
        w = xh.shape[1]
        sq = xh * xh
        hi = sq.astype(BF16)
        lo = (sq - hi.astype(F32)).astype(BF16)
        ms = jnp.dot(hi, bd, preferred_element_type=F32) + jnp.dot(lo, bd, preferred_element_type=F32)
        y = xh * lax.rsqrt(ms + EPS) * gain
        lane = lax.broadcasted_iota(jnp.int32, y.shape, 1)
        first = (lane & (AXIS_DIM - 1)) < quarter
        partner = jnp.where(first, pltpu.roll(y, w - quarter, 1), pltpu.roll(y, quarter, 1))
        reps = w // cos.shape[1]
        c = jnp.concatenate([cos] * reps, axis=1) if reps > 1 else cos
        s = jnp.concatenate([sin] * reps, axis=1) if reps > 1 else sin
        return y * c + partner * s

    q_ref[...] = (prep(x[:, :ATT_W], qg_ref[...], bdq_ref[...]) * ATT_SCALE).astype(BF16)
    k_ref[...] = prep(x[:, ATT_W:ATT_W + KV_W], kg_ref[...], bdk_ref[...]).astype(BF16)


def _qk_prep(p, cos, sin, q_gain, k_gain, rows_per_batch, tm):
    T = p.shape[0]
    tpb = rows_per_batch // tm
    qg = jnp.tile(q_gain, N_HEADS)[None, :]
    kg = jnp.tile(k_gain, N_KV_HEADS)[None, :]

    def block_avg(w):
        hid = jnp.arange(w) // HEAD_DIM
        return jnp.where(hid[:, None] == hid[None, :], 1.0 / HEAD_DIM, 0.0).astype(BF16)

    full = lambda i: (0, 0)
    return pl.pallas_call(
        _qk_prep_kernel,
        grid=(T // tm,),
        in_specs=[
            pl.BlockSpec((tm, QKV_W), lambda i: (i, Q0 // QKV_W)),
            pl.BlockSpec((tm, 2 * HEAD_DIM), lambda i: (i % tpb, 0)),
            pl.BlockSpec((tm, 2 * HEAD_DIM), lambda i: (i % tpb, 0)),
            pl.BlockSpec((1, ATT_W), full),
            pl.BlockSpec((1, KV_W), full),
            pl.BlockSpec((ATT_W, ATT_W), full),
            pl.BlockSpec((KV_W, KV_W), full),
        ],
        out_specs=[pl.BlockSpec((tm, ATT_W), lambda i: (i, 0)), pl.BlockSpec((tm, KV_W), lambda i: (i, 0))],
        out_shape=[jax.ShapeDtypeStruct((T, ATT_W), BF16), jax.ShapeDtypeStruct((T, KV_W), BF16)],
        compiler_params=_cparams(("parallel",)),
        name="qk_prep",
    )(p, cos, sin, qg, kg, block_avg(ATT_W), block_avg(KV_W))


def _shift_rows(cur, prev8, next8, first, last):
    tm = cur.shape[0]
    rid = lax.broadcasted_iota(jnp.int32, cur.shape, 0)
    pr = jnp.where(first, 0.0, prev8[HALO - 1:HALO, :])
    nx = jnp.where(last, 0.0, next8[0:1, :])
    up = jnp.where(rid == 0, pr, pltpu.roll(cur, 1, 0))
    dn = jnp.where(rid == tm - 1, nx, pltpu.roll(cur, tm - 1, 0))
    return up, dn


def _halo_specs(tm, width, col_block, n_rows):
    r = tm // HALO
    last = n_rows // HALO - 1
    prev = pl.BlockSpec((1, HALO, width), lambda b, i: (b, jnp.maximum(i * r - 1, 0), col_block))
    nxt = pl.BlockSpec((1, HALO, width), lambda b, i: (b, jnp.minimum((i + 1) * r, last), col_block))
    return prev, nxt


def _short_conv_kernel(bg_ref, cg_ref, xs_ref, cgp_ref, xsp_ref, cgn_ref, xsn_ref, w_ref, o_ref):
    i = pl.program_id(1)
    f = lambda r: r[0].astype(F32)
    cur = f(cg_ref) * f(xs_ref)
    up, dn = _shift_rows(cur, f(cgp_ref) * f(xsp_ref), f(cgn_ref) * f(xsn_ref), i == 0, i == pl.num_programs(1) - 1)
    o_ref[0] = (f(bg_ref) * (up * w_ref[0:1] + cur * w_ref[1:2] + dn * w_ref[2:3])).astype(o_ref.dtype)


def _short_conv(p3, w, tm):
    B, L, _ = p3.shape
    c0 = SC0 // SC_W
    blk = lambda j: pl.BlockSpec((1, tm, SC_W), lambda b, i: (b, i, c0 + j))
    cgp, cgn = _halo_specs(tm, SC_W, c0 + 1, L)
    xsp, xsn = _halo_specs(tm, SC_W, c0 + 2, L)
    return pl.pallas_call(
        _short_conv_kernel,
        grid=(B, L // tm),
        in_specs=[blk(0), blk(1), blk(2), cgp, xsp, cgn, xsn, pl.BlockSpec((3, SC_W), lambda b, i: (0, 0))],
        out_specs=pl.BlockSpec((1, tm, SC_W), lambda b, i: (b, i, 0)),
        out_shape=jax.ShapeDtypeStruct((B, L, SC_W), BF16),
        compiler_params=_cparams(("parallel", "parallel")),
        name="short_conv",
    )(p3, p3, p3, p3, p3, p3, p3, w)


def _hy_pre_kernel(p_ref, pp_ref, pn_ref, w_ref, b_ref, v_ref, x1_ref, x2_ref):
    i = pl.program_id(1)
    cur = p_ref[0].astype(F32)
    up, dn = _shift_rows(cur, pp_ref[0].astype(F32), pn_ref[0].astype(F32), i == 0, i == pl.num_programs(1) - 1)
    u = up * w_ref[0:1] + cur * w_ref[1:2] + dn * w_ref[2:3] + b_ref[...]
    v_ref[0] = u[:, :HY_W].astype(BF16)
    x1_ref[0] = u[:, HY_W:2 * HY_W].astype(BF16)
    x2_ref[0] = u[:, 2 * HY_W:].astype(BF16)


def _hy_pre(p3, w, b, tm):
    B, L, _ = p3.shape
    W = (HY_ORDER + 1) * HY_W
    c0 = HY0 // W
    prev, nxt = _halo_specs(tm, W, c0, L)
    out = jax.ShapeDtypeStruct((B, L, HY_W), BF16)
    ospec = pl.BlockSpec((1, tm, HY_W), lambda b_, i: (b_, i, 0))
    return pl.pallas_call(
        _hy_pre_kernel,
        grid=(B, L // tm),
        in_specs=[pl.BlockSpec((1, tm, W), lambda b_, i: (b_, i, c0)), prev, nxt,
                  pl.BlockSpec((3, W), lambda b_, i: (0, 0)), pl.BlockSpec((1, W), lambda b_, i: (0, 0))],
        out_specs=[ospec, ospec, ospec],
        out_shape=[out, out, out],
        compiler_params=_cparams(("parallel", "parallel")),
        name="hyena_pre",
    )(p3, p3, p3, w, b[None, :])


def _hy_features(L):
    n = jnp.arange(2 * L, dtype=jnp.int32)
    j = jnp.where(n < L, n, jnp.where(n == L, 0, 2 * L - n)).astype(F32)[:, None]
    t = j / (L - 1)
    w = 2.0 * math.pi * j / L
    bands = jnp.linspace(1e-4, HY_BANDS - 1, HY_BANDS, dtype=F32)[None, :]
    z = jnp.concatenate([t, jnp.cos(bands * w), -jnp.sin(bands * w)], axis=-1)
    return jnp.pad(z, ((0, 0), (0, HY_FEAT - z.shape[1])))


def _hy_filter_kernel(z_ref, w1_ref, b1_ref, f1_ref, w2_ref, b2_ref, f2_ref, w3_ref, dec_ref,
                      k0_ref, k1_ref, sum_ref):
    i = pl.program_id(0)
    half = pl.num_programs(0) // 2

    @pl.when(i == 0)
    def _():
        sum_ref[...] = jnp.zeros(sum_ref.shape, F32)

    hp = lax.Precision.HIGHEST
    z = z_ref[...]
    h = jnp.sin(f1_ref[...] * (jnp.dot(z, w1_ref[...], preferred_element_type=F32, precision=hp) + b1_ref[...]))
    h = jnp.sin(f2_ref[...] * (jnp.dot(h, w2_ref[...], preferred_element_type=F32, precision=hp) + b2_ref[...]))
    k = jnp.dot(h, w3_ref[0], preferred_element_type=F32, precision=hp)
    k = k * jnp.exp(-z[:, 0:1] * dec_ref[...])
    sum_ref[...] += jnp.sum(jnp.abs(k), axis=0, keepdims=True)
    rid = lax.broadcasted_iota(jnp.int32, k.shape, 0)
    k = jnp.where(i == half, jnp.where(rid == 0, 0.0, k), k)
    k0_ref[...] = k[:, :HY_W].astype(BF16)
    k1_ref[...] = k[:, HY_W:].astype(BF16)


def _hy_filter(L, w1, b1, f1, w2, b2, f2, w3, decay, nb):
    N = 2 * L
    fh = w1.shape[1]
    z = _hy_features(L)
    w1p = jnp.pad(w1, ((0, HY_FEAT - w1.shape[0]), (0, 0)))
    w3d = w3.reshape(fh, HY_ORDER, 2, HY_W).transpose(2, 0, 1, 3).reshape(2, fh, HY_ORDER * HY_W)
    dec = jnp.abs(decay).reshape(1, HY_ORDER * HY_W)
    row = lambda a: a[None, :]
    full = lambda i: (0, 0)
    half = N // nb // 2
    k0, k1, tot = pl.pallas_call(
        _hy_filter_kernel,
        grid=(N // nb,),
        in_specs=[
            pl.BlockSpec((nb, HY_FEAT), lambda i: (i, 0)),
            pl.BlockSpec((HY_FEAT, fh), full), pl.BlockSpec((1, fh), full), pl.BlockSpec((1, fh), full),
            pl.BlockSpec((fh, fh), full), pl.BlockSpec((1, fh), full), pl.BlockSpec((1, fh), full),
            pl.BlockSpec((1, fh, HY_ORDER * HY_W), lambda i: (i // half, 0, 0)),
            pl.BlockSpec((1, HY_ORDER * HY_W), full),
        ],
        out_specs=[pl.BlockSpec((nb, HY_W), lambda i: (i, 0)), pl.BlockSpec((nb, HY_W), lambda i: (i, 0)),
                   pl.BlockSpec((1, HY_ORDER * HY_W), full)],
        out_shape=[jax.ShapeDtypeStruct((N, HY_W), BF16), jax.ShapeDtypeStruct((N, HY_W), BF16),
                   jax.ShapeDtypeStruct((1, HY_ORDER * HY_W), F32)],
        compiler_params=_cparams(("arbitrary",)),
        name="hyena_filter",
    )(z, w1p, row(b1), row(f1), w2, row(b2), row(f2), w3d, dec)
    return k0, k1, 1.0 / tot


def _dft_tables(N1, N2):
    N = N1 * N2
    ar = lambda n: jnp.arange(n, dtype=jnp.int32)

    def cs(m, period):
        ang = (-2.0 * math.pi / period) * (m % period).astype(F32)
        return jnp.cos(ang), jnp.sin(ang)

    fr, fi = cs(ar(N1)[:, None] * ar(N1)[None, :], N1)
    hr, hi = fr[:, :max(N1 // 2, 1)], fi[:, :max(N1 // 2, 1)]
    w_fwd = jnp.block([[hr, -hi], [hi, hr]])
    w_real = jnp.concatenate([fr, fi], axis=0)
    w_inv = jnp.block([[hr.T, hi.T], [-hi.T, hr.T]]) / N1
    f2r, f2i = cs(ar(N2)[:, None] * ar(N2)[None, :], N2)
    tr, ti = cs(ar(N1)[:, None] * ar(N2)[None, :], N)
    gr = f2r[None] * tr[:, None, :] - f2i[None] * ti[:, None, :]
    gi = f2r[None] * ti[:, None, :] + f2i[None] * tr[:, None, :]
    g = jnp.concatenate([jnp.concatenate([gr, -gi], axis=2), jnp.concatenate([gi, gr], axis=2)], axis=1)
    grt, git = gr.transpose(0, 2, 1), gi.transpose(0, 2, 1)
    gh = jnp.concatenate([jnp.concatenate([grt, git], axis=2), jnp.concatenate([-git, grt], axis=2)], axis=1) / N2
    return tuple(a.astype(BF16) for a in (w_fwd, w_real, w_inv, g, gh))


def _colmm_kernel(w_ref, x_ref, o_ref):
    o_ref[...] = jnp.dot(w_ref[...], x_ref[...], preferred_element_type=F32).astype(o_ref.dtype)


def _colmm(w, x, cb):
    M, K = w.shape
    C = x.shape[1]
    return pl.pallas_call(
        _colmm_kernel,
        grid=(C // cb,),
        in_specs=[pl.BlockSpec((M, K), lambda j: (0, 0)), pl.BlockSpec((K, cb), lambda j: (0, j))],
        out_specs=pl.BlockSpec((M, cb), lambda j: (0, j)),
        out_shape=jax.ShapeDtypeStruct((M, C), BF16),
        compiler_params=_cparams(("parallel",)),
        name="dft_outer",
    )(w, x)


def _colmm_gate_kernel(w_ref, d_ref, u_ref, g_ref, b_ref, o_ref):
    y = jnp.dot(w_ref[...], d_ref[...], preferred_element_type=F32)
    o_ref[...] = (g_ref[...].astype(F32) * (y + u_ref[...].astype(F32) * b_ref[...])).astype(o_ref.dtype)


def _colmm_gate(w, d, u, gate, bias_row, cb):
    M, K = w.shape
    C = d.shape[1]
    col = lambda j: (0, j)
    return pl.pallas_call(
        _colmm_gate_kernel,
        grid=(C // cb,),
        in_specs=[pl.BlockSpec((M, K), lambda j: (0, 0)), pl.BlockSpec((K, cb), col), pl.BlockSpec((M, cb), col),
                  pl.BlockSpec((M, cb), col), pl.BlockSpec((1, cb), lambda j: (0, 0))],
        out_specs=pl.BlockSpec((M, cb), col),
        out_shape=jax.ShapeDtypeStruct((M, C), BF16),
        compiler_params=_cparams(("parallel",)),
        name="dft_outer_gate",
    )(w, d, u, gate, bias_row)


def _mid_fwd_kernel(a_ref, g_ref, s_ref, k_ref):
    n2 = a_ref.shape[2]
    a = a_ref[:, 0].reshape(2 * n2, a_ref.shape[3])
    x = jnp.dot(g_ref[0], a, preferred_element_type=F32) * s_ref[...]
    k_ref[:, 0] = x.reshape(2, n2, x.shape[1]).astype(k_ref.dtype)


def _mid_fwd(a, g, scale):
    _, N1, N2, C = a.shape
    blk = pl.BlockSpec((2, 1, N2, C), lambda i: (0, i, 0, 0))
    return pl.pallas_call(
        _mid_fwd_kernel,
        grid=(N1,),
        in_specs=[blk, pl.BlockSpec((1, 2 * N2, 2 * N2), lambda i: (i, 0, 0)), pl.BlockSpec((1, C), lambda i: (0, 0))],
        out_specs=blk,
        out_shape=jax.ShapeDtypeStruct((2, N1, N2, C), BF16),
        compiler_params=_cparams(("parallel",)),
        name="dft_inner_filter",
    )(a, g, scale)


def _mid_kernel(a_ref, g_ref, gh_ref, k_ref, d_ref):
    n2 = a_ref.shape[2]
    c = a_ref.shape[3]
    a = a_ref[:, 0].reshape(2 * n2, c)
    x = jnp.dot(g_ref[0], a, preferred_element_type=F32)
    xr, xi = x[:n2], x[n2:]
    kr, ki = k_ref[0, 0].astype(F32), k_ref[1, 0].astype(F32)
    y = jnp.concatenate([xr * kr - xi * ki, xr * ki + xi * kr], axis=0).astype(BF16)
    d = jnp.dot(gh_ref[0], y, preferred_element_type=F32)
    d_ref[:, 0] = d.reshape(2, n2, c).astype(d_ref.dtype)


def _mid(a, g, gh, kf):
    _, N1, N2, C = a.shape
    blk = pl.BlockSpec((2, 1, N2, C), lambda i: (0, i, 0, 0))
    tab = pl.BlockSpec((1, 2 * N2, 2 * N2), lambda i: (i, 0, 0))
    return pl.pallas_call(
        _mid_kernel,
        grid=(N1,),
        in_specs=[blk, tab, tab, blk],
        out_specs=blk,
        out_shape=jax.ShapeDtypeStruct((2, N1, N2, C), BF16),
        compiler_params=_cparams(("parallel",)),
        name="dft_inner",
    )(a, g, gh, kf)


def _hyena_long(v, x1, x2, filt, bias, tabs, cb):
    B, L, C = v.shape
    assert B == 2, "the two batches are packed as real / imaginary parts of one complex signal"
    w_fwd, w_real, w_inv, g, gh = tabs
    N1 = g.shape[0]
    N2 = g.shape[1] // 2
    k0, k1, inv_norm = filt
    flat = lambda a: a.reshape(-1, N2 * C)
    z = flat(v)
    for o, (ker, gate) in enumerate(((k0, x1), (k1, x2))):
        if N1 > 1:
            ka = _colmm(w_real, flat(ker), cb).reshape(2, N1, N2, C)
            za = _colmm(w_fwd, z, cb).reshape(2, N1, N2, C)
        else:
            ka = jnp.stack([ker, jnp.zeros_like(ker)]).reshape(2, 1, N2, C)
            za = jnp.pad(z.reshape(2, L, C), ((0, 0), (0, L), (0, 0))).reshape(2, 1, N2, C)
        kf = _mid_fwd(ka, g, inv_norm[:, o * C:(o + 1) * C])
        d = _mid(za, g, gh, kf)
        if N1 > 1:
            z = _colmm_gate(w_inv, flat(d), z, flat(gate), jnp.tile(bias[o], cb // C)[None, :], cb)
        else:
            y = d.reshape(2, N2, C)[:, :L].astype(F32)
            zf = z.reshape(2, L, C).astype(F32)
            z = (gate.astype(F32) * (y + zf * bias[o])).astype(BF16).reshape(-1, N2 * C // 2)
    return z.reshape(B, L, C)


def _resid_kernel(x_ref, fT_ref, gt_ref, g_ref, o_ref, *, final):
    xn = x_ref[...] + gt_ref[0] * fT_ref[...].T
    if final:
        xn = xn * lax.rsqrt(jnp.mean(xn * xn, axis=-1, keepdims=True) + EPS) * g_ref[...]
    o_ref[...] = xn


def _resid(x2d, fT, col0, gt, g, rows_per_batch, tm, final):
    T, D = x2d.shape
    tpb = rows_per_batch // tm
    c0 = col0 // tm
    row = lambda i: (i, 0)
    return pl.pallas_call(
        functools.partial(_resid_kernel, final=final),
        grid=(T // tm,),
        in_specs=[pl.BlockSpec((tm, D), row), pl.BlockSpec((D, tm), lambda i: (0, c0 + i)),
                  pl.BlockSpec((1, 1, D), lambda i: (i // tpb, 0, 0)), pl.BlockSpec((1, D), lambda i: (0, 0))],
        out_specs=pl.BlockSpec((tm, D), row),
        out_shape=jax.ShapeDtypeStruct((T, D), F32),
        compiler_params=_cparams(("parallel",)),
        name="residual",
    )(x2d, fT, gt, g)


def _peer(h2, wq, subkeys, u_bf, vT_bf, tc, nb):
    T, D = h2.shape
    hT = h2.T
    wqT = wq.T
    wqT_hi = wqT.astype(BF16)
    wqT_lo = (wqT - wqT_hi.astype(F32)).astype(BF16)
    cnt, e1, r2, e2 = _peer_route(hT, wqT_hi, wqT_lo, subkeys, tc)
    cnt4 = cnt.reshape(PEER_HEADS, PEER_NKEYS // nb, nb, T)
    e14 = e1.reshape(PEER_HEADS, PEER_NKEYS // nb, nb, T)
    return _peer_dense(hT, r2, e2, cnt4, e14, u_bf, vT_bf, tc, nb)


def kernel(x, c, ctx, c_ctx, w_mod, b_mod, g_norm1, g_norm2, w_in, q_gain, k_gain, sc_conv_w, hy_conv_w,
           hy_conv_b, hy_w1, hy_b1, hy_f1, hy_w2, hy_b2, hy_f2, hy_w3, hy_decay, hy_bias, w_br_att, w_br_sc,
           w_br_hy, w_out, peer_wq, peer_subkeys, peer_u, peer_v, g_final):
    B, S, D = x.shape
    Lc = ctx.shape[1]
    hp = lax.Precision.HIGHEST
    cond = jnp.concatenate([jax.nn.silu(c), jnp.broadcast_to(jax.nn.silu(c_ctx), (B, D))], axis=0)
    cos, sin = _rope_tables(S, True)
    cos_c, sin_c = _rope_tables(Lc, False)
    tm, tq, tr = min(TM_MATMUL, S), min(TQ_ATT, S), min(TM_ROWS, S)
    cb = DFT_COLS * HY_W
    tabs = _dft_tables(2 * S // HY_N2, HY_N2)
    tabs_c = _dft_tables(1, 2 * Lc)
    x2 = x.reshape(B * S, D)
    ctx2 = ctx.reshape(B * Lc, D)

    for l in range(DEPTH):
        need_ctx = l < DEPTH - 1
        mod = (jnp.dot(cond, w_mod[l], precision=hp) + b_mod[l]).reshape(2, B, 1, 6, D)
        sh1, sc1, gt1, sh2, sc2, gt2 = (mod[0, :, :, i] for i in range(6))
        csh1, csc1, cgt1, csh2, csc2, cgt2 = (mod[1, :, :, i] for i in range(6))
        w_in_bf = jnp.concatenate([w_in[l][:, REF_GT0:], w_in[l][:, REF_SC0:REF_GT0], w_in[l][:, :REF_SC0]],
                                  axis=1).astype(BF16)
        wa, ws, wh, wo = (w.astype(BF16) for w in (w_br_att[l], w_br_sc[l], w_br_hy[l], w_out[l]))
        g1 = g_norm1[l][None, :]
        g2 = g_norm2[l][None, :]
        hy_params = (hy_w1[l], hy_b1[l], hy_f1[l], hy_w2[l], hy_b2[l], hy_f2[l], hy_w3[l], hy_decay[l])

        p = _in_proj(x2, g1, sc1, sh1, w_in_bf, S, min(TM_IN_PROJ, S), IN_W // 3)
        p3 = p.reshape(B, S, -1)
        pc = _in_proj(ctx2, g1, csc1, csh1, w_in_bf, Lc, Lc, IN_W // 3)
        pc3 = pc.reshape(B, Lc, -1)

        qs, k = _qk_prep(p, cos, sin, q_gain[l], k_gain[l], S, tr)
        qcs, kc = _qk_prep(pc, cos_c, sin_c, q_gain[l], k_gain[l], Lc, Lc)
        k_all = jnp.concatenate([kc.reshape(B, Lc, KV_W), k.reshape(B, S, KV_W)], axis=1)
        v_all = jnp.concatenate([pc3[..., V0:IN_W], p3[..., V0:IN_W]], axis=1)
        y_att = _attention(qs.reshape(B, S, ATT_W), k_all, v_all, tq, _key_block(S + Lc)).reshape(B * S, ATT_W)

        y_sc = _short_conv(p3, sc_conv_w[l], tr).reshape(B * S, SC_W)
        y_hy = _hyena_long(*_hy_pre(p3, hy_conv_w[l], hy_conv_b[l], tr), _hy_filter(S, *hy_params, tr),
                           hy_bias[l], tabs, cb).reshape(B * S, HY_W)
        x2, h2 = _merge(x2, y_att, y_sc, y_hy, p, gt1, g2, sc2, sh2, wa, ws, wh, wo, S, tm)

        if need_ctx:
            yc_att = _attention(qcs.reshape(B, Lc, ATT_W), kc.reshape(B, Lc, KV_W), pc3[..., V0:IN_W], Lc, Lc)
            yc_sc = _short_conv(pc3, sc_conv_w[l], Lc).reshape(B * Lc, SC_W)
            yc_hy = _hyena_long(*_hy_pre(pc3, hy_conv_w[l], hy_conv_b[l], Lc), _hy_filter(Lc, *hy_params, Lc),
                                hy_bias[l], tabs_c, cb).reshape(B * Lc, HY_W)
            ctx2, h2c = _merge(ctx2, yc_att.reshape(B * Lc, ATT_W), yc_sc, yc_hy, pc, cgt1, g2, csc2, csh2,
                               wa, ws, wh, wo, Lc, Lc)
            tok = jnp.concatenate([h2, h2c], axis=0)
        else:
            tok = h2

        u_bf = peer_u[l].astype(BF16)
        vT_bf = peer_v[l].astype(BF16).reshape(-1, PEER_BLOCK * PEER_NKEYS, D).transpose(0, 2, 1)
        fT = _peer(tok, peer_wq[l], peer_subkeys[l], u_bf, vT_bf, min(TC_PEER, tok.shape[0]), PEER_BLOCK)
        last = l == DEPTH - 1
        x2 = _resid(x2, fT, 0, gt2, g_final[None, :], S, tr, last)
        if need_ctx:
            ctx2 = _resid(ctx2, fT, B * S, cgt2, g_final[None, :], Lc, Lc, False)

    return x2.reshape(B, S, D)
```

```python
import functools
import math

import jax
import jax.numpy as jnp
from jax import lax
from jax.experimental import pallas as pl
from jax.experimental.pallas import tpu as pltpu

F32 = jnp.float32
BF16 = jnp.bfloat16

DEPTH = 2
GRID_W = 64
EPS = 1e-6
N_HEADS = 8
N_KV_HEADS = 2
GQA_GROUP = N_HEADS // N_KV_HEADS
HEAD_DIM = 64
AXIS_DIM = HEAD_DIM // 2
ATT_W = N_HEADS * HEAD_DIM
KV_W = N_KV_HEADS * HEAD_DIM
ATT_SCALE = HEAD_DIM ** -0.5
ROPE_THETA = 10000.0
SC_W = 512
HY_W = 512
HY_ORDER = 2
HY_BANDS = 16
PEER_HEADS = 8
PEER_NKEYS = 128
PEER_DK = 128
PEER_TOPK = 16
D_MODEL = 1024
QKV_W = ATT_W + 2 * KV_W
REF_SC0 = QKV_W
REF_GT0 = QKV_W + 3 * SC_W + (HY_ORDER + 1) * HY_W
GT0 = 0
SC0 = GT0 + 3 * D_MODEL
HY0 = SC0 + 3 * SC_W
Q0 = HY0 + (HY_ORDER + 1) * HY_W
K0 = Q0 + ATT_W
V0 = K0 + KV_W
IN_W = V0 + KV_W
HALO = 8
HY_N2 = 256
HY_FEAT = 128

VMEM_BYTES_V7X = 64 * 1024 * 1024
VMEM_LIMIT = VMEM_BYTES_V7X - 8 * 1024 * 1024
MXU_WIDTH_V7X = 256
LANES = 128
SUBLANES = 8
PACKED_ROWS = 16

TM_IN_PROJ = 2048
TM_MATMUL = 1024
TM_ROWS = 1024
TQ_ATT = 512
TK_ATT_MAX = 2048
TC_PEER = 512
DFT_COLS = 32


def _cparams(sem):
    return pltpu.CompilerParams(dimension_semantics=sem, vmem_limit_bytes=VMEM_LIMIT)


def _key_block(lk):
    for unit in (MXU_WIDTH_V7X, LANES):
        cands = [t for t in range(unit, min(lk, TK_ATT_MAX) + 1, unit) if lk % t == 0]
        if cands:
            return max(cands)
    raise ValueError(f"no lane-aligned key block divides {lk}")


def _in_proj_kernel(x_ref, g_ref, sc_ref, sh_ref, w_ref, o_ref, h_scr):
    @pl.when(pl.program_id(1) == 0)
    def _():
        x = x_ref[...]
        ms = jnp.mean(x * x, axis=-1, keepdims=True)
        y = x * lax.rsqrt(ms + EPS) * g_ref[...]
        h_scr[...] = (y * (1.0 + sc_ref[0]) + sh_ref[0]).astype(BF16)

    o_ref[...] = jnp.dot(h_scr[...], w_ref[...], preferred_element_type=F32).astype(o_ref.dtype)


def _in_proj(x2d, g, sc, sh, w, rows_per_batch, tm, tn):
    T, D = x2d.shape
    N = w.shape[1]
    tpb = rows_per_batch // tm
    return pl.pallas_call(
        _in_proj_kernel,
        grid=(T // tm, N // tn),
        in_specs=[
            pl.BlockSpec((tm, D), lambda i, j: (i, 0)),
            pl.BlockSpec((1, D), lambda i, j: (0, 0)),
            pl.BlockSpec((1, 1, D), lambda i, j: (i // tpb, 0, 0)),
            pl.BlockSpec((1, 1, D), lambda i, j: (i // tpb, 0, 0)),
            pl.BlockSpec((D, tn), lambda i, j: (0, j)),
        ],
        out_specs=pl.BlockSpec((tm, tn), lambda i, j: (i, j)),
        out_shape=jax.ShapeDtypeStruct((T, N), BF16),
        scratch_shapes=[pltpu.VMEM((tm, D), BF16)],
        compiler_params=_cparams(("parallel", "arbitrary")),
        name="in_proj",
    )(x2d, g, sc, sh, w)


def _attn_kernel(q_ref, kT_ref, v_ref, o_ref, q_scr, m_scr, acc_scr, sa_scr, sb_scr, *, nkb):
    tq = q_ref.shape[1]
    m_scr[...] = jnp.full(m_scr.shape, -jnp.inf, F32)
    acc_scr[...] = jnp.zeros(acc_scr.shape, F32)
    for g in range(GQA_GROUP):
        q_scr[g * tq:(g + 1) * tq, :] = q_ref[0, :, g * HEAD_DIM:(g + 1) * HEAD_DIM]

    def scores(j, dst_scr):
        dst_scr[...] = jnp.dot(q_scr[...], kT_ref[0, 0, j], preferred_element_type=F32)

    def update(j, src_scr):
        s = src_scr[...]
        m_prev = m_scr[...]
        m_new = jnp.maximum(m_prev, jnp.max(s, axis=-1, keepdims=True))
        p = jnp.exp(s - m_new).astype(BF16)
        alpha = jnp.exp(m_prev - m_new)
        acc_scr[...] = acc_scr[...] * alpha + jnp.dot(p, v_ref[0, 0, j], preferred_element_type=F32)
        m_scr[...] = m_new

    scores(0, sa_scr)

    def pair(i, carry):
        j = 2 * i
        scores(j + 1, sb_scr)
        update(j, sa_scr)
        scores(j + 2, sa_scr)
        update(j + 1, sb_scr)
        return carry

    lax.fori_loop(0, (nkb - 1) // 2, pair, 0)
    if nkb % 2 == 1:
        update(nkb - 1, sa_scr)
    else:
        scores(nkb - 1, sb_scr)
        update(nkb - 2, sa_scr)
        update(nkb - 1, sb_scr)
    acc = acc_scr[...]
    o = (acc[:, :HEAD_DIM] / acc[:, HEAD_DIM:HEAD_DIM + 1]).astype(o_ref.dtype)
    o_ref[0] = jnp.concatenate([o[g * tq:(g + 1) * tq] for g in range(GQA_GROUP)], axis=1)


def _attention(q, k, v, tq, tk):
    B, Lq, _ = q.shape
    Lk = k.shape[1]
    nqb, nkb = Lq // tq, Lk // tk
    R = GQA_GROUP * tq
    GW = GQA_GROUP * HEAD_DIM
    kT = k.reshape(B, nkb, tk, N_KV_HEADS, HEAD_DIM).transpose(0, 3, 1, 4, 2)
    vb = v.reshape(B, nkb, tk, N_KV_HEADS, HEAD_DIM).transpose(0, 3, 1, 2, 4)
    ones = jnp.ones(vb.shape[:-1] + (1,), BF16)
    zeros = jnp.zeros(vb.shape[:-1] + (HEAD_DIM - 1,), BF16)
    vb = jnp.concatenate([vb, ones, zeros], axis=-1)
    return pl.pallas_call(
        functools.partial(_attn_kernel, nkb=nkb),
        grid=(B, N_KV_HEADS, nqb),
        in_specs=[
            pl.BlockSpec((1, tq, GW), lambda b, h, i: (b, i, h)),
            pl.BlockSpec((1, 1, nkb, HEAD_DIM, tk), lambda b, h, i: (b, h, 0, 0, 0)),
            pl.BlockSpec((1, 1, nkb, tk, 2 * HEAD_DIM), lambda b, h, i: (b, h, 0, 0, 0)),
        ],
        out_specs=pl.BlockSpec((1, tq, GW), lambda b, h, i: (b, i, h)),
        out_shape=jax.ShapeDtypeStruct((B, Lq, ATT_W), BF16),
        scratch_shapes=[pltpu.VMEM((R, HEAD_DIM), BF16), pltpu.VMEM((R, 1), F32), pltpu.VMEM((R, 2 * HEAD_DIM), F32),
                        pltpu.VMEM((R, tk), F32), pltpu.VMEM((R, tk), F32)],
        compiler_params=_cparams(("parallel", "parallel", "arbitrary")),
        name="attention",
    )(q, kT, vb)


def _merge_kernel(x_ref, ya_ref, ys_ref, yh_ref, ga_ref, gs_ref, gh_ref, gt_ref, g2_ref, sc_ref, sh_ref,
                  wa_ref, ws_ref, wh_ref, wo_ref, xo_ref, h2_ref):
    def br(y_ref, g_ref, w_ref):
        gate = jax.nn.sigmoid(g_ref[...].astype(F32))
        return gate * jnp.dot(y_ref[...], w_ref[...], preferred_element_type=F32)

    m = br(ya_ref, ga_ref, wa_ref) + br(ys_ref, gs_ref, ws_ref) + br(yh_ref, gh_ref, wh_ref)
    o = jnp.dot(m.astype(BF16), wo_ref[...], preferred_element_type=F32)
    xn = x_ref[...] + gt_ref[0] * o
    xo_ref[...] = xn
    ms = jnp.mean(xn * xn, axis=-1, keepdims=True)
    y = xn * lax.rsqrt(ms + EPS) * g2_ref[...]
    h2_ref[...] = (y * (1.0 + sc_ref[0]) + sh_ref[0]).astype(BF16)


def _merge(x2d, ya, ys, yh, p, gt1, g2, sc2, sh2, wa, ws, wh, wo, rows_per_batch, tm):
    T, D = x2d.shape
    tpb = rows_per_batch // tm
    gblk = GT0 // D
    row = lambda i: (i, 0)
    mod = lambda i: (i // tpb, 0, 0)
    full = lambda i: (0, 0)
    return pl.pallas_call(
        _merge_kernel,
        grid=(T // tm,),
        in_specs=[
            pl.BlockSpec((tm, D), row),
            pl.BlockSpec((tm, ATT_W), row),
            pl.BlockSpec((tm, SC_W), row),
            pl.BlockSpec((tm, HY_W), row),
            pl.BlockSpec((tm, D), lambda i: (i, gblk)),
            pl.BlockSpec((tm, D), lambda i: (i, gblk + 1)),
            pl.BlockSpec((tm, D), lambda i: (i, gblk + 2)),
            pl.BlockSpec((1, 1, D), mod),
            pl.BlockSpec((1, D), full),
            pl.BlockSpec((1, 1, D), mod),
            pl.BlockSpec((1, 1, D), mod),
            pl.BlockSpec((ATT_W, D), full),
            pl.BlockSpec((SC_W, D), full),
            pl.BlockSpec((HY_W, D), full),
            pl.BlockSpec((D, D), full),
        ],
        out_specs=[pl.BlockSpec((tm, D), row), pl.BlockSpec((tm, D), row)],
        out_shape=[jax.ShapeDtypeStruct((T, D), F32), jax.ShapeDtypeStruct((T, D), BF16)],
        compiler_params=_cparams(("parallel",)),
        name="merge",
    )(x2d, ya, ys, yh, p, p, p, gt1, g2, sc2, sh2, wa, ws, wh, wo)


PEER_BLOCK = 8
PEER_MM = 8
PEER_ACC = 2


def _peer_kernel(hT_ref, r2_ref, e2_ref, n_ref, e1_ref, u_ref, vT_ref, o_ref, gwa_scr, gwb_scr, act_scr, *, nb):
    e = pl.program_id(1)
    tc = hT_ref.shape[1]

    @pl.when(e == 0)
    def _():
        o_ref[...] = jnp.zeros(o_ref.shape, F32)
        gwb_scr[...] = jnp.zeros(gwb_scr.shape, BF16)

    def step(cur_scr, prev_scr):
        blk = jnp.minimum(e, pl.num_programs(1) - 2)
        shape3 = (PEER_NKEYS // PACKED_ROWS, PACKED_ROWS, LANES)
        for g0 in range(0, nb, PEER_MM):
            grows = slice(g0 * PEER_NKEYS, (g0 + PEER_MM) * PEER_NKEYS)
            o_ref[...] += jnp.dot(vT_ref[0, :, grows], prev_scr[grows, :], preferred_element_type=F32)
            for lt in range(tc // LANES):
                lanes = slice(lt * LANES, (lt + 1) * LANES)
                for i0 in range(g0, g0 + PEER_MM, PEER_ACC):
                    ws = [jnp.zeros(shape3, BF16) for _ in range(PEER_ACC)]
                    for h in range(PEER_HEADS):
                        r2t = pltpu.bitcast(r2_ref[h, :, lanes], BF16).reshape(shape3)
                        e2t = pltpu.bitcast(e2_ref[h, :, lanes], BF16).reshape(shape3)
                        for k in range(PEER_ACC):
                            ii = i0 + k
                            cnt = jnp.broadcast_to(n_ref[h, blk, ii:ii + 1, lanes], (PACKED_ROWS, LANES)).astype(BF16)
                            e1row = jnp.broadcast_to(e1_ref[h, blk, ii:ii + 1, lanes],
                                                     (PACKED_ROWS, LANES)).astype(BF16)
                            ws[k] = ws[k] + jnp.where(r2t < cnt[None], e2t, jnp.zeros_like(e2t)) * e1row[None]
                    for k in range(PEER_ACC):
                        rows = slice((i0 + k) * PEER_NKEYS, (i0 + k + 1) * PEER_NKEYS)
                        cur_scr[rows, lanes] = ws[k].reshape(PEER_NKEYS, LANES)
            a = jnp.dot(u_ref[grows, :], hT_ref[...], preferred_element_type=F32)
            act = (0.5 * a * (1.0 + lax.erf(a * (2.0 ** -0.5)))).astype(BF16)
            cur_scr[grows, :] = cur_scr[grows, :] * act

    @pl.when(e % 2 == 0)
    def _():
        step(gwa_scr, gwb_scr)

    @pl.when(e % 2 == 1)
    def _():
        step(gwb_scr, gwa_scr)


def _peer_dense(hT, r2, e2, cnt, e1, u, vT, tc, nb):
    D, T = hT.shape
    N = u.shape[0]
    eb = nb * PEER_NKEYS
    ne = N // eb
    assert ne % 2 == 0, "the drain step must find the last block in the buffer the parity rule reads"
    cur = lambda e: jnp.minimum(e, ne - 1)
    return pl.pallas_call(
        functools.partial(_peer_kernel, nb=nb),
        grid=(T // tc, ne + 1),
        in_specs=[
            pl.BlockSpec((D, tc), lambda t, e: (0, t)),
            pl.BlockSpec((PEER_HEADS, PEER_NKEYS // 2, tc), lambda t, e: (0, 0, t)),
            pl.BlockSpec((PEER_HEADS, PEER_NKEYS // 2, tc), lambda t, e: (0, 0, t)),
            pl.BlockSpec((PEER_HEADS, ne, nb, tc), lambda t, e: (0, 0, 0, t)),
            pl.BlockSpec((PEER_HEADS, ne, nb, tc), lambda t, e: (0, 0, 0, t)),
            pl.BlockSpec((eb, D), lambda t, e: (cur(e), 0)),
            pl.BlockSpec((1, D, eb), lambda t, e: (jnp.maximum(e - 1, 0), 0, 0)),
        ],
        out_specs=pl.BlockSpec((D, tc), lambda t, e: (0, t)),
        out_shape=jax.ShapeDtypeStruct((D, T), F32),
        scratch_shapes=[pltpu.VMEM((eb, tc), BF16), pltpu.VMEM((eb, tc), BF16), pltpu.VMEM((eb, tc), BF16)],
        compiler_params=_cparams(("parallel", "arbitrary")),
        name="peer_dense",
    )(hT, r2, e2, cnt, e1, u, vT)


ROUTE_ROWS = 24
NEG_INF = float("-inf")
ROUTE_UNROLL = 4


def _sort_network(n):
    pairs = []
    p = 1
    while p < n:
        k = p
        while k >= 1:
            for j in range(k % p, n - k, 2 * k):
                for i in range(min(k, n - j - k)):
                    if (i + j) // (2 * p) == (i + j + k) // (2 * p):
                        pairs.append((i + j, i + j + k))
            k //= 2
        p *= 2
    return pairs


def _top_rows_sorted(s, n):
    tc = s.shape[1]
    levels = s.shape[0] // SUBLANES
    col = [s[r * SUBLANES:(r + 1) * SUBLANES] for r in range(levels)]
    for a, b in _sort_network(1 << (levels - 1).bit_length()):
        if b < levels:
            col[a], col[b] = jnp.maximum(col[a], col[b]), jnp.minimum(col[a], col[b])
    rid = lax.broadcasted_iota(jnp.int32, (ROUTE_ROWS, tc), 0)
    packed = jnp.full((ROUTE_ROWS, tc), NEG_INF, F32)
    for k in range(n):
        m = jnp.max(col[0], axis=0, keepdims=True)
        packed = jnp.where(rid == k, m, packed)
        hit = col[0] == m
        live = min(levels, n - k)
        for r in range(live):
            below = col[r + 1] if r + 1 < levels else NEG_INF
            col[r] = jnp.where(hit, below, col[r])
    return packed


def _row_penalty(tc, lo, hi):
    rid = lax.broadcasted_iota(jnp.int32, (8, tc), 0)
    return jnp.where((rid >= lo) & (rid < hi), 0.0, NEG_INF).astype(F32)


def _route_kernel(hT_ref, wqh_ref, wql_ref, sk_ref, n_ref, e1_ref, r2_ref, e2_ref, q_scr):
    hT = hT_ref[...]
    q_scr[...] = (jnp.dot(wqh_ref[...], hT, preferred_element_type=F32)
                  + jnp.dot(wql_ref[...], hT, preferred_element_type=F32))
    tc = hT.shape[1]
    n = PEER_TOPK + 1
    half = PEER_DK // 2

    def head(h, carry):
        r0 = pl.multiple_of(h * PEER_DK, PEER_DK)
        hp = lax.Precision.HIGHEST
        s1 = jnp.dot(sk_ref[h, 0], q_scr[pl.ds(r0, half), :], preferred_element_type=F32, precision=hp)
        s2 = jnp.dot(sk_ref[h, 1], q_scr[pl.ds(r0 + half, half), :], preferred_element_type=F32, precision=hp)
        v1 = _top_rows_sorted(s1, n)
        v2 = _top_rows_sorted(s2, n)
        tiles = [v1[0:1] + v2[0:8], v1[0:1] + v2[8:16], v1[0:1] + v2[16:24], v1[1:2] + v2[0:8],
                 v1[2:3] + v2[0:8] + _row_penalty(tc, 0, n // 3), v1[3:4] + v2[0:8] + _row_penalty(tc, 0, n // 4),
                 v2[0:1] + v1[0:8] + _row_penalty(tc, 4, 8), v2[0:1] + v1[8:16], v2[0:1] + v1[16:24],
                 v2[1:2] + v1[0:8] + _row_penalty(tc, 4, n // 2), v2[2:3] + v1[0:8] + _row_penalty(tc, 4, n // 3)]
        top = _top_rows_sorted(jnp.concatenate(tiles, axis=0), n)
        theta = 0.5 * (top[PEER_TOPK - 1:PEER_TOPK] + top[PEER_TOPK:PEER_TOPK + 1])
        z = jnp.sum(jnp.exp(top[0:16] - top[0:1]), axis=0, keepdims=True)
        c = theta - s1
        cnt = jnp.zeros(c.shape, F32)
        rank = jnp.zeros(c.shape, F32)
        for b in range(n):
            cnt = jnp.where(v2[b:b + 1] >= c, float(b + 1), cnt)
            rank = jnp.where(v2[b:b + 1] > s2, float(b + 1), rank)
        n_ref[h] = cnt
        e1_ref[h] = jnp.exp(s1 - v1[0:1]) / z
        r2_ref[h] = pltpu.bitcast(rank.astype(BF16), jnp.uint32)
        e2_ref[h] = pltpu.bitcast(jnp.exp(s2 - v2[0:1]).astype(BF16), jnp.uint32)
        return carry

    def heads(i, c):
        for k in range(ROUTE_UNROLL):
            c = head(ROUTE_UNROLL * i + k, c)
        return c

    lax.fori_loop(0, PEER_HEADS // ROUTE_UNROLL, heads, 0)


def _peer_route(hT, wqT_hi, wqT_lo, subkeys, tc):
    D, T = hT.shape
    W = wqT_hi.shape[0]
    out = jax.ShapeDtypeStruct((PEER_HEADS, PEER_NKEYS, T), F32)
    out16 = jax.ShapeDtypeStruct((PEER_HEADS, PEER_NKEYS // 2, T), jnp.uint32)
    ospec = pl.BlockSpec((PEER_HEADS, PEER_NKEYS, tc), lambda t: (0, 0, t))
    ospec16 = pl.BlockSpec((PEER_HEADS, PEER_NKEYS // 2, tc), lambda t: (0, 0, t))
    return pl.pallas_call(
        _route_kernel,
        grid=(T // tc,),
        in_specs=[
            pl.BlockSpec((D, tc), lambda t: (0, t)),
            pl.BlockSpec((W, D), lambda t: (0, 0)),
            pl.BlockSpec((W, D), lambda t: (0, 0)),
            pl.BlockSpec(subkeys.shape, lambda t: (0, 0, 0, 0)),
        ],
        out_specs=[ospec, ospec, ospec16, ospec16],
        out_shape=[out, out, out16, out16],
        scratch_shapes=[pltpu.VMEM((W, tc), F32)],
        compiler_params=_cparams(("parallel",)),
        name="peer_route",
    )(hT, wqT_hi, wqT_lo, subkeys)


def _rope_tables(L, rope):
    if not rope:
        return jnp.ones((L, 2 * HEAD_DIM), F32), jnp.zeros((L, 2 * HEAD_DIM), F32)
    pos = jnp.arange(L, dtype=jnp.int32)
    row = (pos // GRID_W).astype(F32)
    col = (pos % GRID_W).astype(F32)
    inv = jnp.power(ROPE_THETA, -jnp.arange(0, AXIS_DIM, 2, dtype=F32) / AXIS_DIM)
    ar = row[:, None] * inv[None, :]
    ac = col[:, None] * inv[None, :]
    cos = jnp.concatenate([jnp.cos(ar), jnp.cos(ar), jnp.cos(ac), jnp.cos(ac)], axis=-1)
    sin = jnp.concatenate([-jnp.sin(ar), jnp.sin(ar), -jnp.sin(ac), jnp.sin(ac)], axis=-1)
    return jnp.tile(cos, (1, 2)), jnp.tile(sin, (1, 2))


def _qk_prep_kernel(p_ref, cos_ref, sin_ref, qg_ref, kg_ref, bdq_ref, bdk_ref, q_ref, k_ref):
    x = p_ref[...].astype(F32)
    cos = cos_ref[...]
    sin = sin_ref[...]
    quarter = AXIS_DIM // 2

    def prep(xh, gain, bd):
        w = xh.shape[1]
        sq = xh * xh
        hi = sq.astype(BF16)
        lo = (sq - hi.astype(F32)).astype(BF16)
        ms = jnp.dot(hi, bd, preferred_element_type=F32) + jnp.dot(lo, bd, preferred_element_type=F32)
        y = xh * lax.rsqrt(ms + EPS) * gain
        lane = lax.broadcasted_iota(jnp.int32, y.shape, 1)
        first = (lane & (AXIS_DIM - 1)) < quarter
        partner = jnp.where(first, pltpu.roll(y, w - quarter, 1), pltpu.roll(y, quarter, 1))
        reps = w // cos.shape[1]
        c = jnp.concatenate([cos] * reps, axis=1) if reps > 1 else cos
        s = jnp.concatenate([sin] * reps, axis=1) if reps > 1 else sin
        return y * c + partner * s

    q_ref[...] = (prep(x[:, :ATT_W], qg_ref[...], bdq_ref[...]) * ATT_SCALE).astype(BF16)
    k_ref[...] = prep(x[:, ATT_W:ATT_W + KV_W], kg_ref[...], bdk_ref[...]).astype(BF16)


def _qk_prep(p, cos, sin, q_gain, k_gain, rows_per_batch, tm):
    T = p.shape[0]
    tpb = rows_per_batch // tm
    qg = jnp.tile(q_gain, N_HEADS)[None, :]
    kg = jnp.tile(k_gain, N_KV_HEADS)[None, :]

    def block_avg(w):
        hid = jnp.arange(w) // HEAD_DIM
        return jnp.where(hid[:, None] == hid[None, :], 1.0 / HEAD_DIM, 0.0).astype(BF16)

    full = lambda i: (0, 0)
    return pl.pallas_call(
        _qk_prep_kernel,
        grid=(T // tm,),
        in_specs=[
            pl.BlockSpec((tm, QKV_W), lambda i: (i, Q0 // QKV_W)),
            pl.BlockSpec((tm, 2 * HEAD_DIM), lambda i: (i % tpb, 0)),
            pl.BlockSpec((tm, 2 * HEAD_DIM), lambda i: (i % tpb, 0)),
            pl.BlockSpec((1, ATT_W), full),
            pl.BlockSpec((1, KV_W), full),
            pl.BlockSpec((ATT_W, ATT_W), full),
            pl.BlockSpec((KV_W, KV_W), full),
        ],
        out_specs=[pl.BlockSpec((tm, ATT_W), lambda i: (i, 0)), pl.BlockSpec((tm, KV_W), lambda i: (i, 0))],
        out_shape=[jax.ShapeDtypeStruct((T, ATT_W), BF16), jax.ShapeDtypeStruct((T, KV_W), BF16)],
        compiler_params=_cparams(("parallel",)),
        name="qk_prep",
    )(p, cos, sin, qg, kg, block_avg(ATT_W), block_avg(KV_W))


def _shift_rows(cur, prev8, next8, first, last):
    tm = cur.shape[0]
    rid = lax.broadcasted_iota(jnp.int32, cur.shape, 0)
    pr = jnp.where(first, 0.0, prev8[HALO - 1:HALO, :])
    nx = jnp.where(last, 0.0, next8[0:1, :])
    up = jnp.where(rid == 0, pr, pltpu.roll(cur, 1, 0))
    dn = jnp.where(rid == tm - 1, nx, pltpu.roll(cur, tm - 1, 0))
    return up, dn


def _halo_specs(tm, width, col_block, n_rows):
    r = tm // HALO
    last = n_rows // HALO - 1
    prev = pl.BlockSpec((1, HALO, width), lambda b, i: (b, jnp.maximum(i * r - 1, 0), col_block))
    nxt = pl.BlockSpec((1, HALO, width), lambda b, i: (b, jnp.minimum((i + 1) * r, last), col_block))
    return prev, nxt


def _short_conv_kernel(bg_ref, cg_ref, xs_ref, cgp_ref, xsp_ref, cgn_ref, xsn_ref, w_ref, o_ref):
    i = pl.program_id(1)
    f = lambda r: r[0].astype(F32)
    cur = f(cg_ref) * f(xs_ref)
    up, dn = _shift_rows(cur, f(cgp_ref) * f(xsp_ref), f(cgn_ref) * f(xsn_ref), i == 0, i == pl.num_programs(1) - 1)
    o_ref[0] = (f(bg_ref) * (up * w_ref[0:1] + cur * w_ref[1:2] + dn * w_ref[2:3])).astype(o_ref.dtype)


def _short_conv(p3, w, tm):
    B, L, _ = p3.shape
    c0 = SC0 // SC_W
    blk = lambda j: pl.BlockSpec((1, tm, SC_W), lambda b, i: (b, i, c0 + j))
    cgp, cgn = _halo_specs(tm, SC_W, c0 + 1, L)
    xsp, xsn = _halo_specs(tm, SC_W, c0 + 2, L)
    return pl.pallas_call(
        _short_conv_kernel,
        grid=(B, L // tm),
        in_specs=[blk(0), blk(1), blk(2), cgp, xsp, cgn, xsn, pl.BlockSpec((3, SC_W), lambda b, i: (0, 0))],
        out_specs=pl.BlockSpec((1, tm, SC_W), lambda b, i: (b, i, 0)),
        out_shape=jax.ShapeDtypeStruct((B, L, SC_W), BF16),
        compiler_params=_cparams(("parallel", "parallel")),
        name="short_conv",
    )(p3, p3, p3, p3, p3, p3, p3, w)


def _hy_pre_kernel(p_ref, pp_ref, pn_ref, w_ref, b_ref, v_ref, x1_ref, x2_ref):
    i = pl.program_id(1)
    cur = p_ref[0].astype(F32)
    up, dn = _shift_rows(cur, pp_ref[0].astype(F32), pn_ref[0].astype(F32), i == 0, i == pl.num_programs(1) - 1)
    u = up * w_ref[0:1] + cur * w_ref[1:2] + dn * w_ref[2:3] + b_ref[...]
    v_ref[0] = u[:, :HY_W].astype(BF16)
    x1_ref[0] = u[:, HY_W:2 * HY_W].astype(BF16)
    x2_ref[0] = u[:, 2 * HY_W:].astype(BF16)


def _hy_pre(p3, w, b, tm):
    B, L, _ = p3.shape
    W = (HY_ORDER + 1) * HY_W
    c0 = HY0 // W
    prev, nxt = _halo_specs(tm, W, c0, L)
    out = jax.ShapeDtypeStruct((B, L, HY_W), BF16)
    ospec = pl.BlockSpec((1, tm, HY_W), lambda b_, i: (b_, i, 0))
    return pl.pallas_call(
        _hy_pre_kernel,
        grid=(B, L // tm),
        in_specs=[pl.BlockSpec((1, tm, W), lambda b_, i: (b_, i, c0)), prev, nxt,
                  pl.BlockSpec((3, W), lambda b_, i: (0, 0)), pl.BlockSpec((1, W), lambda b_, i: (0, 0))],
        out_specs=[ospec, ospec, ospec],
        out_shape=[out, out, out],
        compiler_params=_cparams(("parallel", "parallel")),
        name="hyena_pre",
    )(p3, p3, p3, w, b[None, :])


def _hy_features(L):
    n = jnp.arange(2 * L, dtype=jnp.int32)
    j = jnp.where(n < L, n, jnp.where(n == L, 0, 2 * L - n)).astype(F32)[:, None]
    t = j / (L - 1)
    w = 2.0 * math.pi * j / L
    bands = jnp.linspace(1e-4, HY_BANDS - 1, HY_BANDS, dtype=F32)[None, :]
    z = jnp.concatenate([t, jnp.cos(bands * w), -jnp.sin(bands * w)], axis=-1)
    return jnp.pad(z, ((0, 0), (0, HY_FEAT - z.shape[1])))


def _hy_filter_kernel(z_ref, w1_ref, b1_ref, f1_ref, w2_ref, b2_ref, f2_ref, w3_ref, dec_ref,
                      k0_ref, k1_ref, sum_ref):
    i = pl.program_id(0)
    half = pl.num_programs(0) // 2

    @pl.when(i == 0)
    def _():
        sum_ref[...] = jnp.zeros(sum_ref.shape, F32)

    hp = lax.Precision.HIGHEST
    z = z_ref[...]
    h = jnp.sin(f1_ref[...] * (jnp.dot(z, w1_ref[...], preferred_element_type=F32, precision=hp) + b1_ref[...]))
    h = jnp.sin(f2_ref[...] * (jnp.dot(h, w2_ref[...], preferred_element_type=F32, precision=hp) + b2_ref[...]))
    k = jnp.dot(h, w3_ref[0], preferred_element_type=F32, precision=hp)
    k = k * jnp.exp(-z[:, 0:1] * dec_ref[...])
    sum_ref[...] += jnp.sum(jnp.abs(k), axis=0, keepdims=True)
    rid = lax.broadcasted_iota(jnp.int32, k.shape, 0)
    k = jnp.where(i == half, jnp.where(rid == 0, 0.0, k), k)
    k0_ref[...] = k[:, :HY_W].astype(BF16)
    k1_ref[...] = k[:, HY_W:].astype(BF16)


def _hy_filter(L, w1, b1, f1, w2, b2, f2, w3, decay, nb):
    N = 2 * L
    fh = w1.shape[1]
    z = _hy_features(L)
    w1p = jnp.pad(w1, ((0, HY_FEAT - w1.shape[0]), (0, 0)))
    w3d = w3.reshape(fh, HY_ORDER, 2, HY_W).transpose(2, 0, 1, 3).reshape(2, fh, HY_ORDER * HY_W)
    dec = jnp.abs(decay).reshape(1, HY_ORDER * HY_W)
    row = lambda a: a[None, :]
    full = lambda i: (0, 0)
    half = N // nb // 2
    k0, k1, tot = pl.pallas_call(
        _hy_filter_kernel,
        grid=(N // nb,),
        in_specs=[
            pl.BlockSpec((nb, HY_FEAT), lambda i: (i, 0)),
            pl.BlockSpec((HY_FEAT, fh), full), pl.BlockSpec((1, fh), full), pl.BlockSpec((1, fh), full),
            pl.BlockSpec((fh, fh), full), pl.BlockSpec((1, fh), full), pl.BlockSpec((1, fh), full),
            pl.BlockSpec((1, fh, HY_ORDER * HY_W), lambda i: (i // half, 0, 0)),
            pl.BlockSpec((1, HY_ORDER * HY_W), full),
        ],
        out_specs=[pl.BlockSpec((nb, HY_W), lambda i: (i, 0)), pl.BlockSpec((nb, HY_W), lambda i: (i, 0)),
                   pl.BlockSpec((1, HY_ORDER * HY_W), full)],
        out_shape=[jax.ShapeDtypeStruct((N, HY_W), BF16), jax.ShapeDtypeStruct((N, HY_W), BF16),
                   jax.ShapeDtypeStruct((1, HY_ORDER * HY_W), F32)],
        compiler_params=_cparams(("arbitrary",)),
        name="hyena_filter",
    )(z, w1p, row(b1), row(f1), w2, row(b2), row(f2), w3d, dec)
    return k0, k1, 1.0 / tot


def _dft_tables(N1, N2):
    N = N1 * N2
    ar = lambda n: jnp.arange(n, dtype=jnp.int32)

    def cs(m, period):
        ang = (-2.0 * math.pi / period) * (m % period).astype(F32)
        return jnp.cos(ang), jnp.sin(ang)

    fr, fi = cs(ar(N1)[:, None] * ar(N1)[None, :], N1)
    hr, hi = fr[:, :max(N1 // 2, 1)], fi[:, :max(N1 // 2, 1)]
    w_fwd = jnp.block([[hr, -hi], [hi, hr]])
    w_real = jnp.concatenate([fr, fi], axis=0)
    w_inv = jnp.block([[hr.T, hi.T], [-hi.T, hr.T]]) / N1
    f2r, f2i = cs(ar(N2)[:, None] * ar(N2)[None, :], N2)
    tr, ti = cs(ar(N1)[:, None] * ar(N2)[None, :], N)
    gr = f2r[None] * tr[:, None, :] - f2i[None] * ti[:, None, :]
    gi = f2r[None] * ti[:, None, :] + f2i[None] * tr[:, None, :]
    g = jnp.concatenate([jnp.concatenate([gr, -gi], axis=2), jnp.concatenate([gi, gr], axis=2)], axis=1)
    grt, git = gr.transpose(0, 2, 1), gi.transpose(0, 2, 1)
    gh = jnp.concatenate([jnp.concatenate([grt, git], axis=2), jnp.concatenate([-git, grt], axis=2)], axis=1) / N2
    return tuple(a.astype(BF16) for a in (w_fwd, w_real, w_inv, g, gh))


def _colmm_kernel(w_ref, x_ref, o_ref):
    o_ref[...] = jnp.dot(w_ref[...], x_ref[...], preferred_element_type=F32).astype(o_ref.dtype)


def _colmm(w, x, cb):
    M, K = w.shape
    C = x.shape[1]
    return pl.pallas_call(
        _colmm_kernel,
        grid=(C // cb,),
        in_specs=[pl.BlockSpec((M, K), lambda j: (0, 0)), pl.BlockSpec((K, cb), lambda j: (0, j))],
        out_specs=pl.BlockSpec((M, cb), lambda j: (0, j)),
        out_shape=jax.ShapeDtypeStruct((M, C), BF16),
        compiler_params=_cparams(("parallel",)),
        name="dft_outer",
    )(w, x)


def _colmm_gate_kernel(w_ref, d_ref, u_ref, g_ref, b_ref, o_ref):
    y = jnp.dot(w_ref[...], d_ref[...], preferred_element_type=F32)
    o_ref[...] = (g_ref[...].astype(F32) * (y + u_ref[...].astype(F32) * b_ref[...])).astype(o_ref.dtype)


def _colmm_gate(w, d, u, gate, bias_row, cb):
    M, K = w.shape
    C = d.shape[1]
    col = lambda j: (0, j)
    return pl.pallas_call(
        _colmm_gate_kernel,
        grid=(C // cb,),
        in_specs=[pl.BlockSpec((M, K), lambda j: (0, 0)), pl.BlockSpec((K, cb), col), pl.BlockSpec((M, cb), col),
                  pl.BlockSpec((M, cb), col), pl.BlockSpec((1, cb), lambda j: (0, 0))],
        out_specs=pl.BlockSpec((M, cb), col),
        out_shape=jax.ShapeDtypeStruct((M, C), BF16),
        compiler_params=_cparams(("parallel",)),
        name="dft_outer_gate",
    )(w, d, u, gate, bias_row)


def _mid_fwd_kernel(a0_ref, a1_ref, g_ref, s_ref, k0_ref, k1_ref):
    n2 = a0_ref.shape[2]
    c = a0_ref.shape[3]
    for o, (a_ref, k_ref) in enumerate(((a0_ref, k0_ref), (a1_ref, k1_ref))):
        a = a_ref[:, 0].reshape(2 * n2, c)
        x = jnp.dot(g_ref[0], a, preferred_element_type=F32) * s_ref[:, o * c:(o + 1) * c]
        k_ref[:, 0] = x.reshape(2, n2, c).astype(k_ref.dtype)


def _mid_fwd(a0, a1, g, scale):
    _, N1, N2, C = a0.shape
    blk = pl.BlockSpec((2, 1, N2, C), lambda i: (0, i, 0, 0))
    out = jax.ShapeDtypeStruct((2, N1, N2, C), BF16)
    return pl.pallas_call(
        _mid_fwd_kernel,
        grid=(N1,),
        in_specs=[blk, blk, pl.BlockSpec((1, 2 * N2, 2 * N2), lambda i: (i, 0, 0)),
                  pl.BlockSpec((1, 2 * C), lambda i: (0, 0))],
        out_specs=[blk, blk],
        out_shape=[out, out],
        compiler_params=_cparams(("parallel",)),
        name="dft_inner_filter",
    )(a0, a1, g, scale)


def _mid_kernel(a_ref, g_ref, gh_ref, k_ref, d_ref):
    n2 = a_ref.shape[2]
    c = a_ref.shape[3]
    a = a_ref[:, 0].reshape(2 * n2, c)
    x = jnp.dot(g_ref[0], a, preferred_element_type=F32)
    xr, xi = x[:n2], x[n2:]
    kr, ki = k_ref[0, 0].astype(F32), k_ref[1, 0].astype(F32)
    y = jnp.concatenate([xr * kr - xi * ki, xr * ki + xi * kr], axis=0).astype(BF16)
    d = jnp.dot(gh_ref[0], y, preferred_element_type=F32)
    d_ref[:, 0] = d.reshape(2, n2, c).astype(d_ref.dtype)


def _mid(a, g, gh, kf):
    _, N1, N2, C = a.shape
    blk = pl.BlockSpec((2, 1, N2, C), lambda i: (0, i, 0, 0))
    tab = pl.BlockSpec((1, 2 * N2, 2 * N2), lambda i: (i, 0, 0))
    return pl.pallas_call(
        _mid_kernel,
        grid=(N1,),
        in_specs=[blk, tab, tab, blk],
        out_specs=blk,
        out_shape=jax.ShapeDtypeStruct((2, N1, N2, C), BF16),
        compiler_params=_cparams(("parallel",)),
        name="dft_inner",
    )(a, g, gh, kf)


def _hyena_long(v, x1, x2, filt, bias, tabs, cb):
    B, L, C = v.shape
    assert B == 2, "the two batches are packed as real / imaginary parts of one complex signal"
    w_fwd, w_real, w_inv, g, gh = tabs
    N1 = g.shape[0]
    N2 = g.shape[1] // 2
    k0, k1, inv_norm = filt
    flat = lambda a: a.reshape(-1, N2 * C)
    z = flat(v)
    if N1 > 1:
        kas = [_colmm(w_real, flat(ker), cb).reshape(2, N1, N2, C) for ker in (k0, k1)]
    else:
        kas = [jnp.stack([ker, jnp.zeros_like(ker)]).reshape(2, 1, N2, C) for ker in (k0, k1)]
    kfs = _mid_fwd(kas[0], kas[1], g, inv_norm)
    for o, gate in enumerate((x1, x2)):
        if N1 > 1:
            za = _colmm(w_fwd, z, cb).reshape(2, N1, N2, C)
        else:
            za = jnp.pad(z.reshape(2, L, C), ((0, 0), (0, L), (0, 0))).reshape(2, 1, N2, C)
        d = _mid(za, g, gh, kfs[o])
        if N1 > 1:
            z = _colmm_gate(w_inv, flat(d), z, flat(gate), jnp.tile(bias[o], cb // C)[None, :], cb)
        else:
            y = d.reshape(2, N2, C)[:, :L].astype(F32)
            zf = z.reshape(2, L, C).astype(F32)
            z = (gate.astype(F32) * (y + zf * bias[o])).astype(BF16).reshape(-1, N2 * C // 2)
    return z.reshape(B, L, C)


def _resid_kernel(x_ref, fT_ref, gt_ref, g_ref, o_ref, *, final):
    xn = x_ref[...] + gt_ref[0] * fT_ref[...].T
    if final:
        xn = xn * lax.rsqrt(jnp.mean(xn * xn, axis=-1, keepdims=True) + EPS) * g_ref[...]
    o_ref[...] = xn


def _resid(x2d, fT, col0, gt, g, rows_per_batch, tm, final):
    T, D = x2d.shape
    tpb = rows_per_batch // tm
    c0 = col0 // tm
    row = lambda i: (i, 0)
    return pl.pallas_call(
        functools.partial(_resid_kernel, final=final),
        grid=(T // tm,),
        in_specs=[pl.BlockSpec((tm, D), row), pl.BlockSpec((D, tm), lambda i: (0, c0 + i)),
                  pl.BlockSpec((1, 1, D), lambda i: (i // tpb, 0, 0)), pl.BlockSpec((1, D), lambda i: (0, 0))],
        out_specs=pl.BlockSpec((tm, D), row),
        out_shape=jax.ShapeDtypeStruct((T, D), F32),
        compiler_params=_cparams(("parallel",)),
        name="residual",
    )(x2d, fT, gt, g)


def _peer(h2, wq, subkeys, u_bf, vT_bf, tc, nb):
    T, D = h2.shape
    hT = h2.T
    wqT = wq.T
    wqT_hi = wqT.astype(BF16)
    wqT_lo = (wqT - wqT_hi.astype(F32)).astype(BF16)
    cnt, e1, r2, e2 = _peer_route(hT, wqT_hi, wqT_lo, subkeys, tc)
    cnt4 = cnt.reshape(PEER_HEADS, PEER_NKEYS // nb, nb, T)
    e14 = e1.reshape(PEER_HEADS, PEER_NKEYS // nb, nb, T)
    return _peer_dense(hT, r2, e2, cnt4, e14, u_bf, vT_bf, tc, nb)


def kernel(x, c, ctx, c_ctx, w_mod, b_mod, g_norm1, g_norm2, w_in, q_gain, k_gain, sc_conv_w, hy_conv_w,
           hy_conv_b, hy_w1, hy_b1, hy_f1, hy_w2, hy_b2, hy_f2, hy_w3, hy_decay, hy_bias, w_br_att, w_br_sc,
           w_br_hy, w_out, peer_wq, peer_subkeys, peer_u, peer_v, g_final):
    B, S, D = x.shape
    Lc = ctx.shape[1]
    hp = lax.Precision.HIGHEST
    cond = jnp.concatenate([jax.nn.silu(c), jnp.broadcast_to(jax.nn.silu(c_ctx), (B, D))], axis=0)
    cos, sin = _rope_tables(S, True)
    cos_c, sin_c = _rope_tables(Lc, False)
    tm, tq, tr = min(TM_MATMUL, S), min(TQ_ATT, S), min(TM_ROWS, S)
    cb = DFT_COLS * HY_W
    tabs = _dft_tables(2 * S // HY_N2, HY_N2)
    tabs_c = _dft_tables(1, 2 * Lc)
    x2 = x.reshape(B * S, D)
    ctx2 = ctx.reshape(B * Lc, D)

    for l in range(DEPTH):
        need_ctx = l < DEPTH - 1
        mod = (jnp.dot(cond, w_mod[l], precision=hp) + b_mod[l]).reshape(2, B, 1, 6, D)
        sh1, sc1, gt1, sh2, sc2, gt2 = (mod[0, :, :, i] for i in range(6))
        csh1, csc1, cgt1, csh2, csc2, cgt2 = (mod[1, :, :, i] for i in range(6))
        w_in_bf = jnp.concatenate([w_in[l][:, REF_GT0:], w_in[l][:, REF_SC0:REF_GT0], w_in[l][:, :REF_SC0]],
                                  axis=1).astype(BF16)
        wa, ws, wh, wo = (w.astype(BF16) for w in (w_br_att[l], w_br_sc[l], w_br_hy[l], w_out[l]))
        g1 = g_norm1[l][None, :]
        g2 = g_norm2[l][None, :]
        hy_params = (hy_w1[l], hy_b1[l], hy_f1[l], hy_w2[l], hy_b2[l], hy_f2[l], hy_w3[l], hy_decay[l])

        p = _in_proj(x2, g1, sc1, sh1, w_in_bf, S, min(TM_IN_PROJ, S), IN_W // 3)
        p3 = p.reshape(B, S, -1)
        pc = _in_proj(ctx2, g1, csc1, csh1, w_in_bf, Lc, Lc, IN_W // 3)
        pc3 = pc.reshape(B, Lc, -1)

        qs, k = _qk_prep(p, cos, sin, q_gain[l], k_gain[l], S, tr)
        qcs, kc = _qk_prep(pc, cos_c, sin_c, q_gain[l], k_gain[l], Lc, Lc)
        k_all = jnp.concatenate([kc.reshape(B, Lc, KV_W), k.reshape(B, S, KV_W)], axis=1)
        v_all = jnp.concatenate([pc3[..., V0:IN_W], p3[..., V0:IN_W]], axis=1)
        y_att = _attention(qs.reshape(B, S, ATT_W), k_all, v_all, tq, _key_block(S + Lc)).reshape(B * S, ATT_W)

        y_sc = _short_conv(p3, sc_conv_w[l], tr).reshape(B * S, SC_W)
        y_hy = _hyena_long(*_hy_pre(p3, hy_conv_w[l], hy_conv_b[l], tr), _hy_filter(S, *hy_params, tr),
                           hy_bias[l], tabs, cb).reshape(B * S, HY_W)
        x2, h2 = _merge(x2, y_att, y_sc, y_hy, p, gt1, g2, sc2, sh2, wa, ws, wh, wo, S, tm)

        if need_ctx:
            yc_att = _attention(qcs.reshape(B, Lc, ATT_W), kc.reshape(B, Lc, KV_W), pc3[..., V0:IN_W], Lc, Lc)
            yc_sc = _short_conv(pc3, sc_conv_w[l], Lc).reshape(B * Lc, SC_W)
            yc_hy = _hyena_long(*_hy_pre(pc3, hy_conv_w[l], hy_conv_b[l], Lc), _hy_filter(Lc, *hy_params, Lc),
                                hy_bias[l], tabs_c, cb).reshape(B * Lc, HY_W)
            ctx2, h2c = _merge(ctx2, yc_att.reshape(B * Lc, ATT_W), yc_sc, yc_hy, pc, cgt1, g2, csc2, csh2,
                               wa, ws, wh, wo, Lc, Lc)
            tok = jnp.concatenate([h2, h2c], axis=0)
        else:
            tok = h2

        u_bf = peer_u[l].astype(BF16)
        vT_bf = peer_v[l].astype(BF16).reshape(-1, PEER_BLOCK * PEER_NKEYS, D).transpose(0, 2, 1)
        fT = _peer(tok, peer_wq[l], peer_subkeys[l], u_bf, vT_bf, min(TC_PEER, tok.shape[0]), PEER_BLOCK)
        last = l == DEPTH - 1
        x2 = _resid(x2, fT, 0, gt2, g_final[None, :], S, tr, last)
        if need_ctx:
            ctx2 = _resid(ctx2, fT, B * S, cgt2, g_final[None, :], Lc, Lc, False)

    return x2.reshape(B, S, D)
```
